```python
import math
import jax, jax.numpy as jnp
from jax import lax
import numpy as np

D_MODEL = 2048
BATCH = 2
SEQ = 16384
DEPTH = 2

CHUNK = 64
Q_BLOCK = 128
TOKEN_BLOCK = 128
RMS_EPS = 1e-6
ROPE_THETA = 500000.0
BRANCH_WIDTH = 512
N_BRANCHES = 4
MLA_HEADS = 8
MLA_Q_RANK = 384
MLA_KV_RANK = 256
MLA_NOPE = 64
MLA_ROPE = 32
MLA_V = 64
FOX_HEADS = 8
FOX_HD = 64
CONV_WIDTH = 512
CONV_TAPS = 3
DIFF_HEADS = 4
DIFF_HD = 64
DIFF_ROT = DIFF_HD // 4
N_EXPERTS = 64
TOP_K = 8
EXPERT_FF = 512
SHARED_FF = 512
ROUTED_SCALE = 2.5

IN_SPLITS = (MLA_Q_RANK, MLA_KV_RANK, MLA_ROPE,
             FOX_HEADS * FOX_HD, FOX_HEADS * FOX_HD, FOX_HEADS * FOX_HD, FOX_HEADS,
             CONV_WIDTH, CONV_WIDTH, CONV_WIDTH,
             DIFF_HEADS * 2 * DIFF_HD, DIFF_HEADS * 2 * DIFF_HD, DIFF_HEADS * 2 * DIFF_HD)
IN_WIDTH = sum(IN_SPLITS)
SPLIT_POINTS = tuple(int(v) for v in np.cumsum(IN_SPLITS)[:-1])

kernel_name = 'hybrid_chunk_causal_mixer_moe_trunk'


def _rmsnorm(x, g, eps=RMS_EPS):
    xf = x.astype(jnp.float32)
    y = xf * lax.rsqrt(jnp.mean(xf * xf, axis=-1, keepdims=True) + eps)
    return (y * g.astype(jnp.float32)).astype(x.dtype)


def _rope(x, pos, rot_dim):
    half = rot_dim // 2
    inv_freq = ROPE_THETA ** (-jnp.arange(half, dtype=jnp.float32) / half)
    ang = pos.astype(jnp.float32)[:, None, :, None] * inv_freq
    cos, sin = jnp.cos(ang), jnp.sin(ang)
    x1 = x[..., :half].astype(jnp.float32)
    x2 = x[..., half:rot_dim].astype(jnp.float32)
    rot = jnp.concatenate([x1 * cos - x2 * sin, x1 * sin + x2 * cos], axis=-1).astype(x.dtype)
    return jnp.concatenate([rot, x[..., rot_dim:]], axis=-1)


def _heads(t, n_heads):
    b, s, _ = t.shape
    return t.reshape(b, s, n_heads, -1).transpose(0, 2, 1, 3)


def _merge(t):
    b, h, s, d = t.shape
    return t.transpose(0, 2, 1, 3).reshape(b, s, h * d)


def _block_attention(q, k, v, per_frame, decay=None):
    b, h, s, dk = q.shape
    dv = v.shape[-1]
    nb = s // Q_BLOCK
    scale = dk ** -0.5
    k_pos = jnp.arange(s)
    k_unit = k_pos if per_frame else k_pos // CHUNK
    q_blocks = q.reshape(b, h, nb, Q_BLOCK, dk).transpose(2, 0, 1, 3, 4)
    if decay is None:
        xs = (jnp.arange(nb), q_blocks)
    else:
        xs = (jnp.arange(nb), q_blocks, decay.reshape(b, h, nb, Q_BLOCK).transpose(2, 0, 1, 3))

    def one_block(args):
        i, q_i = args[0], args[1]
        q_pos = i * Q_BLOCK + jnp.arange(Q_BLOCK)
        q_unit = q_pos if per_frame else q_pos // CHUNK
        logits = jnp.einsum('bhqd,bhkd->bhqk', q_i, k, preferred_element_type=jnp.float32) * scale
        if decay is not None:
            logits = logits + args[2][..., :, None] - decay[..., None, :]
        logits = jnp.where(k_unit[None, :] <= q_unit[:, None], logits, -jnp.inf)
        p = jax.nn.softmax(logits, axis=-1)
        return jnp.einsum('bhqk,bhkd->bhqd', p.astype(v.dtype), v)

    out = lax.map(one_block, xs)
    return out.transpose(1, 2, 0, 3, 4).reshape(b, h, s, dv)


def _mla(cq, ckv, kr, pos, q_norm, w_uq, kv_norm, w_ukv):
    q = _heads(_rmsnorm(cq, q_norm) @ w_uq, MLA_HEADS)
    kv = _heads(_rmsnorm(ckv, kv_norm) @ w_ukv, MLA_HEADS)
    k_nope, v = kv[..., :MLA_NOPE], kv[..., MLA_NOPE:]
    q = jnp.concatenate([q[..., :MLA_NOPE], _rope(q[..., MLA_NOPE:], pos, MLA_ROPE)], axis=-1)
    k_pe = _rope(kr[:, None], pos, MLA_ROPE)
    k = jnp.concatenate([k_nope, jnp.broadcast_to(k_pe, k_nope.shape[:-1] + (MLA_ROPE,))], axis=-1)
    return _merge(_block_attention(q, k, v, per_frame=False))


def _fox(q, k, v, z_f, f_bias):
    log_f = jax.nn.log_sigmoid(z_f.astype(jnp.float32) + f_bias.astype(jnp.float32))
    cum = jnp.cumsum(log_f, axis=1).transpose(0, 2, 1)
    out = _block_attention(_heads(q, FOX_HEADS), _heads(k, FOX_HEADS), _heads(v, FOX_HEADS),
                           per_frame=True, decay=cum)
    return _merge(out)


def _short_conv(b_gate, c_gate, u_in, conv_w):
    u = c_gate * u_in
    s = u.shape[1]
    up = jnp.pad(u, ((0, 0), (CONV_TAPS - 1, 0), (0, 0)))
    y = up[:, 0:s] * conv_w[0]
    for j in range(1, CONV_TAPS):
        y = y + up[:, j:j + s] * conv_w[j]
    return b_gate * y


def _diff_attention(q, k, v, pos, lq1, lk1, lq2, lk2, subln, lam_init):
    q = _heads(q, DIFF_HEADS)
    k = _heads(k, DIFF_HEADS)
    v = _heads(v, DIFF_HEADS)
    q1, q2 = _rope(q[..., :DIFF_HD], pos, DIFF_ROT), _rope(q[..., DIFF_HD:], pos, DIFF_ROT)
    k1, k2 = _rope(k[..., :DIFF_HD], pos, DIFF_ROT), _rope(k[..., DIFF_HD:], pos, DIFF_ROT)
    lam = (jnp.exp(jnp.sum(lq1.astype(jnp.float32) * lk1.astype(jnp.float32)))
           - jnp.exp(jnp.sum(lq2.astype(jnp.float32) * lk2.astype(jnp.float32))) + lam_init)
    o = (_block_attention(q1, k1, v, per_frame=False)
         - lam.astype(v.dtype) * _block_attention(q2, k2, v, per_frame=False))
    o = _rmsnorm(o, subln) * (1.0 - lam_init)
    return _merge(o)


def _moe(h, w_router, router_bias, w_eg, w_eu, w_ed, w_sg, w_su, w_sd):
    b, s, d = h.shape
    t = h.reshape(-1, d)
    scores = jax.nn.sigmoid(jnp.einsum('nd,de->ne', t, w_router, preferred_element_type=jnp.float32))
    _, idx = lax.top_k(scores + router_bias.astype(jnp.float32), TOP_K)
    sel = jnp.take_along_axis(scores, idx, axis=-1)
    wts = sel / jnp.sum(sel, axis=-1, keepdims=True) * ROUTED_SCALE
    gates = jnp.sum(jax.nn.one_hot(idx, N_EXPERTS, dtype=jnp.float32) * wts[..., None], axis=1)
    nblk = t.shape[0] // TOKEN_BLOCK

    def expert_block(args):
        t_b, g_b = args
        a = jnp.einsum('nd,edf->nef', t_b, w_eg)
        u = jnp.einsum('nd,edf->nef', t_b, w_eu)
        hid = jax.nn.silu(a) * u * g_b[..., None].astype(t_b.dtype)
        return jnp.einsum('nef,efd->nd', hid, w_ed)

    routed = lax.map(expert_block, (t.reshape(nblk, TOKEN_BLOCK, d),
                                    gates.reshape(nblk, TOKEN_BLOCK, N_EXPERTS)))
    shared = (jax.nn.silu(t @ w_sg) * (t @ w_su)) @ w_sd
    return (routed.reshape(-1, d) + shared).reshape(b, s, d)


def setup_inputs(seed: int = 0) -> dict:
    key = jax.random.key(seed)
    ks = iter(jax.random.split(key, 40))
    f32 = jnp.float32
    L, D, E = DEPTH, D_MODEL, N_EXPERTS

    def nrm(shape, fan_in, scale=1.0):
        return jax.random.normal(next(ks), shape, f32) * (scale * fan_in ** -0.5)

    def gain(shape):
        return 1.0 + 0.01 * jax.random.normal(next(ks), shape, f32)

    def small(shape, scale=0.01):
        return scale * jax.random.normal(next(ks), shape, f32)

    x = jax.random.normal(next(ks), (BATCH, SEQ, D), f32)
    c = jax.random.normal(next(ks), (BATCH, D), f32)
    offset = jax.random.randint(next(ks), (BATCH, 1), 0, 64, dtype=jnp.int32) * CHUNK
    positions = (offset + jnp.arange(SEQ, dtype=jnp.int32)[None, :]).astype(jnp.int32)
    return {
        'x': x,
        'c': c,
        'positions': positions,
        'ada_w': nrm((L, D, 6 * D), D, 0.5),
        'ada_b': small((L, 6 * D)),
        'norm_mix': gain((L, D)),
        'norm_ffn': gain((L, D)),
        'w_in': nrm((L, D, IN_WIDTH), D),
        'mla_q_norm': gain((L, MLA_Q_RANK)),
        'mla_w_uq': nrm((L, MLA_Q_RANK, MLA_HEADS * (MLA_NOPE + MLA_ROPE)), MLA_Q_RANK),
        'mla_kv_norm': gain((L, MLA_KV_RANK)),
        'mla_w_ukv': nrm((L, MLA_KV_RANK, MLA_HEADS * (MLA_NOPE + MLA_V)), MLA_KV_RANK),
        'fox_f_bias': 3.0 + 0.5 * jax.random.normal(next(ks), (L, FOX_HEADS), f32),
        'conv_w': nrm((L, CONV_TAPS, CONV_WIDTH), CONV_TAPS),
        'diff_lam_q1': small((L, DIFF_HD), 0.1),
        'diff_lam_k1': small((L, DIFF_HD), 0.1),
        'diff_lam_q2': small((L, DIFF_HD), 0.1),
        'diff_lam_k2': small((L, DIFF_HD), 0.1),
        'diff_subln': gain((L, 2 * DIFF_HD)),
        'w_branch': nrm((L, N_BRANCHES, BRANCH_WIDTH, D), BRANCH_WIDTH),
        'w_gate': nrm((L, N_BRANCHES, D, D), D),
        'b_gate': small((L, N_BRANCHES, D)),
        'w_out': nrm((L, D, D), D),
        'w_router': nrm((L, D, E), D),
        'router_bias': small((L, E)),
        'w_exp_gate': nrm((L, E, D, EXPERT_FF), D),
        'w_exp_up': nrm((L, E, D, EXPERT_FF), D),
        'w_exp_down': nrm((L, E, EXPERT_FF, D), EXPERT_FF),
        'w_sh_gate': nrm((L, D, SHARED_FF), D),
        'w_sh_up': nrm((L, D, SHARED_FF), D),
        'w_sh_down': nrm((L, SHARED_FF, D), SHARED_FF),
        'final_norm': gain((D,)),
    }


def reference(x, c, positions, ada_w, ada_b, norm_mix, norm_ffn, w_in, mla_q_norm, mla_w_uq,
              mla_kv_norm, mla_w_ukv, fox_f_bias, conv_w, diff_lam_q1, diff_lam_k1, diff_lam_q2,
              diff_lam_k2, diff_subln, w_branch, w_gate, b_gate, w_out, w_router, router_bias,
              w_exp_gate, w_exp_up, w_exp_down, w_sh_gate, w_sh_up, w_sh_down, final_norm):
    c_act = jax.nn.silu(c)
    for l in range(DEPTH):
        lam_init = 0.8 - 0.6 * math.exp(-0.3 * l)
        mod = c_act @ ada_w[l] + ada_b[l]
        sh_m, sc_m, g_m, sh_f, sc_f, g_f = jnp.split(mod, 6, axis=-1)

        h = _rmsnorm(x, norm_mix[l]) * (1.0 + sc_m[:, None]) + sh_m[:, None]
        proj = h @ w_in[l]
        (a_cq, a_ckv, a_kr, b_q, b_k, b_v, b_zf,
         c_b, c_c, c_u, d_q, d_k, d_v) = jnp.split(proj, SPLIT_POINTS, axis=-1)
        branches = (
            _mla(a_cq, a_ckv, a_kr, positions, mla_q_norm[l], mla_w_uq[l], mla_kv_norm[l], mla_w_ukv[l]),
            _fox(b_q, b_k, b_v, b_zf, fox_f_bias[l]),
            _short_conv(c_b, c_c, c_u, conv_w[l]),
            _diff_attention(d_q, d_k, d_v, positions, diff_lam_q1[l], diff_lam_k1[l],
                            diff_lam_q2[l], diff_lam_k2[l], diff_subln[l], lam_init),
        )
        merged = jnp.zeros_like(x)
        for br, o in enumerate(branches):
            gate = jax.nn.sigmoid(h @ w_gate[l, br] + b_gate[l, br])
            merged = merged + gate * (o @ w_branch[l, br])
        x = x + g_m[:, None] * (merged @ w_out[l])

        h = _rmsnorm(x, norm_ffn[l]) * (1.0 + sc_f[:, None]) + sh_f[:, None]
        x = x + g_f[:, None] * _moe(h, w_router[l], router_bias[l], w_exp_gate[l], w_exp_up[l],
                                    w_exp_down[l], w_sh_gate[l], w_sh_up[l], w_sh_down[l])
    return _rmsnorm(x, final_norm)
```

```python
import functools
import math

import jax
import jax.numpy as jnp
from jax import lax
from jax.experimental import pallas as pl
from jax.experimental.pallas import tpu as pltpu

F32 = jnp.float32
BF16 = jnp.bfloat16
I32 = jnp.int32

CHUNK = 64
CHUNK_SHIFT = 6
RMS_EPS = 1e-6
ROPE_THETA = 500000.0
MLA_HEADS, MLA_Q_RANK, MLA_KV_RANK, MLA_NOPE, MLA_ROPE, MLA_V = 8, 384, 256, 64, 32, 64
FOX_HEADS, FOX_HD = 8, 64
CONV_WIDTH, CONV_TAPS = 512, 3
DIFF_HEADS, DIFF_HD, DIFF_ROT = 4, 64, 16
N_EXPERTS, TOP_K, ROUTED_SCALE = 64, 8, 2.5
BRANCH_WIDTH = 512

LANES = 128
V7X_VMEM_BYTES = 64 * 1024 * 1024
VMEM_LIMIT = V7X_VMEM_BYTES - 8 * 1024 * 1024
NEG_BIG = -1e30

TM_PROJ = 512
TN_PROJ = 512
TM_PREP = 512
TS_CUM = 256
TQ_ATT = 1024
TK_ATT = 512
TM_MERGE = 256
TM_ROUTE = 256
TM_EXP = 256
TM_FFN = 512
TC_COMB = 64

COL_FOXQ, COL_FOXK, COL_FOXV = 0, 512, 1024
COL_CB, COL_CC, COL_CU = 1536, 2048, 2560
COL_DQ, COL_DK, COL_DV = 3072, 3584, 4096
COL_CQ = 4608
COL_KR = 4992
COL_CKV = 5120
COL_ZF = 5376
P_WIDTH = 5632


def _cparams(*sem):
    return pltpu.CompilerParams(dimension_semantics=sem, vmem_limit_bytes=VMEM_LIMIT)


def _normmod(x, g, sc, sh):
    ms = jnp.mean(x * x, axis=-1, keepdims=True)
    return (x * lax.rsqrt(ms + RMS_EPS)) * g * (1.0 + sc) + sh


def _rope_lanes(x, cos_t, sin_t, lo, half, period):
    width = x.shape[1]
    lane = lax.broadcasted_iota(I32, x.shape, 1) & (period - 1)
    x_up = pltpu.roll(x, width - half, 1)
    x_dn = pltpu.roll(x, half, 1)
    rot = jnp.where(lane < lo, 0.0,
                    jnp.where(lane < lo + half, -x_up,
                              jnp.where(lane < lo + 2 * half, x_dn, 0.0)))
    return x * cos_t + rot * sin_t


def _mod_kernel(c_ref, w_ref, b_ref, o_ref):
    c = c_ref[...]
    ca = (c * jax.nn.sigmoid(c)).astype(BF16)
    o_ref[0] = jnp.dot(ca, w_ref[0].astype(BF16), preferred_element_type=F32) + b_ref[0]


def _modulation(c, ada_w, ada_b):
    n_layers, d, d6 = ada_w.shape
    b = c.shape[0]
    rows = 8
    cp = jnp.zeros((rows, d), F32).at[:b].set(c)
    tn = 1024
    return pl.pallas_call(
        _mod_kernel,
        grid=(n_layers, d6 // tn),
        in_specs=[pl.BlockSpec((rows, d), lambda l, j: (0, 0)),
                  pl.BlockSpec((1, d, tn), lambda l, j: (l, 0, j)),
                  pl.BlockSpec((1, 1, tn), lambda l, j: (l, 0, j))],
        out_specs=pl.BlockSpec((1, rows, tn), lambda l, j: (l, 0, j)),
        out_shape=jax.ShapeDtypeStruct((n_layers, rows, d6), F32),
        compiler_params=_cparams("parallel", "parallel"),
        name="adaln_modulation",
    )(cp, ada_w, ada_b.reshape(n_layers, 1, d6))


def _proj_kernel(x_ref, g_ref, sc_ref, sh_ref, w_ref, b_ref, o_ref, h_scr, *, sigmoid):
    @pl.when(pl.program_id(1) == 0)
    def _():
        h_scr[...] = _normmod(x_ref[...], g_ref[...], sc_ref[0], sh_ref[0]).astype(BF16)

    acc = jnp.dot(h_scr[...], w_ref[...], preferred_element_type=F32)
    if sigmoid:
        acc = jax.nn.sigmoid(acc + b_ref[...])
    o_ref[...] = acc.astype(o_ref.dtype)


def _norm_proj(x, g, sc, sh, w, bias, seq, *, sigmoid, name):
    n, d = x.shape
    cols = w.shape[1]
    tm, tn = min(TM_PROJ, seq), TN_PROJ
    per_b = seq // tm
    return pl.pallas_call(
        functools.partial(_proj_kernel, sigmoid=sigmoid),
        grid=(n // tm, cols // tn),
        in_specs=[pl.BlockSpec((tm, d), lambda i, j: (i, 0)),
                  pl.BlockSpec((1, d), lambda i, j: (0, 0)),
                  pl.BlockSpec((1, 1, d), lambda i, j: (i // per_b, 0, 0)),
                  pl.BlockSpec((1, 1, d), lambda i, j: (i // per_b, 0, 0)),
                  pl.BlockSpec((d, tn), lambda i, j: (0, j)),
                  pl.BlockSpec((1, tn), lambda i, j: (0, j))],
        out_specs=pl.BlockSpec((tm, tn), lambda i, j: (i, j)),
        out_shape=jax.ShapeDtypeStruct((n, cols), BF16),
        scratch_shapes=[pltpu.VMEM((tm, d), BF16)],
        compiler_params=_cparams("parallel", "arbitrary"),
        name=name,
    )(x, g, sc, sh, w, bias)


def _mla_prep_kernel(cq_ref, ckv_ref, kr_ref, cos_ref, sin_ref, qn_ref, kvn_ref,
                     wq_ref, wk_ref, wv_ref, q_ref, k_ref, v_ref, *, scale):
    cos_t, sin_t = cos_ref[...], sin_ref[...]

    def rms(t, gain):
        t = t.astype(F32)
        return (t * lax.rsqrt(jnp.mean(t * t, axis=-1, keepdims=True) + RMS_EPS) * gain).astype(BF16)

    def rope(t):
        return _rope_lanes(t, cos_t, sin_t, MLA_NOPE, MLA_ROPE // 2, LANES)

    qn = rms(cq_ref[...], qn_ref[...])
    kvn = rms(ckv_ref[...], kvn_ref[...])
    q = jnp.dot(qn, wq_ref[...], preferred_element_type=F32)
    k = jnp.dot(kvn, wk_ref[...], preferred_element_type=F32)
    v_ref[...] = jnp.dot(kvn, wv_ref[...], preferred_element_type=F32).astype(BF16)
    k_pe = rope(kr_ref[...].astype(F32))
    for h in range(MLA_HEADS):
        sl = slice(h * LANES, (h + 1) * LANES)
        q_ref[:, sl] = (rope(q[:, sl]) * scale).astype(BF16)
        k_ref[:, sl] = (k[:, sl] + k_pe).astype(BF16)


def _mla_prep(p, cos_t, sin_t, q_norm, kv_norm, wq, wk, wv):
    n = p.shape[0]
    tm = min(TM_PREP, n)
    hq = MLA_HEADS * LANES
    hv = MLA_HEADS * MLA_V
    scale = float(MLA_NOPE + MLA_ROPE) ** -0.5
    const = lambda i: (0, 0)
    return pl.pallas_call(
        functools.partial(_mla_prep_kernel, scale=scale),
        grid=(n // tm,),
        in_specs=[pl.BlockSpec((tm, MLA_Q_RANK), lambda i: (i, COL_CQ // MLA_Q_RANK)),
                  pl.BlockSpec((tm, MLA_KV_RANK), lambda i: (i, COL_CKV // MLA_KV_RANK)),
                  pl.BlockSpec((tm, LANES), lambda i: (i, COL_KR // LANES)),
                  pl.BlockSpec((tm, LANES), lambda i: (i, 0)),
                  pl.BlockSpec((tm, LANES), lambda i: (i, 0)),
                  pl.BlockSpec((1, MLA_Q_RANK), const),
                  pl.BlockSpec((1, MLA_KV_RANK), const),
                  pl.BlockSpec((MLA_Q_RANK, hq), const),
                  pl.BlockSpec((MLA_KV_RANK, hq), const),
                  pl.BlockSpec((MLA_KV_RANK, hv), const)],
        out_specs=[pl.BlockSpec((tm, hq), lambda i: (i, 0)),
                   pl.BlockSpec((tm, hq), lambda i: (i, 0)),
                   pl.BlockSpec((tm, hv), lambda i: (i, 0))],
        out_shape=[jax.ShapeDtypeStruct((n, hq), BF16),
                   jax.ShapeDtypeStruct((n, hq), BF16),
                   jax.ShapeDtypeStruct((n, hv), BF16)],
        compiler_params=_cparams("parallel"),
        name="mla_prep",
    )(p, p, p, cos_t, sin_t, q_norm, kv_norm, wq, wk, wv)


def _diff_rope_kernel(q_ref, k_ref, cos_ref, sin_ref, qo_ref, ko_ref):
    cos_t, sin_t = cos_ref[...], sin_ref[...]
    for src, dst in ((q_ref, qo_ref), (k_ref, ko_ref)):
        for h in range(DIFF_HEADS):
            sl = slice(h * LANES, (h + 1) * LANES)
            t = src[:, sl].astype(F32)
            dst[:, sl] = _rope_lanes(t, cos_t, sin_t, 0, DIFF_ROT // 2, DIFF_HD).astype(BF16)


def _diff_rope(p, cos_t, sin_t):
    n = p.shape[0]
    tm = min(TM_PREP, n)
    w = DIFF_HEADS * 2 * DIFF_HD
    return pl.pallas_call(
        _diff_rope_kernel,
        grid=(n // tm,),
        in_specs=[pl.BlockSpec((tm, w), lambda i: (i, COL_DQ // w)),
                  pl.BlockSpec((tm, w), lambda i: (i, COL_DK // w)),
                  pl.BlockSpec((tm, LANES), lambda i: (i, 0)),
                  pl.BlockSpec((tm, LANES), lambda i: (i, 0))],
        out_specs=[pl.BlockSpec((tm, w), lambda i: (i, 0)),
                   pl.BlockSpec((tm, w), lambda i: (i, 0))],
        out_shape=[jax.ShapeDtypeStruct((n, w), BF16)] * 2,
        compiler_params=_cparams("parallel"),
        name="diff_rope",
    )(p, p, cos_t, sin_t)


def _foxcum_kernel(z_ref, b_ref, tri_ref, o_ref, carry):
    @pl.when(pl.program_id(1) == 0)
    def _():
        carry[...] = jnp.zeros_like(carry)

    z = z_ref[...].astype(F32) + b_ref[...]
    log_f = jnp.minimum(z, 0.0) - jnp.log(1.0 + jnp.exp(-jnp.abs(z)))
    cum = jnp.dot(tri_ref[...], log_f, precision=lax.Precision.HIGHEST,
                  preferred_element_type=F32) + carry[...]
    ts = cum.shape[0]
    carry[...] = cum[ts - 1:ts, :]
    o_ref[0] = (-cum).T[:FOX_HEADS, :]


def _fox_neg_cum(p, f_bias, batch, seq):
    ts = min(TS_CUM, seq)
    per_b = seq // ts
    bias = jnp.zeros((1, LANES), F32).at[0, :FOX_HEADS].set(f_bias)
    tri = jnp.tril(jnp.ones((ts, ts), F32))
    return pl.pallas_call(
        _foxcum_kernel,
        grid=(batch, per_b),
        in_specs=[pl.BlockSpec((ts, LANES), lambda b, j: (b * per_b + j, COL_ZF // LANES)),
                  pl.BlockSpec((1, LANES), lambda b, j: (0, 0)),
                  pl.BlockSpec((ts, ts), lambda b, j: (0, 0))],
        out_specs=pl.BlockSpec((1, FOX_HEADS, ts), lambda b, j: (b, 0, j)),
        out_shape=jax.ShapeDtypeStruct((batch, FOX_HEADS, seq), F32),
        scratch_shapes=[pltpu.VMEM((1, LANES), F32)],
        compiler_params=_cparams("parallel", "arbitrary"),
        name="fox_cumgate",
    )(p, bias, tri)


def _attn_kernel(*refs, mode, tq, tk, per_frame, lam_init):
    if mode == "mla":
        qa_ref, qb_ref, ka_ref, kb_ref, v_ref, o_ref, q_scr, m_scr, l_scr, acc_scr = refs
        k_refs = (ka_ref, kb_ref)
    elif mode == "fox":
        q_ref, k_ref, v_ref, nf_ref, o_ref, q_scr, m_scr, l_scr, acc_scr = refs
        k_refs = (k_ref, k_ref)
    else:
        q_ref, k_ref, v_ref, lam_ref, subln_ref, o_ref, q_scr, m_scr, l_scr, acc_scr = refs
        k_refs = (k_ref, k_ref)

    i = pl.program_id(2)
    lane = lax.broadcasted_iota(I32, (tq, LANES), 1)
    if mode == "mla":
        q_scr[0] = qa_ref[...]
        q_scr[1] = qb_ref[...]
    else:
        q = q_ref[...]
        zero = jnp.zeros_like(q)
        q_scr[0] = jnp.where(lane < FOX_HD, q, zero)
        q_scr[1] = jnp.where(lane < FOX_HD, zero, q)
    m_scr[...] = jnp.full(m_scr.shape, NEG_BIG, F32)
    l_scr[...] = jnp.zeros(l_scr.shape, F32)
    acc_scr[...] = jnp.zeros(acc_scr.shape, F32)

    def step(j, masked):
        ks = pl.multiple_of(j * tk, tk)
        v = v_ref[pl.ds(ks, tk), :]
        if masked:
            q_pos = i * tq + lax.broadcasted_iota(I32, (tq, tk), 0)
            k_pos = ks + lax.broadcasted_iota(I32, (tq, tk), 1)
            if per_frame:
                visible = k_pos <= q_pos
            else:
                visible = (k_pos >> CHUNK_SHIFT) <= (q_pos >> CHUNK_SHIFT)
        for st in range(2):
            k = k_refs[st][pl.ds(ks, tk), :]
            s = lax.dot_general(q_scr[st], k, (((1,), (1,)), ((), ())), preferred_element_type=F32)
            if mode == "fox":
                s = s + nf_ref[0, st:st + 1, pl.ds(ks, tk)]
            if masked:
                s = jnp.where(visible, s, NEG_BIG)
            m_prev = m_scr[st]
            m_new = jnp.maximum(m_prev, jnp.max(s, axis=1, keepdims=True))
            alpha = jnp.exp(m_prev - m_new)
            p = jnp.exp(s - m_new)
            l_scr[st] = alpha * l_scr[st] + jnp.sum(p, axis=1, keepdims=True)
            acc_scr[st] = alpha * acc_scr[st] + jnp.dot(p.astype(BF16), v, preferred_element_type=F32)
            m_scr[st] = m_new

    n_diag = tq // tk
    n_full = i * n_diag

    def full_body(j, carry):
        step(j, False)
        return carry

    lax.fori_loop(0, n_full, full_body, 0)
    for r in range(n_diag):
        step(n_full + r, True)

    oa = acc_scr[0] / l_scr[0]
    ob = acc_scr[1] / l_scr[1]
    if mode == "diff":
        lv = lam_ref[...]
        lam = (jnp.exp(jnp.sum(lv[0:1] * lv[1:2], axis=1, keepdims=True))
               - jnp.exp(jnp.sum(lv[2:3] * lv[3:4], axis=1, keepdims=True)) + lam_init)
        o = oa - lam * ob
        o = o * lax.rsqrt(jnp.mean(o * o, axis=-1, keepdims=True) + RMS_EPS) * subln_ref[...]
        o = o * (1.0 - lam_init)
    else:
        o = jnp.where(lane < FOX_HD, oa, ob)
    o_ref[...] = o.astype(BF16)


def _attention(mode, batch, seq, *, q, k, v, q_col=0, k_col=0, v_col=0, extra=(), lam_init=0.0):
    n = batch * seq
    tq = min(TQ_ATT, seq)
    tk = min(TK_ATT, tq)
    nq = seq // tq
    pairs = 4
    grid = (batch, pairs, nq)
    q_spec = lambda off, mul: pl.BlockSpec((tq, LANES), lambda b, p, i: (b * nq + i, off + mul * p))
    kv_spec = lambda off, mul: pl.BlockSpec((seq, LANES), lambda b, p, i: (b, off + mul * p))
    if mode == "mla":
        in_specs = [q_spec(q_col, 2), q_spec(q_col + 1, 2), kv_spec(k_col, 2), kv_spec(k_col + 1, 2),
                    kv_spec(v_col, 1)]
        args = (q, q, k, k, v)
    elif mode == "fox":
        in_specs = [q_spec(q_col, 1), kv_spec(k_col, 1), kv_spec(v_col, 1),
                    pl.BlockSpec((1, 2, seq), lambda b, p, i: (b * pairs + p, 0, 0))]
        args = (q, k, v) + tuple(extra)
    else:
        in_specs = [q_spec(q_col, 1), kv_spec(k_col, 1), kv_spec(v_col, 1),
                    pl.BlockSpec((4, DIFF_HD), lambda b, p, i: (0, 0)),
                    pl.BlockSpec((1, LANES), lambda b, p, i: (0, 0))]
        args = (q, k, v) + tuple(extra)
    return pl.pallas_call(
        functools.partial(_attn_kernel, mode=mode, tq=tq, tk=tk, per_frame=(mode == "fox"),
                          lam_init=lam_init),
        grid=grid,
        in_specs=in_specs,
        out_specs=pl.BlockSpec((tq, LANES), lambda b, p, i: (b * nq + i, p)),
        out_shape=jax.ShapeDtypeStruct((n, pairs * LANES), BF16),
        scratch_shapes=[pltpu.VMEM((2, tq, LANES), BF16),
                        pltpu.VMEM((2, tq, 1), F32),
                        pltpu.VMEM((2, tq, 1), F32),
                        pltpu.VMEM((2, tq, LANES), F32)],
        compiler_params=_cparams("parallel", "parallel", "arbitrary"),
        name="attn_" + mode,
    )(*args)


def _conv_kernel(b_ref, c_ref, u_ref, ch_ref, uh_ref, w_ref, o_ref, *, per_b):
    u = c_ref[...].astype(F32) * u_ref[...].astype(F32)
    halo = ch_ref[...].astype(F32) * uh_ref[...].astype(F32)
    halo = jnp.where(pl.program_id(0) % per_b == 0, 0.0, halo)
    hr = halo.shape[0]
    row = lax.broadcasted_iota(I32, u.shape, 0)
    u1 = jnp.where(row == 0, halo[hr - 1:hr, :], pltpu.roll(u, 1, 0))
    u2 = jnp.where(row == 0, halo[hr - 2:hr - 1, :],
                   jnp.where(row == 1, halo[hr - 1:hr, :], pltpu.roll(u, 2, 0)))
    w = w_ref[...]
    y = u2 * w[0:1, :] + u1 * w[1:2, :] + u * w[2:3, :]
    o_ref[...] = (b_ref[...].astype(F32) * y).astype(BF16)


def _short_conv(p, conv_w, seq):
    n = p.shape[0]
    ts = min(TM_PREP, seq)
    per_b = seq // ts
    w = CONV_WIDTH
    hr = 16
    halo_map = lambda col: (lambda i: (jnp.maximum(i * (ts // hr) - 1, 0), col // w))
    return pl.pallas_call(
        functools.partial(_conv_kernel, per_b=per_b),
        grid=(n // ts,),
        in_specs=[pl.BlockSpec((ts, w), lambda i: (i, COL_CB // w)),
                  pl.BlockSpec((ts, w), lambda i: (i, COL_CC // w)),
                  pl.BlockSpec((ts, w), lambda i: (i, COL_CU // w)),
                  pl.BlockSpec((hr, w), halo_map(COL_CC)),
                  pl.BlockSpec((hr, w), halo_map(COL_CU)),
                  pl.BlockSpec((CONV_TAPS, w), lambda i: (0, 0))],
        out_specs=pl.BlockSpec((ts, w), lambda i: (i, 0)),
        out_shape=jax.ShapeDtypeStruct((n, w), BF16),
        compiler_params=_cparams("parallel"),
        name="short_conv",
    )(p, p, p, p, p, conv_w)


def _merge_kernel(o0, o1, o2, o3, g0, g1, g2, g3, wb_ref, wo_ref, x_ref, gm_ref, out_ref):
    merged = None
    for br, (o_ref, g_ref) in enumerate(((o0, g0), (o1, g1), (o2, g2), (o3, g3))):
        t = jnp.dot(o_ref[...], wb_ref[br], preferred_element_type=F32) * g_ref[...].astype(F32)
        merged = t if merged is None else merged + t
    y = jnp.dot(merged.astype(BF16), wo_ref[...], preferred_element_type=F32)
    out_ref[...] = x_ref[...] + gm_ref[0] * y


def _merge(branches, gates, w_branch, w_out, x, g_m, seq):
    n, d = x.shape
    tm = min(TM_MERGE, seq)
    per_b = seq // tm
    bw = BRANCH_WIDTH
    o_spec = pl.BlockSpec((tm, bw), lambda i: (i, 0))
    g_specs = [pl.BlockSpec((tm, d), lambda i, br=br: (i, br)) for br in range(4)]
    single = pl.Buffered(1)
    return pl.pallas_call(
        _merge_kernel,
        grid=(n // tm,),
        in_specs=[o_spec] * 4 + g_specs + [
            pl.BlockSpec((4, bw, d), lambda i: (0, 0, 0), pipeline_mode=single),
            pl.BlockSpec((d, d), lambda i: (0, 0), pipeline_mode=single),
            pl.BlockSpec((tm, d), lambda i: (i, 0)),
            pl.BlockSpec((1, 1, d), lambda i: (i // per_b, 0, 0))],
        out_specs=pl.BlockSpec((tm, d), lambda i: (i, 0)),
        out_shape=jax.ShapeDtypeStruct((n, d), F32),
        compiler_params=_cparams("parallel"),
        name="branch_merge",
    )(*branches, gates, gates, gates, gates, w_branch, w_out, x, g_m)


def _router_kernel(x_ref, g_ref, sc_ref, sh_ref, wr_ref, rb_ref, h_ref, idx_ref, wts_ref):
    h = _normmod(x_ref[...], g_ref[...], sc_ref[0], sh_ref[0])
    h_ref[...] = h
    logits = jnp.dot(h, wr_ref[...], precision=lax.Precision.HIGHEST, preferred_element_type=F32)
    scores = jax.nn.sigmoid(logits)
    work = scores + rb_ref[...]
    tm, ne = scores.shape
    lane = lax.broadcasted_iota(I32, (tm, ne), 1).astype(F32)
    slot = lax.broadcasted_iota(I32, (tm, TOP_K), 1)
    idx = jnp.zeros((tm, TOP_K), F32)
    sel = jnp.zeros((tm, TOP_K), F32)
    for kk in range(TOP_K):
        mx = jnp.max(work, axis=1, keepdims=True)
        first = jnp.min(jnp.where(work == mx, lane, float(ne)), axis=1, keepdims=True)
        hit = lane == first
        sc_k = jnp.sum(jnp.where(hit, scores, 0.0), axis=1, keepdims=True)
        work = jnp.where(hit, -jnp.inf, work)
        idx = jnp.where(slot == kk, first, idx)
        sel = jnp.where(slot == kk, sc_k, sel)
    idx_ref[...] = idx.astype(I32)
    wts_ref[...] = sel / jnp.sum(sel, axis=1, keepdims=True) * ROUTED_SCALE


def _router(x, g, sc, sh, w_router, router_bias, seq):
    n, d = x.shape
    tm = min(TM_ROUTE, seq)
    per_b = seq // tm
    ne = w_router.shape[1]
    return pl.pallas_call(
        _router_kernel,
        grid=(n // tm,),
        in_specs=[pl.BlockSpec((tm, d), lambda i: (i, 0)),
                  pl.BlockSpec((1, d), lambda i: (0, 0)),
                  pl.BlockSpec((1, 1, d), lambda i: (i // per_b, 0, 0)),
                  pl.BlockSpec((1, 1, d), lambda i: (i // per_b, 0, 0)),
                  pl.BlockSpec((d, ne), lambda i: (0, 0)),
                  pl.BlockSpec((1, ne), lambda i: (0, 0))],
        out_specs=[pl.BlockSpec((tm, d), lambda i: (i, 0)),
                   pl.BlockSpec((tm, TOP_K), lambda i: (i, 0)),
                   pl.BlockSpec((tm, TOP_K), lambda i: (i, 0))],
        out_shape=[jax.ShapeDtypeStruct((n, d), F32),
                   jax.ShapeDtypeStruct((n, TOP_K), I32),
                   jax.ShapeDtypeStruct((n, TOP_K), F32)],
        compiler_params=_cparams("parallel"),
        name="moe_router",
    )(x, g, sc, sh, w_router, router_bias)


def _gather_step(i, n_steps, idx_hbm, idx_smem, idx_sem, data_hbm, buf, data_sem, n_rows):
    def idx_copy(step, slot):
        return pltpu.make_async_copy(idx_hbm.at[step], idx_smem.at[pl.ds(slot, 1)], idx_sem.at[slot])

    def issue_rows(slot):
        def body(r, carry):
            row = idx_smem[slot, r]
            pltpu.make_async_copy(data_hbm.at[pl.ds(row, 1)], buf.at[slot, pl.ds(r, 1)],
                                  data_sem.at[slot]).start()
            return carry
        lax.fori_loop(0, n_rows, body, 0)

    slot = i % 2
    nxt = 1 - slot

    @pl.when(i == 0)
    def _():
        idx_copy(0, 0).start()
        idx_copy(0, 0).wait()
        issue_rows(0)

        if n_steps > 1:
            idx_copy(1, 1).start()

    @pl.when(i + 1 < n_steps)
    def _():
        idx_copy(i + 1, nxt).wait()
        issue_rows(nxt)

    @pl.when(i + 2 < n_steps)
    def _():
        idx_copy(i + 2, slot).start()

    pltpu.make_async_copy(data_hbm.at[pl.ds(0, n_rows)], buf.at[slot], data_sem.at[slot]).wait()
    return slot


def _expert_kernel(te_ref, src_hbm, h_hbm, wg_ref, wu_ref, wd_ref, gw_ref, y_ref,
                   idx_smem, idx_sem, xbuf, x_sem, wg_s, wu_s, wd_s, *, n_steps, tm):
    i = pl.program_id(0)
    slot = _gather_step(i, n_steps, src_hbm, idx_smem, idx_sem, h_hbm, xbuf, x_sem, tm)

    new_expert = jnp.logical_or(i == 0, te_ref[i] != te_ref[jnp.maximum(i - 1, 0)])

    @pl.when(new_expert)
    def _():
        wg_s[...] = wg_ref[0, 0].astype(BF16)
        wu_s[...] = wu_ref[0, 0].astype(BF16)
        wd_s[...] = wd_ref[0, 0].astype(BF16)

    x = xbuf[slot].astype(BF16)
    a = jnp.dot(x, wg_s[...], preferred_element_type=F32)
    u = jnp.dot(x, wu_s[...], preferred_element_type=F32)
    hid = (a * jax.nn.sigmoid(a)) * u * gw_ref[0]
    y_ref[...] = jnp.dot(hid.astype(BF16), wd_s[...], preferred_element_type=F32)


def _expert_ffn(layer, tile_expert, src, h, w_eg, w_eu, w_ed, gw):
    n_steps, _, tm = src.shape
    d = h.shape[1]
    ff = w_eg.shape[3]
    grid_spec = pltpu.PrefetchScalarGridSpec(
        num_scalar_prefetch=1,
        grid=(n_steps,),
        in_specs=[pl.BlockSpec(memory_space=pl.ANY),
                  pl.BlockSpec(memory_space=pl.ANY),
                  pl.BlockSpec((1, 1, d, ff), lambda i, te: (layer, te[i], 0, 0)),
                  pl.BlockSpec((1, 1, d, ff), lambda i, te: (layer, te[i], 0, 0)),
                  pl.BlockSpec((1, 1, ff, d), lambda i, te: (layer, te[i], 0, 0)),
                  pl.BlockSpec((1, tm, 1), lambda i, te: (i, 0, 0))],
        out_specs=pl.BlockSpec((tm, d), lambda i, te: (i, 0)),
        scratch_shapes=[pltpu.SMEM((2, tm), I32),
                        pltpu.SemaphoreType.DMA((2,)),
                        pltpu.VMEM((2, tm, d), F32),
                        pltpu.SemaphoreType.DMA((2,)),
                        pltpu.VMEM((d, ff), BF16),
                        pltpu.VMEM((d, ff), BF16),
                        pltpu.VMEM((ff, d), BF16)],
    )
    return pl.pallas_call(
        functools.partial(_expert_kernel, n_steps=n_steps, tm=tm),
        grid_spec=grid_spec,
        out_shape=jax.ShapeDtypeStruct((n_steps * tm, d), F32),
        compiler_params=_cparams("arbitrary"),
        name="moe_experts",
    )(tile_expert, src, h, w_eg, w_eu, w_ed, gw)


def _shared_kernel(h_ref, wg_ref, wu_ref, wd_ref, o_ref):
    h = h_ref[...].astype(BF16)
    a = jnp.dot(h, wg_ref[...], preferred_element_type=F32)
    u = jnp.dot(h, wu_ref[...], preferred_element_type=F32)
    hid = ((a * jax.nn.sigmoid(a)) * u).astype(BF16)
    o_ref[...] = jnp.dot(hid, wd_ref[...], preferred_element_type=F32)


def _shared_ffn(h, wg, wu, wd):
    n, d = h.shape
    ff = wg.shape[1]
    tm = min(TM_FFN, n)
    return pl.pallas_call(
        _shared_kernel,
        grid=(n // tm,),
        in_specs=[pl.BlockSpec((tm, d), lambda i: (i, 0)),
                  pl.BlockSpec((d, ff), lambda i: (0, 0)),
                  pl.BlockSpec((d, ff), lambda i: (0, 0)),
                  pl.BlockSpec((ff, d), lambda i: (0, 0))],
        out_specs=pl.BlockSpec((tm, d), lambda i: (i, 0)),
        out_shape=jax.ShapeDtypeStruct((n, d), F32),
        compiler_params=_cparams("parallel"),
        name="moe_shared",
    )(h, wg, wu, wd)


def _combine_kernel(dest_hbm, y_hbm, sh_ref, x_ref, gf_ref, fn_ref, o_ref,
                    idx_smem, idx_sem, ybuf, y_sem, *, n_steps, tc, final_norm):
    i = pl.program_id(0)
    slot = _gather_step(i, n_steps, dest_hbm, idx_smem, idx_sem, y_hbm, ybuf, y_sem, TOP_K * tc)
    moe = sh_ref[...]
    for kk in range(TOP_K):
        moe = moe + ybuf[slot, kk * tc:(kk + 1) * tc, :]
    out = x_ref[...] + gf_ref[0] * moe
    if final_norm:
        out = out * lax.rsqrt(jnp.mean(out * out, axis=-1, keepdims=True) + RMS_EPS) * fn_ref[...]
    o_ref[...] = out


def _combine(dest, y, shared, x, g_f, final_gain, seq, *, final_norm):
    n, d = x.shape
    n_steps, _, rows = dest.shape
    tc = rows // TOP_K
    per_b = seq // tc
    return pl.pallas_call(
        functools.partial(_combine_kernel, n_steps=n_steps, tc=tc, final_norm=final_norm),
        grid=(n_steps,),
        in_specs=[pl.BlockSpec(memory_space=pl.ANY),
                  pl.BlockSpec(memory_space=pl.ANY),
                  pl.BlockSpec((tc, d), lambda i: (i, 0)),
                  pl.BlockSpec((tc, d), lambda i: (i, 0)),
                  pl.BlockSpec((1, 1, d), lambda i: (i // per_b, 0, 0)),
                  pl.BlockSpec((1, d), lambda i: (0, 0))],
        out_specs=pl.BlockSpec((tc, d), lambda i: (i, 0)),
        out_shape=jax.ShapeDtypeStruct((n, d), F32),
        scratch_shapes=[pltpu.SMEM((2, rows), I32),
                        pltpu.SemaphoreType.DMA((2,)),
                        pltpu.VMEM((2, rows, d), F32),
                        pltpu.SemaphoreType.DMA((2,))],
        compiler_params=_cparams("arbitrary"),
        name="moe_combine",
    )(dest, y, shared, x, g_f, final_gain)


def _rope_tables(positions):
    pos = positions.reshape(-1).astype(F32)
    n = pos.shape[0]

    def cos_sin(half):
        inv_freq = ROPE_THETA ** (-jnp.arange(half, dtype=F32) / half)
        ang = pos[:, None] * inv_freq
        return jnp.cos(ang), jnp.sin(ang)

    cm, sm = cos_sin(MLA_ROPE // 2)
    tail = LANES - MLA_NOPE - MLA_ROPE
    mla_cos = jnp.concatenate([jnp.ones((n, MLA_NOPE), F32), cm, cm, jnp.ones((n, tail), F32)], axis=1)
    mla_sin = jnp.concatenate([jnp.zeros((n, MLA_NOPE), F32), sm, sm, jnp.zeros((n, tail), F32)], axis=1)
    cd, sd = cos_sin(DIFF_ROT // 2)
    rest = DIFF_HD - DIFF_ROT
    dc = jnp.concatenate([cd, cd, jnp.ones((n, rest), F32)], axis=1)
    ds = jnp.concatenate([sd, sd, jnp.zeros((n, rest), F32)], axis=1)
    return mla_cos, mla_sin, jnp.concatenate([dc, dc], axis=1), jnp.concatenate([ds, ds], axis=1)


def _in_proj_weight(w_in):
    d = w_in.shape[0]
    sizes = (MLA_Q_RANK, MLA_KV_RANK, MLA_ROPE, 512, 512, 512, FOX_HEADS, 512, 512, 512, 512, 512, 512)
    parts, off = [], 0
    for s in sizes:
        parts.append(w_in[:, off:off + s])
        off += s
    a_cq, a_ckv, a_kr, b_q, b_k, b_v, b_zf, c_b, c_c, c_u, d_q, d_k, d_v = parts
    zeros = lambda w: jnp.zeros((d, w), w_in.dtype)
    q_scale = FOX_HD ** -0.5
    cols = [b_q * q_scale, b_k, b_v, c_b, c_c, c_u, d_q * q_scale, d_k, d_v, a_cq,
            zeros(MLA_NOPE), a_kr, zeros(LANES - MLA_NOPE - MLA_ROPE),
            a_ckv, b_zf, zeros(LANES - FOX_HEADS), zeros(P_WIDTH - COL_ZF - LANES)]
    return jnp.concatenate(cols, axis=1).astype(BF16)


def _dispatch_plan(idx, wts, tm):
    n = idx.shape[0]
    onehot = (idx[:, :, None] == jnp.arange(N_EXPERTS, dtype=I32)[None, None, :]).astype(I32).sum(axis=1)
    before = jnp.cumsum(onehot, axis=0) - onehot
    counts = onehot.sum(axis=0)
    padded = (counts + tm - 1) // tm * tm
    ends = jnp.cumsum(padded)
    starts = ends - padded
    rank = jnp.take_along_axis(before, idx, axis=1)
    dest = starts[idx] + rank
    n_steps = (n * TOP_K) // tm + N_EXPERTS
    rows = n_steps * tm
    flat = dest.reshape(-1)
    tok = jnp.repeat(jnp.arange(n, dtype=I32), TOP_K)
    src = jnp.zeros((rows,), I32).at[flat].set(tok)
    gw = jnp.zeros((rows,), F32).at[flat].set(wts.reshape(-1))
    tile_start = jnp.arange(n_steps, dtype=I32) * tm
    tile_expert = jnp.minimum(jnp.searchsorted(ends, tile_start, side="right"), N_EXPERTS - 1).astype(I32)
    return dest.astype(I32), src.reshape(n_steps, 1, tm), gw.reshape(n_steps, tm, 1), tile_expert


def kernel(x, c, positions, ada_w, ada_b, norm_mix, norm_ffn, w_in, mla_q_norm, mla_w_uq, mla_kv_norm,
           mla_w_ukv, fox_f_bias, conv_w, diff_lam_q1, diff_lam_k1, diff_lam_q2, diff_lam_k2, diff_subln,
           w_branch, w_gate, b_gate, w_out, w_router, router_bias, w_exp_gate, w_exp_up, w_exp_down,
           w_sh_gate, w_sh_up, w_sh_down, final_norm):
    batch, seq, d = x.shape
    n = batch * seq
    depth = ada_w.shape[0]
    xf = x.reshape(n, d)

    mod = _modulation(c, ada_w, ada_b)
    mla_cos, mla_sin, diff_cos, diff_sin = _rope_tables(positions)

    for l in range(depth):
        lam_init = 0.8 - 0.6 * math.exp(-0.3 * l)
        mods = mod[l, :batch].reshape(batch, 6, 1, d)
        sh_m, sc_m, g_m, sh_f, sc_f, g_f = (mods[:, t] for t in range(6))

        w_p = _in_proj_weight(w_in[l])
        p = _norm_proj(xf, norm_mix[l][None], sc_m, sh_m, w_p, jnp.zeros((1, P_WIDTH), F32), seq,
                       sigmoid=False, name="in_proj")
        w_g = w_gate[l].transpose(1, 0, 2).reshape(d, 4 * d).astype(BF16)
        gates = _norm_proj(xf, norm_mix[l][None], sc_m, sh_m, w_g, b_gate[l].reshape(1, 4 * d), seq,
                           sigmoid=True, name="branch_gates")

        wq = jnp.pad(mla_w_uq[l].reshape(MLA_Q_RANK, MLA_HEADS, MLA_NOPE + MLA_ROPE),
                     ((0, 0), (0, 0), (0, LANES - MLA_NOPE - MLA_ROPE)))
        wq = wq.reshape(MLA_Q_RANK, MLA_HEADS * LANES).astype(BF16)
        wkv = mla_w_ukv[l].reshape(MLA_KV_RANK, MLA_HEADS, MLA_NOPE + MLA_V)
        wk = jnp.pad(wkv[:, :, :MLA_NOPE], ((0, 0), (0, 0), (0, LANES - MLA_NOPE)))
        wk = wk.reshape(MLA_KV_RANK, MLA_HEADS * LANES).astype(BF16)
        wv = wkv[:, :, MLA_NOPE:].reshape(MLA_KV_RANK, MLA_HEADS * MLA_V).astype(BF16)
        q_a, k_a, v_a = _mla_prep(p, mla_cos, mla_sin, mla_q_norm[l][None], mla_kv_norm[l][None], wq, wk, wv)
        o_mla = _attention("mla", batch, seq, q=q_a, k=k_a, v=v_a)

        neg_cum = _fox_neg_cum(p, fox_f_bias[l], batch, seq).reshape(batch * 4, 2, seq)
        o_fox = _attention("fox", batch, seq, q=p, k=p, v=p, q_col=COL_FOXQ // LANES,
                           k_col=COL_FOXK // LANES, v_col=COL_FOXV // LANES, extra=(neg_cum,))

        o_conv = _short_conv(p, conv_w[l], seq)

        q_d, k_d = _diff_rope(p, diff_cos, diff_sin)
        lam_vecs = jnp.stack([diff_lam_q1[l], diff_lam_k1[l], diff_lam_q2[l], diff_lam_k2[l]])
        o_diff = _attention("diff", batch, seq, q=q_d, k=k_d, v=p, v_col=COL_DV // LANES,
                            extra=(lam_vecs, diff_subln[l][None]), lam_init=lam_init)

        xf = _merge((o_mla, o_fox, o_conv, o_diff), gates, w_branch[l].astype(BF16),
                    w_out[l].astype(BF16), xf, g_m, seq)

        h, idx, wts = _router(xf, norm_ffn[l][None], sc_f, sh_f, w_router[l], router_bias[l][None], seq)
        tm_e = min(TM_EXP, n)
        dest, src, gw, tile_expert = _dispatch_plan(idx, wts, tm_e)
        y = _expert_ffn(l, tile_expert, src, h, w_exp_gate, w_exp_up, w_exp_down, gw)
        shared = _shared_ffn(h, w_sh_gate[l].astype(BF16), w_sh_up[l].astype(BF16),
                             w_sh_down[l].astype(BF16))
        tc = min(TC_COMB, seq)
        dest_t = dest.reshape(n // tc, tc, TOP_K).transpose(0, 2, 1).reshape(n // tc, 1, TOP_K * tc)
        xf = _combine(dest_t, y, shared, xf, g_f, final_norm[None], seq, final_norm=(l == depth - 1))

    return xf.reshape(batch, seq, d)
```

```python
import functools
import math

import numpy as np
import jax
import jax.numpy as jnp
from jax import lax
from jax.experimental import pallas as pl
from jax.experimental.pallas import tpu as pltpu

F32 = jnp.float32
BF16 = jnp.bfloat16
I32 = jnp.int32

CHUNK = 64
CHUNK_SHIFT = 6
RMS_EPS = 1e-6
ROPE_THETA = 500000.0
MLA_HEADS, MLA_Q_RANK, MLA_KV_RANK, MLA_NOPE, MLA_ROPE, MLA_V = 8, 384, 256, 64, 32, 64
FOX_HEADS, FOX_HD = 8, 64
CONV_WIDTH, CONV_TAPS = 512, 3
DIFF_HEADS, DIFF_HD, DIFF_ROT = 4, 64, 16
N_EXPERTS, TOP_K, ROUTED_SCALE = 64, 8, 2.5
BRANCH_WIDTH = 512
LOG2E = 1.4426950408889634

LANES = 128
BF16_SUBLANES = 16
V7X_VMEM_BYTES = 64 * 1024 * 1024
VMEM_LIMIT = V7X_VMEM_BYTES - 8 * 1024 * 1024
NEG_BIG = -1e30

TM_PROJ = 512
TN_PROJ = 512
TM_PREP = 512
TS_CUM = 256
TQ_ATT = 256
TK_ATT = 512
TM_MERGE = 256
TM_ROUTE = 256
TM_EXP = 256
TM_FFN = 512
TC_COMB = 64

PAIRS = 4
VT_ROWS = LANES + BF16_SUBLANES

COL_FOXQ, COL_FOXK, COL_FOXV = 0, 512, 1024
COL_CB, COL_CC, COL_CU = 1536, 2048, 2560
COL_DQ, COL_DK, COL_DV = 3072, 3584, 4096
COL_CQ = 4608
COL_KR = 4992
COL_CKV = 5120
COL_ZF = 5376
P_WIDTH = 5632


def _cparams(*sem):
    return pltpu.CompilerParams(dimension_semantics=sem, vmem_limit_bytes=VMEM_LIMIT)


def _normmod(x, g, sc, sh):
    ms = jnp.mean(x * x, axis=-1, keepdims=True)
    return (x * lax.rsqrt(ms + RMS_EPS)) * g * (1.0 + sc) + sh


def _rope_lanes(x, cos_t, sin_t, lo, half, period):
    width = x.shape[1]
    lane = lax.broadcasted_iota(I32, x.shape, 1) & (period - 1)
    x_up = pltpu.roll(x, width - half, 1)
    x_dn = pltpu.roll(x, half, 1)
    rot = jnp.where(lane < lo, 0.0,
                    jnp.where(lane < lo + half, -x_up,
                              jnp.where(lane < lo + 2 * half, x_dn, 0.0)))
    return x * cos_t + rot * sin_t


def _store_vt(vt_ref, v):
    tm = v.shape[0]
    ones = jnp.ones((VT_ROWS - LANES, tm), BF16)
    for p in range(PAIRS):
        vt_ref[0, p, 0:LANES, :] = v[:, p * LANES:(p + 1) * LANES].T.astype(BF16)
        vt_ref[0, p, LANES:VT_ROWS, :] = ones


def _vt_out(batch, seq, tm):
    per_b = seq // tm
    spec = pl.BlockSpec((1, PAIRS, VT_ROWS, tm), lambda i: (i // per_b, 0, 0, i % per_b))
    return spec, jax.ShapeDtypeStruct((batch, PAIRS, VT_ROWS, seq), BF16)


def _mod_kernel(c_ref, w_ref, b_ref, o_ref):
    c = c_ref[...]
    ca = (c * jax.nn.sigmoid(c)).astype(BF16)
    o_ref[0] = jnp.dot(ca, w_ref[0].astype(BF16), preferred_element_type=F32) + b_ref[0]


def _modulation(c, ada_w, ada_b):
    n_layers, d, d6 = ada_w.shape
    b = c.shape[0]
    rows = 8
    cp = jnp.zeros((rows, d), F32).at[:b].set(c)
    tn = 1024
    return pl.pallas_call(
        _mod_kernel,
        grid=(n_layers, d6 // tn),
        in_specs=[pl.BlockSpec((rows, d), lambda l, j: (0, 0)),
                  pl.BlockSpec((1, d, tn), lambda l, j: (l, 0, j)),
                  pl.BlockSpec((1, 1, tn), lambda l, j: (l, 0, j))],
        out_specs=pl.BlockSpec((1, rows, tn), lambda l, j: (l, 0, j)),
        out_shape=jax.ShapeDtypeStruct((n_layers, rows, d6), F32),
        compiler_params=_cparams("parallel", "parallel"),
        name="adaln_modulation",
    )(cp, ada_w, ada_b.reshape(n_layers, 1, d6))


def _proj_kernel(x_ref, g_ref, sc_ref, sh_ref, w_ref, e_ref, o_ref, h_scr, *, sigmoid):
    @pl.when(pl.program_id(1) == 0)
    def _():
        h_scr[...] = _normmod(x_ref[...], g_ref[...], sc_ref[0], sh_ref[0]).astype(BF16)

    acc = jnp.dot(h_scr[...], w_ref[...], preferred_element_type=F32)
    if sigmoid:
        acc = jax.nn.sigmoid(acc + e_ref[...])
    else:
        acc = acc * e_ref[...]
    o_ref[...] = acc.astype(o_ref.dtype)


def _norm_proj(x, g, sc, sh, w, epi, seq, *, sigmoid, name):
    n, d = x.shape
    cols = w.shape[1]
    tm, tn = min(TM_PROJ, seq), TN_PROJ
    per_b = seq // tm
    return pl.pallas_call(
        functools.partial(_proj_kernel, sigmoid=sigmoid),
        grid=(n // tm, cols // tn),
        in_specs=[pl.BlockSpec((tm, d), lambda i, j: (i, 0)),
                  pl.BlockSpec((1, d), lambda i, j: (0, 0)),
                  pl.BlockSpec((1, 1, d), lambda i, j: (i // per_b, 0, 0)),
                  pl.BlockSpec((1, 1, d), lambda i, j: (i // per_b, 0, 0)),
                  pl.BlockSpec((d, tn), lambda i, j: (0, j)),
                  pl.BlockSpec((1, tn), lambda i, j: (0, j))],
        out_specs=pl.BlockSpec((tm, tn), lambda i, j: (i, j)),
        out_shape=jax.ShapeDtypeStruct((n, cols), BF16),
        scratch_shapes=[pltpu.VMEM((tm, d), BF16)],
        compiler_params=_cparams("parallel", "arbitrary"),
        name=name,
    )(x, g, sc, sh, w, epi)


def _mla_prep_kernel(cq_ref, ckv_ref, kr_ref, cos_ref, sin_ref, qn_ref, kvn_ref,
                     wq_ref, wk_ref, wv_ref, q_ref, k_ref, vt_ref, *, scale):
    cos_t, sin_t = cos_ref[...], sin_ref[...]

    def rms(t, gain):
        t = t.astype(F32)
        return (t * lax.rsqrt(jnp.mean(t * t, axis=-1, keepdims=True) + RMS_EPS) * gain).astype(BF16)

    def rope(t):
        return _rope_lanes(t, cos_t, sin_t, MLA_NOPE, MLA_ROPE // 2, LANES)

    qn = rms(cq_ref[...], qn_ref[...])
    kvn = rms(ckv_ref[...], kvn_ref[...])
    q = jnp.dot(qn, wq_ref[...], preferred_element_type=F32)
    k = jnp.dot(kvn, wk_ref[...], preferred_element_type=F32)
    _store_vt(vt_ref, jnp.dot(kvn, wv_ref[...], preferred_element_type=F32))
    k_pe = rope(kr_ref[...].astype(F32))
    for h in range(MLA_HEADS):
        sl = slice(h * LANES, (h + 1) * LANES)
        q_ref[:, sl] = (rope(q[:, sl]) * scale).astype(BF16)
        k_ref[:, sl] = (k[:, sl] + k_pe).astype(BF16)


def _mla_prep(p, cos_t, sin_t, q_norm, kv_norm, wq, wk, wv, batch, seq):
    n = p.shape[0]
    tm = min(TM_PREP, seq)
    hq = MLA_HEADS * LANES
    hv = MLA_HEADS * MLA_V
    scale = float(MLA_NOPE + MLA_ROPE) ** -0.5 * LOG2E
    const = lambda i: (0, 0)
    vt_spec, vt_shape = _vt_out(batch, seq, tm)
    return pl.pallas_call(
        functools.partial(_mla_prep_kernel, scale=scale),
        grid=(n // tm,),
        in_specs=[pl.BlockSpec((tm, MLA_Q_RANK), lambda i: (i, COL_CQ // MLA_Q_RANK)),
                  pl.BlockSpec((tm, MLA_KV_RANK), lambda i: (i, COL_CKV // MLA_KV_RANK)),
                  pl.BlockSpec((tm, LANES), lambda i: (i, COL_KR // LANES)),
                  pl.BlockSpec((tm, LANES), lambda i: (i, 0)),
                  pl.BlockSpec((tm, LANES), lambda i: (i, 0)),
                  pl.BlockSpec((1, MLA_Q_RANK), const),
                  pl.BlockSpec((1, MLA_KV_RANK), const),
                  pl.BlockSpec((MLA_Q_RANK, hq), const),
                  pl.BlockSpec((MLA_KV_RANK, hq), const),
                  pl.BlockSpec((MLA_KV_RANK, hv), const)],
        out_specs=[pl.BlockSpec((tm, hq), lambda i: (i, 0)),
                   pl.BlockSpec((tm, hq), lambda i: (i, 0)),
                   vt_spec],
        out_shape=[jax.ShapeDtypeStruct((n, hq), BF16),
                   jax.ShapeDtypeStruct((n, hq), BF16),
                   vt_shape],
        compiler_params=_cparams("parallel"),
        name="mla_prep",
    )(p, p, p, cos_t, sin_t, q_norm, kv_norm, wq, wk, wv)


def _diff_prep_kernel(q_ref, k_ref, v_ref, cos_ref, sin_ref, qo_ref, ko_ref, vt_ref):
    cos_t, sin_t = cos_ref[...], sin_ref[...]
    for src, dst in ((q_ref, qo_ref), (k_ref, ko_ref)):
        for h in range(DIFF_HEADS):
            sl = slice(h * LANES, (h + 1) * LANES)
            t = src[:, sl].astype(F32)
            dst[:, sl] = _rope_lanes(t, cos_t, sin_t, 0, DIFF_ROT // 2, DIFF_HD).astype(BF16)
    _store_vt(vt_ref, v_ref[...].astype(F32))


def _diff_prep(p, cos_t, sin_t, batch, seq):
    n = p.shape[0]
    tm = min(TM_PREP, seq)
    w = DIFF_HEADS * 2 * DIFF_HD
    vt_spec, vt_shape = _vt_out(batch, seq, tm)
    return pl.pallas_call(
        _diff_prep_kernel,
        grid=(n // tm,),
        in_specs=[pl.BlockSpec((tm, w), lambda i: (i, COL_DQ // w)),
                  pl.BlockSpec((tm, w), lambda i: (i, COL_DK // w)),
                  pl.BlockSpec((tm, w), lambda i: (i, COL_DV // w)),
                  pl.BlockSpec((tm, LANES), lambda i: (i, 0)),
                  pl.BlockSpec((tm, LANES), lambda i: (i, 0))],
        out_specs=[pl.BlockSpec((tm, w), lambda i: (i, 0)),
                   pl.BlockSpec((tm, w), lambda i: (i, 0)),
                   vt_spec],
        out_shape=[jax.ShapeDtypeStruct((n, w), BF16)] * 2 + [vt_shape],
        compiler_params=_cparams("parallel"),
        name="diff_prep",
    )(p, p, p, cos_t, sin_t)


def _fox_prep_kernel(z_ref, k_ref, v_ref, b_ref, tri_ref, sel_ref, ko_ref, vt_ref, carry):
    @pl.when(pl.program_id(1) == 0)
    def _():
        carry[...] = jnp.zeros_like(carry)

    z = z_ref[...].astype(F32) + b_ref[...]
    log_f = jnp.minimum(z, 0.0) - jnp.log(1.0 + jnp.exp(-jnp.abs(z)))
    cum = jnp.dot(tri_ref[...], log_f, precision=lax.Precision.HIGHEST,
                  preferred_element_type=F32) + carry[...]
    ts = cum.shape[0]
    carry[...] = cum[ts - 1:ts, :]
    neg = cum * (-LOG2E)
    hi = neg.astype(BF16)
    r1 = neg - hi.astype(F32)
    mid = r1.astype(BF16)
    lo = (r1 - mid.astype(F32)).astype(BF16)
    parts = jnp.concatenate([hi, mid, lo], axis=1)
    for p in range(PAIRS):
        ko_ref[:, 2 * p * LANES:(2 * p + 1) * LANES] = k_ref[:, p * LANES:(p + 1) * LANES]
        ext = jnp.dot(parts, sel_ref[p], preferred_element_type=F32)
        ko_ref[:, (2 * p + 1) * LANES:(2 * p + 2) * LANES] = ext.astype(BF16)
    _store_vt(vt_ref, v_ref[...].astype(F32))


def _fox_decay_select():
    sel = np.zeros((PAIRS, 3 * LANES, LANES), np.float32)
    for p in range(PAIRS):
        for t in range(2):
            for j in range(3):
                sel[p, j * LANES + 2 * p + t, 3 * t + j] = 1.0
    return jnp.asarray(sel, BF16)


def _fox_prep(p, f_bias, batch, seq):
    n = p.shape[0]
    ts = min(TS_CUM, seq)
    per_b = seq // ts
    w = FOX_HEADS * FOX_HD
    bias = jnp.zeros((1, LANES), F32).at[0, :FOX_HEADS].set(f_bias)
    tri = jnp.tril(jnp.ones((ts, ts), F32))
    row = lambda b, j: b * per_b + j
    return pl.pallas_call(
        _fox_prep_kernel,
        grid=(batch, per_b),
        in_specs=[pl.BlockSpec((ts, LANES), lambda b, j: (row(b, j), COL_ZF // LANES)),
                  pl.BlockSpec((ts, w), lambda b, j: (row(b, j), COL_FOXK // w)),
                  pl.BlockSpec((ts, w), lambda b, j: (row(b, j), COL_FOXV // w)),
                  pl.BlockSpec((1, LANES), lambda b, j: (0, 0)),
                  pl.BlockSpec((ts, ts), lambda b, j: (0, 0)),
                  pl.BlockSpec((PAIRS, 3 * LANES, LANES), lambda b, j: (0, 0, 0))],
        out_specs=[pl.BlockSpec((ts, 2 * w), lambda b, j: (row(b, j), 0)),
                   pl.BlockSpec((1, PAIRS, VT_ROWS, ts), lambda b, j: (b, 0, 0, j))],
        out_shape=[jax.ShapeDtypeStruct((n, 2 * w), BF16),
                   jax.ShapeDtypeStruct((batch, PAIRS, VT_ROWS, seq), BF16)],
        scratch_shapes=[pltpu.VMEM((1, LANES), F32)],
        compiler_params=_cparams("parallel", "arbitrary"),
        name="fox_prep",
    )(p, p, p, bias, tri, _fox_decay_select())


def _attn_kernel(*refs, mode, tq, tk, per_frame, lam_init):
    if mode == "diff":
        q_ref, k_ref, vt_ref, lam_ref, subln_ref, o_ref, q_scr, s_scr, m_scr, acc_scr = refs
    else:
        q_ref, k_ref, vt_ref, o_ref, q_scr, s_scr, m_scr, acc_scr = refs
    w = 2 * tq
    i = pl.program_id(2)
    q = q_ref[...]
    zero = jnp.zeros((tq, LANES), BF16)
    lane = lax.broadcasted_iota(I32, (tq, LANES), 1)
    if mode == "mla":
        q_scr[0:tq, :] = jnp.concatenate([q[:, :LANES], zero], axis=1)
        q_scr[tq:w, :] = jnp.concatenate([zero, q[:, LANES:]], axis=1)
    elif mode == "fox":
        ind_a = jnp.where(lane < 3, 1.0, 0.0).astype(BF16)
        ind_b = jnp.where(lane < 3, 0.0, jnp.where(lane < 6, 1.0, 0.0)).astype(BF16)
        q_scr[0:tq, :] = jnp.concatenate([jnp.where(lane < FOX_HD, q, zero), ind_a], axis=1)
        q_scr[tq:w, :] = jnp.concatenate([jnp.where(lane < FOX_HD, zero, q), ind_b], axis=1)
    else:
        q_scr[0:tq, :] = jnp.where(lane < DIFF_HD, q, zero)
        q_scr[tq:w, :] = jnp.where(lane < DIFF_HD, zero, q)
    m_scr[...] = jnp.full(m_scr.shape, NEG_BIG, F32)
    acc_scr[...] = jnp.zeros(acc_scr.shape, F32)

    def logits(j, slot):
        ks = pl.multiple_of(j * tk, tk)
        s_scr[slot] = lax.dot_general(k_ref[pl.ds(ks, tk), :], q_scr[...], (((1,), (1,)), ((), ())),
                                      preferred_element_type=F32)

    def consume(j, slot, masked):
        ks = pl.multiple_of(j * tk, tk)
        s = s_scr[slot]
        if masked:
            k_pos = ks + lax.broadcasted_iota(I32, (tk, w), 0)
            q_pos = i * tq + (lax.broadcasted_iota(I32, (tk, w), 1) & (tq - 1))
            if per_frame:
                visible = k_pos <= q_pos
            else:
                visible = (k_pos >> CHUNK_SHIFT) <= (q_pos >> CHUNK_SHIFT)
            s = jnp.where(visible, s, NEG_BIG)
        m_prev = m_scr[...]
        m_new = jnp.maximum(m_prev, jnp.max(s, axis=0, keepdims=True))
        alpha = jnp.exp2(m_prev - m_new)
        p = jnp.exp2(s - m_new).astype(BF16)
        acc_scr[...] = alpha * acc_scr[...] + jnp.dot(vt_ref[:, pl.ds(ks, tk)], p,
                                                      preferred_element_type=F32)
        m_scr[...] = m_new

    n_un = (i * tq) // tk
    logits(0, 0)

    def pair_body(jj, carry):
        j = 2 * jj
        logits(j + 1, 1)
        consume(j, 0, False)
        logits(j + 2, 0)
        consume(j + 1, 1, False)
        return carry

    lax.fori_loop(0, n_un // 2, pair_body, 0)

    @pl.when(n_un % 2 == 1)
    def _():
        logits(n_un, 1)
        consume(n_un - 1, 0, False)
        consume(n_un, 1, True)

    @pl.when(n_un % 2 == 0)
    def _():
        consume(n_un, 0, True)

    oa = acc_scr[0:LANES, 0:tq] / acc_scr[LANES:LANES + 1, 0:tq]
    ob = acc_scr[0:LANES, tq:w] / acc_scr[LANES:LANES + 1, tq:w]
    if mode == "diff":
        lv = lam_ref[...]
        lam = (jnp.exp(jnp.sum(lv[0:1] * lv[1:2], axis=1, keepdims=True))
               - jnp.exp(jnp.sum(lv[2:3] * lv[3:4], axis=1, keepdims=True)) + lam_init)
        o = oa - lam * ob
        o = o * lax.rsqrt(jnp.mean(o * o, axis=0, keepdims=True) + RMS_EPS)
        o = o.T * subln_ref[...] * (1.0 - lam_init)
    else:
        row = lax.broadcasted_iota(I32, (LANES, tq), 0)
        o = jnp.where(row < FOX_HD, oa, ob).T
    o_ref[...] = o.astype(BF16)


def _attention(mode, batch, seq, *, q, k, vt, q_col=0, extra=(), lam_init=0.0):
    n = batch * seq
    tq = min(TQ_ATT, seq // 2)
    tk = 2 * tq
    nq = seq // tq
    kw = LANES if mode == "diff" else 2 * LANES
    qw = 2 * LANES if mode == "mla" else LANES
    qb = q_col // qw
    in_specs = [pl.BlockSpec((tq, qw), lambda b, p, i: (b * nq + i, qb + p)),
                pl.BlockSpec((seq, kw), lambda b, p, i: (b, p)),
                pl.BlockSpec((1, 1, VT_ROWS, seq), lambda b, p, i: (b, p, 0, 0))]
    args = (q, k, vt)
    if mode == "diff":
        in_specs += [pl.BlockSpec((4, DIFF_HD), lambda b, p, i: (0, 0)),
                     pl.BlockSpec((1, LANES), lambda b, p, i: (0, 0))]
        args += tuple(extra)

    def body(*refs):
        refs = list(refs)
        refs[2] = refs[2].at[0, 0]
        _attn_kernel(*refs, mode=mode, tq=tq, tk=tk, per_frame=(mode == "fox"), lam_init=lam_init)

    return pl.pallas_call(
        body,
        grid=(batch, PAIRS, nq),
        in_specs=in_specs,
        out_specs=pl.BlockSpec((tq, LANES), lambda b, p, i: (b * nq + i, p)),
        out_shape=jax.ShapeDtypeStruct((n, PAIRS * LANES), BF16),
        scratch_shapes=[pltpu.VMEM((2 * tq, kw), BF16),
                        pltpu.VMEM((2, tk, 2 * tq), F32),
                        pltpu.VMEM((1, 2 * tq), F32),
                        pltpu.VMEM((VT_ROWS, 2 * tq), F32)],
        compiler_params=_cparams("parallel", "parallel", "arbitrary"),
        name="attn_" + mode,
    )(*args)


def _conv_kernel(b_ref, c_ref, u_ref, ch_ref, uh_ref, w_ref, o_ref, *, per_b):
    u = c_ref[...].astype(F32) * u_ref[...].astype(F32)
    halo = ch_ref[...].astype(F32) * uh_ref[...].astype(F32)
    halo = jnp.where(pl.program_id(0) % per_b == 0, 0.0, halo)
    hr = halo.shape[0]
    row = lax.broadcasted_iota(I32, u.shape, 0)
    u1 = jnp.where(row == 0, halo[hr - 1:hr, :], pltpu.roll(u, 1, 0))
    u2 = jnp.where(row == 0, halo[hr - 2:hr - 1, :],
                   jnp.where(row == 1, halo[hr - 1:hr, :], pltpu.roll(u, 2, 0)))
    w = w_ref[...]
    y = u2 * w[0:1, :] + u1 * w[1:2, :] + u * w[2:3, :]
    o_ref[...] = (b_ref[...].astype(F32) * y).astype(BF16)


def _short_conv(p, conv_w, seq):
    n = p.shape[0]
    ts = min(TM_PREP, seq)
    per_b = seq // ts
    w = CONV_WIDTH
    hr = BF16_SUBLANES
    halo_map = lambda col: (lambda i: (jnp.maximum(i * (ts // hr) - 1, 0), col // w))
    return pl.pallas_call(
        functools.partial(_conv_kernel, per_b=per_b),
        grid=(n // ts,),
        in_specs=[pl.BlockSpec((ts, w), lambda i: (i, COL_CB // w)),
                  pl.BlockSpec((ts, w), lambda i: (i, COL_CC // w)),
                  pl.BlockSpec((ts, w), lambda i: (i, COL_CU // w)),
                  pl.BlockSpec((hr, w), halo_map(COL_CC)),
                  pl.BlockSpec((hr, w), halo_map(COL_CU)),
                  pl.BlockSpec((CONV_TAPS, w), lambda i: (0, 0))],
        out_specs=pl.BlockSpec((ts, w), lambda i: (i, 0)),
        out_shape=jax.ShapeDtypeStruct((n, w), BF16),
        compiler_params=_cparams("parallel"),
        name="short_conv",
    )(p, p, p, p, p, conv_w)


def _merge_kernel(o0, o1, o2, o3, g0, g1, g2, g3, wb_ref, wo_ref, x_ref, gm_ref, out_ref):
    merged = None
    for br, (o_ref, g_ref) in enumerate(((o0, g0), (o1, g1), (o2, g2), (o3, g3))):
        t = jnp.dot(o_ref[...], wb_ref[br], preferred_element_type=F32) * g_ref[...].astype(F32)
        merged = t if merged is None else merged + t
    y = jnp.dot(merged.astype(BF16), wo_ref[...], preferred_element_type=F32)
    out_ref[...] = x_ref[...] + gm_ref[0] * y


def _merge(branches, gates, w_branch, w_out, x, g_m, seq):
    n, d = x.shape
    tm = min(TM_MERGE, seq)
    per_b = seq // tm
    bw = BRANCH_WIDTH
    o_spec = pl.BlockSpec((tm, bw), lambda i: (i, 0))
    g_specs = [pl.BlockSpec((tm, d), lambda i, br=br: (i, br)) for br in range(4)]
    single = pl.Buffered(1)
    return pl.pallas_call(
        _merge_kernel,
        grid=(n // tm,),
        in_specs=[o_spec] * 4 + g_specs + [
            pl.BlockSpec((4, bw, d), lambda i: (0, 0, 0), pipeline_mode=single),
            pl.BlockSpec((d, d), lambda i: (0, 0), pipeline_mode=single),
            pl.BlockSpec((tm, d), lambda i: (i, 0)),
            pl.BlockSpec((1, 1, d), lambda i: (i // per_b, 0, 0))],
        out_specs=pl.BlockSpec((tm, d), lambda i: (i, 0)),
        out_shape=jax.ShapeDtypeStruct((n, d), F32),
        compiler_params=_cparams("parallel"),
        name="branch_merge",
    )(*branches, gates, gates, gates, gates, w_branch, w_out, x, g_m)


def _router_kernel(x_ref, g_ref, sc_ref, sh_ref, wr_ref, rb_ref, tri_ref,
                   h_ref, idx_ref, wts_ref, rank_ref, cnt_ref, carry):
    @pl.when(pl.program_id(0) == 0)
    def _():
        carry[...] = jnp.zeros_like(carry)

    h = _normmod(x_ref[...], g_ref[...], sc_ref[0], sh_ref[0])
    h_ref[...] = h
    logits = jnp.dot(h, wr_ref[...], precision=lax.Precision.HIGHEST, preferred_element_type=F32)
    scores = jax.nn.sigmoid(logits)
    work = scores + rb_ref[...]
    tm, ne = scores.shape
    lane = lax.broadcasted_iota(I32, (tm, ne), 1).astype(F32)
    slot = lax.broadcasted_iota(I32, (tm, TOP_K), 1)
    idx = jnp.zeros((tm, TOP_K), F32)
    sel = jnp.zeros((tm, TOP_K), F32)
    hits = []
    chosen = jnp.zeros((tm, ne), F32)
    for kk in range(TOP_K):
        mx = jnp.max(work, axis=1, keepdims=True)
        first = jnp.min(jnp.where(work == mx, lane, float(ne)), axis=1, keepdims=True)
        hit = lane == first
        hits.append(hit)
        chosen = jnp.where(hit, 1.0, chosen)
        sc_k = jnp.sum(jnp.where(hit, scores, 0.0), axis=1, keepdims=True)
        work = jnp.where(hit, -jnp.inf, work)
        idx = jnp.where(slot == kk, first, idx)
        sel = jnp.where(slot == kk, sc_k, sel)
    idx_ref[...] = idx.astype(I32)
    wts_ref[...] = sel / jnp.sum(sel, axis=1, keepdims=True) * ROUTED_SCALE
    before = jnp.dot(tri_ref[...], chosen.astype(BF16), preferred_element_type=F32) + carry[...]
    rank = jnp.zeros((tm, TOP_K), F32)
    for kk in range(TOP_K):
        r_k = jnp.sum(jnp.where(hits[kk], before, 0.0), axis=1, keepdims=True)
        rank = jnp.where(slot == kk, r_k, rank)
    rank_ref[...] = rank.astype(I32)
    total = carry[...] + jnp.sum(chosen, axis=0, keepdims=True)
    carry[...] = total
    cnt_ref[...] = total.astype(I32)


def _router(x, g, sc, sh, w_router, router_bias, seq):
    n, d = x.shape
    tm = min(TM_ROUTE, seq)
    per_b = seq // tm
    ne = w_router.shape[1]
    tri = jnp.tril(jnp.ones((tm, tm), F32), k=-1).astype(BF16)
    return pl.pallas_call(
        _router_kernel,
        grid=(n // tm,),
        in_specs=[pl.BlockSpec((tm, d), lambda i: (i, 0)),
                  pl.BlockSpec((1, d), lambda i: (0, 0)),
                  pl.BlockSpec((1, 1, d), lambda i: (i // per_b, 0, 0)),
                  pl.BlockSpec((1, 1, d), lambda i: (i // per_b, 0, 0)),
                  pl.BlockSpec((d, ne), lambda i: (0, 0)),
                  pl.BlockSpec((1, ne), lambda i: (0, 0)),
                  pl.BlockSpec((tm, tm), lambda i: (0, 0))],
        out_specs=[pl.BlockSpec((tm, d), lambda i: (i, 0)),
                   pl.BlockSpec((tm, TOP_K), lambda i: (i, 0)),
                   pl.BlockSpec((tm, TOP_K), lambda i: (i, 0)),
                   pl.BlockSpec((tm, TOP_K), lambda i: (i, 0)),
                   pl.BlockSpec((1, ne), lambda i: (0, 0))],
        out_shape=[jax.ShapeDtypeStruct((n, d), F32),
                   jax.ShapeDtypeStruct((n, TOP_K), I32),
                   jax.ShapeDtypeStruct((n, TOP_K), F32),
                   jax.ShapeDtypeStruct((n, TOP_K), I32),
                   jax.ShapeDtypeStruct((1, ne), I32)],
        scratch_shapes=[pltpu.VMEM((1, ne), F32)],
        compiler_params=_cparams("arbitrary"),
        name="moe_router",
    )(x, g, sc, sh, w_router, router_bias, tri)


def _gather_step(i, n_steps, idx_hbm, idx_smem, idx_sem, data_hbm, buf, data_sem, n_rows):
    def idx_copy(step, slot):
        return pltpu.make_async_copy(idx_hbm.at[step], idx_smem.at[pl.ds(slot, 1)], idx_sem.at[slot])

    def rows_wait(slot):
        pltpu.make_async_copy(data_hbm.at[pl.ds(0, n_rows)], buf.at[slot], data_sem.at[slot]).wait()

    def issue_rows(slot):
        def body(r, carry):
            row = idx_smem[slot, r]
            pltpu.make_async_copy(data_hbm.at[pl.ds(row, 1)], buf.at[slot, pl.ds(r, 1)],
                                  data_sem.at[slot]).start()
            return carry
        lax.fori_loop(0, n_rows, body, 0, unroll=True)

    slot = i % 2
    nxt = 1 - slot
    last = n_steps - 1

    @pl.when(i == 0)
    def _():
        idx_copy(0, 0).start()
        idx_copy(0, 0).wait()
        issue_rows(0)
        idx_copy(jnp.minimum(1, last), 1).start()

    rows_wait(slot)
    idx_copy(jnp.minimum(i + 1, last), nxt).wait()

    def prefetch():
        issue_rows(nxt)
        idx_copy(jnp.minimum(i + 2, last), slot).start()

    def drain():
        @pl.when(i == last)
        def _():
            rows_wait(nxt)
            idx_copy(last, slot).wait()

    return slot, prefetch, drain


def _expert_kernel(te_ref, src_hbm, h_hbm, wg_ref, wu_ref, wd_ref, y_ref,
                   idx_smem, idx_sem, xbuf, x_sem, wg_s, wu_s, wd_s, xs_s, *, n_steps, tm):
    i = pl.program_id(0)
    slot, prefetch, drain = _gather_step(i, n_steps, src_hbm, idx_smem, idx_sem, h_hbm, xbuf, x_sem, tm)

    new_expert = jnp.logical_or(i == 0, te_ref[i] != te_ref[jnp.maximum(i - 1, 0)])

    @pl.when(new_expert)
    def _():
        wg_s[...] = wg_ref[0, 0].astype(BF16)
        wu_s[...] = wu_ref[0, 0].astype(BF16)
        wd_s[...] = wd_ref[0, 0].astype(BF16)

    xs_s[...] = xbuf[slot].astype(BF16)
    prefetch()
    x = xs_s[...]
    a = jnp.dot(x, wg_s[...], preferred_element_type=F32)
    u = jnp.dot(x, wu_s[...], preferred_element_type=F32)
    hid = (a * jax.nn.sigmoid(a)) * u
    y_ref[...] = jnp.dot(hid.astype(BF16), wd_s[...], preferred_element_type=F32)
    drain()


def _expert_ffn(layer, tile_expert, src, h, w_eg, w_eu, w_ed):
    n_steps, _, tm = src.shape
    d = h.shape[1]
    ff = w_eg.shape[3]
    grid_spec = pltpu.PrefetchScalarGridSpec(
        num_scalar_prefetch=1,
        grid=(n_steps,),
        in_specs=[pl.BlockSpec(memory_space=pl.ANY),
                  pl.BlockSpec(memory_space=pl.ANY),
                  pl.BlockSpec((1, 1, d, ff), lambda i, te: (layer, te[i], 0, 0)),
                  pl.BlockSpec((1, 1, d, ff), lambda i, te: (layer, te[i], 0, 0)),
                  pl.BlockSpec((1, 1, ff, d), lambda i, te: (layer, te[i], 0, 0))],
        out_specs=pl.BlockSpec((tm, d), lambda i, te: (i, 0)),
        scratch_shapes=[pltpu.SMEM((2, tm), I32),
                        pltpu.SemaphoreType.DMA((2,)),
                        pltpu.VMEM((2, tm, d), F32),
                        pltpu.SemaphoreType.DMA((2,)),
                        pltpu.VMEM((d, ff), BF16),
                        pltpu.VMEM((d, ff), BF16),
                        pltpu.VMEM((ff, d), BF16),
                        pltpu.VMEM((tm, d), BF16)],
    )
    return pl.pallas_call(
        functools.partial(_expert_kernel, n_steps=n_steps, tm=tm),
        grid_spec=grid_spec,
        out_shape=jax.ShapeDtypeStruct((n_steps * tm, d), F32),
        compiler_params=_cparams("arbitrary"),
        name="moe_experts",
    )(tile_expert, src, h, w_eg, w_eu, w_ed)


def _shared_kernel(h_ref, wg_ref, wu_ref, wd_ref, o_ref):
    h = h_ref[...].astype(BF16)
    a = jnp.dot(h, wg_ref[...], preferred_element_type=F32)
    u = jnp.dot(h, wu_ref[...], preferred_element_type=F32)
    hid = ((a * jax.nn.sigmoid(a)) * u).astype(BF16)
    o_ref[...] = jnp.dot(hid, wd_ref[...], preferred_element_type=F32)


def _shared_ffn(h, wg, wu, wd):
    n, d = h.shape
    ff = wg.shape[1]
    tm = min(TM_FFN, n)
    return pl.pallas_call(
        _shared_kernel,
        grid=(n // tm,),
        in_specs=[pl.BlockSpec((tm, d), lambda i: (i, 0)),
                  pl.BlockSpec((d, ff), lambda i: (0, 0)),
                  pl.BlockSpec((d, ff), lambda i: (0, 0)),
                  pl.BlockSpec((ff, d), lambda i: (0, 0))],
        out_specs=pl.BlockSpec((tm, d), lambda i: (i, 0)),
        out_shape=jax.ShapeDtypeStruct((n, d), F32),
        compiler_params=_cparams("parallel"),
        name="moe_shared",
    )(h, wg, wu, wd)


def _combine_kernel(dest_hbm, y_hbm, wts_ref, sh_ref, x_ref, gf_ref, fn_ref, o_ref,
                    idx_smem, idx_sem, ybuf, y_sem, *, n_steps, tc, final_norm):
    i = pl.program_id(0)
    slot, prefetch, drain = _gather_step(i, n_steps, dest_hbm, idx_smem, idx_sem, y_hbm, ybuf, y_sem,
                                         TOP_K * tc)
    moe = sh_ref[...]
    wts = wts_ref[...]
    for kk in range(TOP_K):
        moe = moe + wts[:, kk:kk + 1] * ybuf[slot, kk * tc:(kk + 1) * tc, :]
    prefetch()
    out = x_ref[...] + gf_ref[0] * moe
    if final_norm:
        out = out * lax.rsqrt(jnp.mean(out * out, axis=-1, keepdims=True) + RMS_EPS) * fn_ref[...]
    o_ref[...] = out
    drain()


def _combine(dest, y, wts, shared, x, g_f, final_gain, seq, *, final_norm):
    n, d = x.shape
    n_steps, _, rows = dest.shape
    tc = rows // TOP_K
    per_b = seq // tc
    return pl.pallas_call(
        functools.partial(_combine_kernel, n_steps=n_steps, tc=tc, final_norm=final_norm),
        grid=(n_steps,),
        in_specs=[pl.BlockSpec(memory_space=pl.ANY),
                  pl.BlockSpec(memory_space=pl.ANY),
                  pl.BlockSpec((tc, TOP_K), lambda i: (i, 0)),
                  pl.BlockSpec((tc, d), lambda i: (i, 0)),
                  pl.BlockSpec((tc, d), lambda i: (i, 0)),
                  pl.BlockSpec((1, 1, d), lambda i: (i // per_b, 0, 0)),
                  pl.BlockSpec((1, d), lambda i: (0, 0))],
        out_specs=pl.BlockSpec((tc, d), lambda i: (i, 0)),
        out_shape=jax.ShapeDtypeStruct((n, d), F32),
        scratch_shapes=[pltpu.SMEM((2, rows), I32),
                        pltpu.SemaphoreType.DMA((2,)),
                        pltpu.VMEM((2, rows, d), F32),
                        pltpu.SemaphoreType.DMA((2,))],
        compiler_params=_cparams("arbitrary"),
        name="moe_combine",
    )(dest, y, wts, shared, x, g_f, final_gain)


def _rope_tables(positions):
    pos = positions.reshape(-1).astype(F32)
    n = pos.shape[0]

    def cos_sin(half):
        inv_freq = ROPE_THETA ** (-jnp.arange(half, dtype=F32) / half)
        ang = pos[:, None] * inv_freq
        return jnp.cos(ang), jnp.sin(ang)

    cm, sm = cos_sin(MLA_ROPE // 2)
    tail = LANES - MLA_NOPE - MLA_ROPE
    mla_cos = jnp.concatenate([jnp.ones((n, MLA_NOPE), F32), cm, cm, jnp.ones((n, tail), F32)], axis=1)
    mla_sin = jnp.concatenate([jnp.zeros((n, MLA_NOPE), F32), sm, sm, jnp.zeros((n, tail), F32)], axis=1)
    cd, sd = cos_sin(DIFF_ROT // 2)
    rest = DIFF_HD - DIFF_ROT
    dc = jnp.concatenate([cd, cd, jnp.ones((n, rest), F32)], axis=1)
    ds = jnp.concatenate([sd, sd, jnp.zeros((n, rest), F32)], axis=1)
    return mla_cos, mla_sin, jnp.concatenate([dc, dc], axis=1), jnp.concatenate([ds, ds], axis=1)


def _in_proj_weight(w_in):
    d = w_in.shape[0]
    sizes = (MLA_Q_RANK, MLA_KV_RANK, MLA_ROPE, 512, 512, 512, FOX_HEADS, 512, 512, 512, 512, 512, 512)
    parts, off = [], 0
    for s in sizes:
        parts.append(w_in[:, off:off + s])
        off += s
    a_cq, a_ckv, a_kr, b_q, b_k, b_v, b_zf, c_b, c_c, c_u, d_q, d_k, d_v = parts
    zeros = lambda w: jnp.zeros((d, w), w_in.dtype)
    cols = [b_q, b_k, b_v, c_b, c_c, c_u, d_q, d_k, d_v, a_cq,
            zeros(MLA_NOPE), a_kr, zeros(LANES - MLA_NOPE - MLA_ROPE),
            a_ckv, b_zf, zeros(LANES - FOX_HEADS), zeros(P_WIDTH - COL_ZF - LANES)]
    return jnp.concatenate(cols, axis=1).astype(BF16)


def _in_proj_scale():
    s = np.ones((1, P_WIDTH), np.float32)
    qs = FOX_HD ** -0.5 * LOG2E
    s[0, COL_FOXQ:COL_FOXQ + FOX_HEADS * FOX_HD] = qs
    s[0, COL_DQ:COL_DQ + DIFF_HEADS * 2 * DIFF_HD] = qs
    return jnp.asarray(s)


def _dispatch_plan(idx, rank, counts, tm):
    n = idx.shape[0]
    counts = counts.reshape(-1)
    padded = (counts + tm - 1) // tm * tm
    ends = jnp.cumsum(padded)
    starts = ends - padded
    dest = (starts[idx] + rank).astype(I32)
    n_steps = (n * TOP_K) // tm + N_EXPERTS
    tok = jnp.repeat(jnp.arange(n, dtype=I32), TOP_K)
    src = jnp.zeros((n_steps * tm,), I32).at[dest.reshape(-1)].set(tok)
    tile_start = jnp.arange(n_steps, dtype=I32) * tm
    tile_expert = jnp.sum((ends[None, :] <= tile_start[:, None]).astype(I32), axis=1)
    tile_expert = jnp.minimum(tile_expert, N_EXPERTS - 1)
    return dest, src.reshape(n_steps, 1, tm), tile_expert


def kernel(x, c, positions, ada_w, ada_b, norm_mix, norm_ffn, w_in, mla_q_norm, mla_w_uq, mla_kv_norm,
           mla_w_ukv, fox_f_bias, conv_w, diff_lam_q1, diff_lam_k1, diff_lam_q2, diff_lam_k2, diff_subln,
           w_branch, w_gate, b_gate, w_out, w_router, router_bias, w_exp_gate, w_exp_up, w_exp_down,
           w_sh_gate, w_sh_up, w_sh_down, final_norm):
    batch, seq, d = x.shape
    n = batch * seq
    depth = ada_w.shape[0]
    xf = x.reshape(n, d)

    mod = _modulation(c, ada_w, ada_b)
    mla_cos, mla_sin, diff_cos, diff_sin = _rope_tables(positions)
    p_scale = _in_proj_scale()

    for l in range(depth):
        lam_init = 0.8 - 0.6 * math.exp(-0.3 * l)
        mods = mod[l, :batch].reshape(batch, 6, 1, d)
        sh_m, sc_m, g_m, sh_f, sc_f, g_f = (mods[:, t] for t in range(6))

        p = _norm_proj(xf, norm_mix[l][None], sc_m, sh_m, _in_proj_weight(w_in[l]), p_scale, seq,
                       sigmoid=False, name="in_proj")
        w_g = w_gate[l].transpose(1, 0, 2).reshape(d, 4 * d).astype(BF16)
        gates = _norm_proj(xf, norm_mix[l][None], sc_m, sh_m, w_g, b_gate[l].reshape(1, 4 * d), seq,
                           sigmoid=True, name="branch_gates")

        wq = jnp.pad(mla_w_uq[l].reshape(MLA_Q_RANK, MLA_HEADS, MLA_NOPE + MLA_ROPE),
                     ((0, 0), (0, 0), (0, LANES - MLA_NOPE - MLA_ROPE)))
        wq = wq.reshape(MLA_Q_RANK, MLA_HEADS * LANES).astype(BF16)
        wkv = mla_w_ukv[l].reshape(MLA_KV_RANK, MLA_HEADS, MLA_NOPE + MLA_V)
        wk = jnp.pad(wkv[:, :, :MLA_NOPE], ((0, 0), (0, 0), (0, LANES - MLA_NOPE)))
        wk = wk.reshape(MLA_KV_RANK, MLA_HEADS * LANES).astype(BF16)
        wv = wkv[:, :, MLA_NOPE:].reshape(MLA_KV_RANK, MLA_HEADS * MLA_V).astype(BF16)
        q_a, k_a, vt_a = _mla_prep(p, mla_cos, mla_sin, mla_q_norm[l][None], mla_kv_norm[l][None],
                                   wq, wk, wv, batch, seq)
        o_mla = _attention("mla", batch, seq, q=q_a, k=k_a, vt=vt_a)

        k_f, vt_f = _fox_prep(p, fox_f_bias[l], batch, seq)
        o_fox = _attention("fox", batch, seq, q=p, k=k_f, vt=vt_f, q_col=COL_FOXQ)

        o_conv = _short_conv(p, conv_w[l], seq)

        q_d, k_d, vt_d = _diff_prep(p, diff_cos, diff_sin, batch, seq)
        lam_vecs = jnp.stack([diff_lam_q1[l], diff_lam_k1[l], diff_lam_q2[l], diff_lam_k2[l]])
        o_diff = _attention("diff", batch, seq, q=q_d, k=k_d, vt=vt_d,
                            extra=(lam_vecs, diff_subln[l][None]), lam_init=lam_init)

        xf = _merge((o_mla, o_fox, o_conv, o_diff), gates, w_branch[l].astype(BF16),
                    w_out[l].astype(BF16), xf, g_m, seq)

        h, idx, wts, rank, counts = _router(xf, norm_ffn[l][None], sc_f, sh_f, w_router[l],
                                            router_bias[l][None], seq)
        dest, src, tile_expert = _dispatch_plan(idx, rank, counts, min(TM_EXP, n))
        y = _expert_ffn(l, tile_expert, src, h, w_exp_gate, w_exp_up, w_exp_down)
        shared = _shared_ffn(h, w_sh_gate[l].astype(BF16), w_sh_up[l].astype(BF16),
                             w_sh_down[l].astype(BF16))
        tc = min(TC_COMB, seq)
        dest_t = dest.reshape(n // tc, tc, TOP_K).transpose(0, 2, 1).reshape(n // tc, 1, TOP_K * tc)
        xf = _combine(dest_t, y, wts, shared, xf, g_f, final_norm[None], seq, final_norm=(l == depth - 1))

    return xf.reshape(batch, seq, d)
```

```python
import functools
import math

import numpy as np
import jax
import jax.numpy as jnp
from jax import lax
from jax.experimental import pallas as pl
from jax.experimental.pallas import tpu as pltpu

F32 = jnp.float32
BF16 = jnp.bfloat16
I32 = jnp.int32

CHUNK = 64
CHUNK_SHIFT = 6
RMS_EPS = 1e-6
ROPE_THETA = 500000.0
MLA_HEADS, MLA_Q_RANK, MLA_KV_RANK, MLA_NOPE, MLA_ROPE, MLA_V = 8, 384, 256, 64, 32, 64
FOX_HEADS, FOX_HD = 8, 64
CONV_WIDTH, CONV_TAPS = 512, 3
DIFF_HEADS, DIFF_HD, DIFF_ROT = 4, 64, 16
N_EXPERTS, TOP_K, ROUTED_SCALE = 64, 8, 2.5
BRANCH_WIDTH = 512
LOG2E = 1.4426950408889634

LANES = 128
BF16_SUBLANES = 16
V7X_VMEM_BYTES = 64 * 1024 * 1024
VMEM_LIMIT = V7X_VMEM_BYTES - 8 * 1024 * 1024
NEG_BIG = -1e30

TM_PROJ = 1024
TN_PROJ = 512
TM_PREP = 512
TS_CUM = 256
TQ_ATT = 512
TK_ATT = 512
TM_MERGE = 256
TM_ROUTE = 256
TM_EXP = 256
TM_FFN = 512
TC_COMB = 64

PAIRS = 4
VT_ROWS = LANES + BF16_SUBLANES

COL_FOXQ, COL_FOXK, COL_FOXV = 0, 512, 1024
COL_CB, COL_CC, COL_CU = 1536, 2048, 2560
COL_DQ, COL_DK, COL_DV = 3072, 3584, 4096
COL_CQ = 4608
COL_KR = 4992
COL_CKV = 5120
COL_ZF = 5376
P_WIDTH = 5632


def _cparams(*sem):
    return pltpu.CompilerParams(dimension_semantics=sem, vmem_limit_bytes=VMEM_LIMIT)


def _normmod(x, g, sc, sh):
    ms = jnp.mean(x * x, axis=-1, keepdims=True)
    return (x * lax.rsqrt(ms + RMS_EPS)) * g * (1.0 + sc) + sh


def _rope_lanes(x, cos_t, sin_t, lo, half, period):
    width = x.shape[1]
    lane = lax.broadcasted_iota(I32, x.shape, 1) & (period - 1)
    x_up = pltpu.roll(x, width - half, 1)
    x_dn = pltpu.roll(x, half, 1)
    rot = jnp.where(lane < lo, 0.0,
                    jnp.where(lane < lo + half, -x_up,
                              jnp.where(lane < lo + 2 * half, x_dn, 0.0)))
    return x * cos_t + rot * sin_t


def _store_vt(vt_ref, v):
    tm = v.shape[0]
    ones = jnp.ones((VT_ROWS - LANES, tm), BF16)
    for p in range(PAIRS):
        vt_ref[0, p, 0:LANES, :] = v[:, p * LANES:(p + 1) * LANES].T.astype(BF16)
        vt_ref[0, p, LANES:VT_ROWS, :] = ones


def _vt_out(batch, seq, tm):
    per_b = seq // tm
    spec = pl.BlockSpec((1, PAIRS, VT_ROWS, tm), lambda i: (i // per_b, 0, 0, i % per_b))
    return spec, jax.ShapeDtypeStruct((batch, PAIRS, VT_ROWS, seq), BF16)


def _mod_kernel(c_ref, w_ref, b_ref, o_ref):
    c = c_ref[...]
    ca = (c * jax.nn.sigmoid(c)).astype(BF16)
    o_ref[0] = jnp.dot(ca, w_ref[0].astype(BF16), preferred_element_type=F32) + b_ref[0]


def _modulation(c, ada_w, ada_b):
    n_layers, d, d6 = ada_w.shape
    b = c.shape[0]
    rows = 8
    cp = jnp.zeros((rows, d), F32).at[:b].set(c)
    tn = 1024
    return pl.pallas_call(
        _mod_kernel,
        grid=(n_layers, d6 // tn),
        in_specs=[pl.BlockSpec((rows, d), lambda l, j: (0, 0)),
                  pl.BlockSpec((1, d, tn), lambda l, j: (l, 0, j)),
                  pl.BlockSpec((1, 1, tn), lambda l, j: (l, 0, j))],
        out_specs=pl.BlockSpec((1, rows, tn), lambda l, j: (l, 0, j)),
        out_shape=jax.ShapeDtypeStruct((n_layers, rows, d6), F32),
        compiler_params=_cparams("parallel", "parallel"),
        name="adaln_modulation",
    )(cp, ada_w, ada_b.reshape(n_layers, 1, d6))


def _proj_kernel(x_ref, g_ref, sc_ref, sh_ref, w_ref, e_ref, o_ref, h_scr, *, sigmoid):
    @pl.when(pl.program_id(1) == 0)
    def _():
        h_scr[...] = _normmod(x_ref[...], g_ref[...], sc_ref[0], sh_ref[0]).astype(BF16)

    acc = jnp.dot(h_scr[...], w_ref[0], preferred_element_type=F32)
    if sigmoid:
        acc = jax.nn.sigmoid(acc + e_ref[...])
    else:
        acc = acc * e_ref[...]
    o_ref[...] = acc.astype(o_ref.dtype)


def _norm_proj(x, g, sc, sh, w, epi, seq, *, sigmoid, name):
    n, d = x.shape
    cols = w.shape[1]
    tm, tn = min(TM_PROJ, seq), TN_PROJ
    per_b = seq // tm
    w = w.reshape(d, cols // tn, tn).transpose(1, 0, 2)
    return pl.pallas_call(
        functools.partial(_proj_kernel, sigmoid=sigmoid),
        grid=(n // tm, cols // tn),
        in_specs=[pl.BlockSpec((tm, d), lambda i, j: (i, 0)),
                  pl.BlockSpec((1, d), lambda i, j: (0, 0)),
                  pl.BlockSpec((1, 1, d), lambda i, j: (i // per_b, 0, 0)),
                  pl.BlockSpec((1, 1, d), lambda i, j: (i // per_b, 0, 0)),
                  pl.BlockSpec((1, d, tn), lambda i, j: (j, 0, 0)),
                  pl.BlockSpec((1, tn), lambda i, j: (0, j))],
        out_specs=pl.BlockSpec((tm, tn), lambda i, j: (i, j)),
        out_shape=jax.ShapeDtypeStruct((n, cols), BF16),
        scratch_shapes=[pltpu.VMEM((tm, d), BF16)],
        compiler_params=_cparams("parallel", "arbitrary"),
        name=name,
    )(x, g, sc, sh, w, epi)


def _mla_prep_kernel(cq_ref, ckv_ref, kr_ref, cos_ref, sin_ref, qn_ref, kvn_ref,
                     wq_ref, wk_ref, wv_ref, q_ref, k_ref, vt_ref, *, scale):
    cos_t, sin_t = cos_ref[...], sin_ref[...]

    def rms(t, gain):
        t = t.astype(F32)
        return (t * lax.rsqrt(jnp.mean(t * t, axis=-1, keepdims=True) + RMS_EPS) * gain).astype(BF16)

    def rope(t):
        return _rope_lanes(t, cos_t, sin_t, MLA_NOPE, MLA_ROPE // 2, LANES)

    qn = rms(cq_ref[...], qn_ref[...])
    kvn = rms(ckv_ref[...], kvn_ref[...])
    q = jnp.dot(qn, wq_ref[...], preferred_element_type=F32)
    k = jnp.dot(kvn, wk_ref[...], preferred_element_type=F32)
    _store_vt(vt_ref, jnp.dot(kvn, wv_ref[...], preferred_element_type=F32))
    k_pe = rope(kr_ref[...].astype(F32))
    for h in range(MLA_HEADS):
        sl = slice(h * LANES, (h + 1) * LANES)
        q_ref[:, sl] = (rope(q[:, sl]) * scale).astype(BF16)
        k_ref[:, sl] = (k[:, sl] + k_pe).astype(BF16)


def _mla_prep(p, cos_t, sin_t, q_norm, kv_norm, wq, wk, wv, batch, seq):
    n = p.shape[0]
    tm = min(TM_PREP, seq)
    hq = MLA_HEADS * LANES
    hv = MLA_HEADS * MLA_V
    scale = float(MLA_NOPE + MLA_ROPE) ** -0.5 * LOG2E
    const = lambda i: (0, 0)
    vt_spec, vt_shape = _vt_out(batch, seq, tm)
    return pl.pallas_call(
        functools.partial(_mla_prep_kernel, scale=scale),
        grid=(n // tm,),
        in_specs=[pl.BlockSpec((tm, MLA_Q_RANK), lambda i: (i, COL_CQ // MLA_Q_RANK)),
                  pl.BlockSpec((tm, MLA_KV_RANK), lambda i: (i, COL_CKV // MLA_KV_RANK)),
                  pl.BlockSpec((tm, LANES), lambda i: (i, COL_KR // LANES)),
                  pl.BlockSpec((tm, LANES), lambda i: (i, 0)),
                  pl.BlockSpec((tm, LANES), lambda i: (i, 0)),
                  pl.BlockSpec((1, MLA_Q_RANK), const),
                  pl.BlockSpec((1, MLA_KV_RANK), const),
                  pl.BlockSpec((MLA_Q_RANK, hq), const),
                  pl.BlockSpec((MLA_KV_RANK, hq), const),
                  pl.BlockSpec((MLA_KV_RANK, hv), const)],
        out_specs=[pl.BlockSpec((tm, hq), lambda i: (i, 0)),
                   pl.BlockSpec((tm, hq), lambda i: (i, 0)),
                   vt_spec],
        out_shape=[jax.ShapeDtypeStruct((n, hq), BF16),
                   jax.ShapeDtypeStruct((n, hq), BF16),
                   vt_shape],
        compiler_params=_cparams("parallel"),
        name="mla_prep",
    )(p, p, p, cos_t, sin_t, q_norm, kv_norm, wq, wk, wv)


def _diff_prep_kernel(q_ref, k_ref, v_ref, cos_ref, sin_ref, qo_ref, ko_ref, vt_ref):
    cos_t, sin_t = cos_ref[...], sin_ref[...]
    for src, dst in ((q_ref, qo_ref), (k_ref, ko_ref)):
        for h in range(DIFF_HEADS):
            sl = slice(h * LANES, (h + 1) * LANES)
            t = src[:, sl].astype(F32)
            dst[:, sl] = _rope_lanes(t, cos_t, sin_t, 0, DIFF_ROT // 2, DIFF_HD).astype(BF16)
    _store_vt(vt_ref, v_ref[...].astype(F32))


def _diff_prep(p, cos_t, sin_t, batch, seq):
    n = p.shape[0]
    tm = min(TM_PREP, seq)
    w = DIFF_HEADS * 2 * DIFF_HD
    vt_spec, vt_shape = _vt_out(batch, seq, tm)
    return pl.pallas_call(
        _diff_prep_kernel,
        grid=(n // tm,),
        in_specs=[pl.BlockSpec((tm, w), lambda i: (i, COL_DQ // w)),
                  pl.BlockSpec((tm, w), lambda i: (i, COL_DK // w)),
                  pl.BlockSpec((tm, w), lambda i: (i, COL_DV // w)),
                  pl.BlockSpec((tm, LANES), lambda i: (i, 0)),
                  pl.BlockSpec((tm, LANES), lambda i: (i, 0))],
        out_specs=[pl.BlockSpec((tm, w), lambda i: (i, 0)),
                   pl.BlockSpec((tm, w), lambda i: (i, 0)),
                   vt_spec],
        out_shape=[jax.ShapeDtypeStruct((n, w), BF16)] * 2 + [vt_shape],
        compiler_params=_cparams("parallel"),
        name="diff_prep",
    )(p, p, p, cos_t, sin_t)


def _fox_prep_kernel(z_ref, k_ref, v_ref, b_ref, tri_ref, sel_ref, ko_ref, vt_ref, carry):
    @pl.when(pl.program_id(1) == 0)
    def _():
        carry[...] = jnp.zeros_like(carry)

    z = z_ref[...].astype(F32) + b_ref[...]
    log_f = jnp.minimum(z, 0.0) - jnp.log(1.0 + jnp.exp(-jnp.abs(z)))
    cum = jnp.dot(tri_ref[...], log_f, precision=lax.Precision.HIGHEST,
                  preferred_element_type=F32) + carry[...]
    ts = cum.shape[0]
    carry[...] = cum[ts - 1:ts, :]
    neg = cum * (-LOG2E)
    hi = neg.astype(BF16)
    r1 = neg - hi.astype(F32)
    mid = r1.astype(BF16)
    lo = (r1 - mid.astype(F32)).astype(BF16)
    parts = jnp.concatenate([hi, mid, lo], axis=1)
    for p in range(PAIRS):
        ko_ref[:, 2 * p * LANES:(2 * p + 1) * LANES] = k_ref[:, p * LANES:(p + 1) * LANES]
        ext = jnp.dot(parts, sel_ref[p], preferred_element_type=F32)
        ko_ref[:, (2 * p + 1) * LANES:(2 * p + 2) * LANES] = ext.astype(BF16)
    _store_vt(vt_ref, v_ref[...].astype(F32))


def _fox_decay_select():
    sel = np.zeros((PAIRS, 3 * LANES, LANES), np.float32)
    for p in range(PAIRS):
        for t in range(2):
            for j in range(3):
                sel[p, j * LANES + 2 * p + t, 3 * t + j] = 1.0
    return jnp.asarray(sel, BF16)


def _fox_prep(p, f_bias, batch, seq):
    n = p.shape[0]
    ts = min(TS_CUM, seq)
    per_b = seq // ts
    w = FOX_HEADS * FOX_HD
    bias = jnp.zeros((1, LANES), F32).at[0, :FOX_HEADS].set(f_bias)
    tri = jnp.tril(jnp.ones((ts, ts), F32))
    row = lambda b, j: b * per_b + j
    return pl.pallas_call(
        _fox_prep_kernel,
        grid=(batch, per_b),
        in_specs=[pl.BlockSpec((ts, LANES), lambda b, j: (row(b, j), COL_ZF // LANES)),
                  pl.BlockSpec((ts, w), lambda b, j: (row(b, j), COL_FOXK // w)),
                  pl.BlockSpec((ts, w), lambda b, j: (row(b, j), COL_FOXV // w)),
                  pl.BlockSpec((1, LANES), lambda b, j: (0, 0)),
                  pl.BlockSpec((ts, ts), lambda b, j: (0, 0)),
                  pl.BlockSpec((PAIRS, 3 * LANES, LANES), lambda b, j: (0, 0, 0))],
        out_specs=[pl.BlockSpec((ts, 2 * w), lambda b, j: (row(b, j), 0)),
                   pl.BlockSpec((1, PAIRS, VT_ROWS, ts), lambda b, j: (b, 0, 0, j))],
        out_shape=[jax.ShapeDtypeStruct((n, 2 * w), BF16),
                   jax.ShapeDtypeStruct((batch, PAIRS, VT_ROWS, seq), BF16)],
        scratch_shapes=[pltpu.VMEM((1, LANES), F32)],
        compiler_params=_cparams("parallel", "arbitrary"),
        name="fox_prep",
    )(p, p, p, bias, tri, _fox_decay_select())


def _attn_kernel(*refs, mode, tq, tk, per_frame, lam_init):
    if mode == "diff":
        q_ref, k_ref, vt_ref, lam_ref, subln_ref, o_ref, q_scr, s_scr, m_scr, acc_scr = refs
    else:
        q_ref, k_ref, vt_ref, o_ref, q_scr, s_scr, m_scr, acc_scr = refs
    w = 2 * tq
    i = pl.program_id(2)
    q = q_ref[...]
    zero = jnp.zeros((tq, LANES), BF16)
    lane = lax.broadcasted_iota(I32, (tq, LANES), 1)
    if mode == "mla":
        q_scr[0:tq, :] = jnp.concatenate([q[:, :LANES], zero], axis=1)
        q_scr[tq:w, :] = jnp.concatenate([zero, q[:, LANES:]], axis=1)
    elif mode == "fox":
        ind_a = jnp.where(lane < 3, 1.0, 0.0).astype(BF16)
        ind_b = jnp.where(lane < 3, 0.0, jnp.where(lane < 6, 1.0, 0.0)).astype(BF16)
        q_scr[0:tq, :] = jnp.concatenate([jnp.where(lane < FOX_HD, q, zero), ind_a], axis=1)
        q_scr[tq:w, :] = jnp.concatenate([jnp.where(lane < FOX_HD, zero, q), ind_b], axis=1)
    else:
        q_scr[0:tq, :] = jnp.where(lane < DIFF_HD, q, zero)
        q_scr[tq:w, :] = jnp.where(lane < DIFF_HD, zero, q)
    m_scr[...] = jnp.full(m_scr.shape, NEG_BIG, F32)
    acc_scr[...] = jnp.zeros(acc_scr.shape, F32)

    def logits(j, slot):
        ks = pl.multiple_of(j * tk, tk)
        s_scr[slot] = lax.dot_general(k_ref[pl.ds(ks, tk), :], q_scr[...], (((1,), (1,)), ((), ())),
                                      preferred_element_type=F32)

    def consume(j, slot, masked):
        ks = pl.multiple_of(j * tk, tk)
        s = s_scr[slot]
        if masked:
            k_pos = ks + lax.broadcasted_iota(I32, (tk, w), 0)
            q_pos = i * tq + (lax.broadcasted_iota(I32, (tk, w), 1) & (tq - 1))
            if per_frame:
                visible = k_pos <= q_pos
            else:
                visible = (k_pos >> CHUNK_SHIFT) <= (q_pos >> CHUNK_SHIFT)
            s = jnp.where(visible, s, NEG_BIG)
        m_prev = m_scr[...]
        m_new = jnp.maximum(m_prev, jnp.max(s, axis=0, keepdims=True))
        alpha = jnp.exp2(m_prev - m_new)
        p = jnp.exp2(s - m_new).astype(BF16)
        acc_scr[...] = alpha * acc_scr[...] + jnp.dot(vt_ref[:, pl.ds(ks, tk)], p,
                                                      preferred_element_type=F32)
        m_scr[...] = m_new

    n_un = (i * tq) // tk
    logits(0, 0)

    def pair_body(jj, carry):
        j = 2 * jj
        logits(j + 1, 1)
        consume(j, 0, False)
        logits(j + 2, 0)
        consume(j + 1, 1, False)
        return carry

    lax.fori_loop(0, n_un // 2, pair_body, 0)

    @pl.when(n_un % 2 == 1)
    def _():
        logits(n_un, 1)
        consume(n_un - 1, 0, False)
        consume(n_un, 1, True)

    @pl.when(n_un % 2 == 0)
    def _():
        consume(n_un, 0, True)

    oa = acc_scr[0:LANES, 0:tq] / acc_scr[LANES:LANES + 1, 0:tq]
    ob = acc_scr[0:LANES, tq:w] / acc_scr[LANES:LANES + 1, tq:w]
    if mode == "diff":
        lv = lam_ref[...]
        lam = (jnp.exp(jnp.sum(lv[0:1] * lv[1:2], axis=1, keepdims=True))
               - jnp.exp(jnp.sum(lv[2:3] * lv[3:4], axis=1, keepdims=True)) + lam_init)
        o = oa - lam * ob
        o = o * lax.rsqrt(jnp.mean(o * o, axis=0, keepdims=True) + RMS_EPS)
        o = o.T * subln_ref[...] * (1.0 - lam_init)
    else:
        row = lax.broadcasted_iota(I32, (LANES, tq), 0)
        o = jnp.where(row < FOX_HD, oa, ob).T
    o_ref[...] = o.astype(BF16)


def _attention(mode, batch, seq, *, q, k, vt, q_col=0, extra=(), lam_init=0.0):
    n = batch * seq
    tq = min(TQ_ATT, seq // 2)
    tk = min(TK_ATT, seq)
    nq = seq // tq
    kw = LANES if mode == "diff" else 2 * LANES
    qw = 2 * LANES if mode == "mla" else LANES
    qb = q_col // qw
    in_specs = [pl.BlockSpec((tq, qw), lambda b, p, i: (b * nq + i, qb + p)),
                pl.BlockSpec((seq, kw), lambda b, p, i: (b, p)),
                pl.BlockSpec((1, 1, VT_ROWS, seq), lambda b, p, i: (b, p, 0, 0))]
    args = (q, k, vt)
    if mode == "diff":
        in_specs += [pl.BlockSpec((4, DIFF_HD), lambda b, p, i: (0, 0)),
                     pl.BlockSpec((1, LANES), lambda b, p, i: (0, 0))]
        args += tuple(extra)

    def body(*refs):
        refs = list(refs)
        refs[2] = refs[2].at[0, 0]
        _attn_kernel(*refs, mode=mode, tq=tq, tk=tk, per_frame=(mode == "fox"), lam_init=lam_init)

    return pl.pallas_call(
        body,
        grid=(batch, PAIRS, nq),
        in_specs=in_specs,
        out_specs=pl.BlockSpec((tq, LANES), lambda b, p, i: (b * nq + i, p)),
        out_shape=jax.ShapeDtypeStruct((n, PAIRS * LANES), BF16),
        scratch_shapes=[pltpu.VMEM((2 * tq, kw), BF16),
                        pltpu.VMEM((2, tk, 2 * tq), F32),
                        pltpu.VMEM((1, 2 * tq), F32),
                        pltpu.VMEM((VT_ROWS, 2 * tq), F32)],
        compiler_params=_cparams("parallel", "parallel", "arbitrary"),
        name="attn_" + mode,
    )(*args)


def _conv_kernel(b_ref, c_ref, u_ref, ch_ref, uh_ref, w_ref, o_ref, *, per_b):
    u = c_ref[...].astype(F32) * u_ref[...].astype(F32)
    halo = ch_ref[...].astype(F32) * uh_ref[...].astype(F32)
    halo = jnp.where(pl.program_id(0) % per_b == 0, 0.0, halo)
    hr = halo.shape[0]
    row = lax.broadcasted_iota(I32, u.shape, 0)
    u1 = jnp.where(row == 0, halo[hr - 1:hr, :], pltpu.roll(u, 1, 0))
    u2 = jnp.where(row == 0, halo[hr - 2:hr - 1, :],
                   jnp.where(row == 1, halo[hr - 1:hr, :], pltpu.roll(u, 2, 0)))
    w = w_ref[...]
    y = u2 * w[0:1, :] + u1 * w[1:2, :] + u * w[2:3, :]
    o_ref[...] = (b_ref[...].astype(F32) * y).astype(BF16)


def _short_conv(p, conv_w, seq):
    n = p.shape[0]
    ts = min(TM_PREP, seq)
    per_b = seq // ts
    w = CONV_WIDTH
    hr = BF16_SUBLANES
    halo_map = lambda col: (lambda i: (jnp.maximum(i * (ts // hr) - 1, 0), col // w))
    return pl.pallas_call(
        functools.partial(_conv_kernel, per_b=per_b),
        grid=(n // ts,),
        in_specs=[pl.BlockSpec((ts, w), lambda i: (i, COL_CB // w)),
                  pl.BlockSpec((ts, w), lambda i: (i, COL_CC // w)),
                  pl.BlockSpec((ts, w), lambda i: (i, COL_CU // w)),
                  pl.BlockSpec((hr, w), halo_map(COL_CC)),
                  pl.BlockSpec((hr, w), halo_map(COL_CU)),
                  pl.BlockSpec((CONV_TAPS, w), lambda i: (0, 0))],
        out_specs=pl.BlockSpec((ts, w), lambda i: (i, 0)),
        out_shape=jax.ShapeDtypeStruct((n, w), BF16),
        compiler_params=_cparams("parallel"),
        name="short_conv",
    )(p, p, p, p, p, conv_w)


def _merge_kernel(o0, o1, o2, o3, g0, g1, g2, g3, wb_ref, wo_ref, x_ref, gm_ref, out_ref):
    merged = None
    for br, (o_ref, g_ref) in enumerate(((o0, g0), (o1, g1), (o2, g2), (o3, g3))):
        t = jnp.dot(o_ref[...], wb_ref[br], preferred_element_type=F32) * g_ref[...].astype(F32)
        merged = t if merged is None else merged + t
    y = jnp.dot(merged.astype(BF16), wo_ref[...], preferred_element_type=F32)
    out_ref[...] = x_ref[...] + gm_ref[0] * y


def _merge(branches, gates, w_branch, w_out, x, g_m, seq):
    n, d = x.shape
    tm = min(TM_MERGE, seq)
    per_b = seq // tm
    bw = BRANCH_WIDTH
    o_spec = pl.BlockSpec((tm, bw), lambda i: (i, 0))
    g_specs = [pl.BlockSpec((tm, d), lambda i, br=br: (i, br)) for br in range(4)]
    single = pl.Buffered(1)
    return pl.pallas_call(
        _merge_kernel,
        grid=(n // tm,),
        in_specs=[o_spec] * 4 + g_specs + [
            pl.BlockSpec((4, bw, d), lambda i: (0, 0, 0), pipeline_mode=single),
            pl.BlockSpec((d, d), lambda i: (0, 0), pipeline_mode=single),
            pl.BlockSpec((tm, d), lambda i: (i, 0)),
            pl.BlockSpec((1, 1, d), lambda i: (i // per_b, 0, 0))],
        out_specs=pl.BlockSpec((tm, d), lambda i: (i, 0)),
        out_shape=jax.ShapeDtypeStruct((n, d), F32),
        compiler_params=_cparams("parallel"),
        name="branch_merge",
    )(*branches, gates, gates, gates, gates, w_branch, w_out, x, g_m)


def _router_kernel(x_ref, g_ref, sc_ref, sh_ref, wr_ref, rb_ref, tri_ref,
                   hb_ref, hs_ref, idx_ref, wts_ref, rank_ref, cnt_ref, carry):
    @pl.when(pl.program_id(0) == 0)
    def _():
        carry[...] = jnp.zeros_like(carry)

    h = _normmod(x_ref[...], g_ref[...], sc_ref[0], sh_ref[0])
    hb_ref[...] = h.astype(BF16)
    slab = h.shape[1] // LANES
    for s in range(slab):
        hs_ref[pl.ds(s, h.shape[0], stride=slab), :] = h[:, s * LANES:(s + 1) * LANES]
    logits = jnp.dot(h, wr_ref[...], precision=lax.Precision.HIGHEST, preferred_element_type=F32)
    scores = jax.nn.sigmoid(logits)
    work = scores + rb_ref[...]
    tm, ne = scores.shape
    lane = lax.broadcasted_iota(I32, (tm, ne), 1).astype(F32)
    slot = lax.broadcasted_iota(I32, (tm, TOP_K), 1)
    idx = jnp.zeros((tm, TOP_K), F32)
    sel = jnp.zeros((tm, TOP_K), F32)
    hits = []
    chosen = jnp.zeros((tm, ne), F32)
    for kk in range(TOP_K):
        mx = jnp.max(work, axis=1, keepdims=True)
        first = jnp.min(jnp.where(work == mx, lane, float(ne)), axis=1, keepdims=True)
        hit = lane == first
        hits.append(hit)
        chosen = jnp.where(hit, 1.0, chosen)
        sc_k = jnp.sum(jnp.where(hit, scores, 0.0), axis=1, keepdims=True)
        work = jnp.where(hit, -jnp.inf, work)
        idx = jnp.where(slot == kk, first, idx)
        sel = jnp.where(slot == kk, sc_k, sel)
    idx_ref[...] = idx.astype(I32)
    wts_ref[...] = sel / jnp.sum(sel, axis=1, keepdims=True) * ROUTED_SCALE
    before = jnp.dot(tri_ref[...], chosen.astype(BF16), preferred_element_type=F32) + carry[...]
    rank = jnp.zeros((tm, TOP_K), F32)
    for kk in range(TOP_K):
        r_k = jnp.sum(jnp.where(hits[kk], before, 0.0), axis=1, keepdims=True)
        rank = jnp.where(slot == kk, r_k, rank)
    rank_ref[...] = rank.astype(I32)
    total = carry[...] + jnp.sum(chosen, axis=0, keepdims=True)
    carry[...] = total
    cnt_ref[...] = total.astype(I32)


def _router(x, g, sc, sh, w_router, router_bias, seq):
    n, d = x.shape
    tm = min(TM_ROUTE, seq)
    per_b = seq // tm
    ne = w_router.shape[1]
    tri = jnp.tril(jnp.ones((tm, tm), F32), k=-1).astype(BF16)
    return pl.pallas_call(
        _router_kernel,
        grid=(n // tm,),
        in_specs=[pl.BlockSpec((tm, d), lambda i: (i, 0)),
                  pl.BlockSpec((1, d), lambda i: (0, 0)),
                  pl.BlockSpec((1, 1, d), lambda i: (i // per_b, 0, 0)),
                  pl.BlockSpec((1, 1, d), lambda i: (i // per_b, 0, 0)),
                  pl.BlockSpec((d, ne), lambda i: (0, 0)),
                  pl.BlockSpec((1, ne), lambda i: (0, 0)),
                  pl.BlockSpec((tm, tm), lambda i: (0, 0))],
        out_specs=[pl.BlockSpec((tm, d), lambda i: (i, 0)),
                   pl.BlockSpec((tm * (d // LANES), LANES), lambda i: (i, 0)),
                   pl.BlockSpec((tm, TOP_K), lambda i: (i, 0)),
                   pl.BlockSpec((tm, TOP_K), lambda i: (i, 0)),
                   pl.BlockSpec((tm, TOP_K), lambda i: (i, 0)),
                   pl.BlockSpec((1, ne), lambda i: (0, 0))],
        out_shape=[jax.ShapeDtypeStruct((n, d), BF16),
                   jax.ShapeDtypeStruct((n * (d // LANES), LANES), F32),
                   jax.ShapeDtypeStruct((n, TOP_K), I32),
                   jax.ShapeDtypeStruct((n, TOP_K), F32),
                   jax.ShapeDtypeStruct((n, TOP_K), I32),
                   jax.ShapeDtypeStruct((1, ne), I32)],
        scratch_shapes=[pltpu.VMEM((1, ne), F32)],
        compiler_params=_cparams("arbitrary"),
        name="moe_router",
    )(x, g, sc, sh, w_router, router_bias, tri)


def _gather_step(i, slot, n_steps, idx_hbm, idx_smem, idx_sem, data_hbm, buf, data_sem, n_rows, slab):
    def idx_copy(step, slot):
        return pltpu.make_async_copy(idx_hbm.at[step], idx_smem.at[pl.ds(slot, 1)], idx_sem.at[slot])

    def rows_wait(slot):
        pltpu.make_async_copy(data_hbm.at[pl.ds(0, n_rows * slab)], buf.at[slot], data_sem.at[slot]).wait()

    def issue_rows(slot):
        for r in range(n_rows):
            row = pl.multiple_of(idx_smem[slot, r], slab)
            pltpu.make_async_copy(data_hbm.at[pl.ds(row, slab)], buf.at[slot, pl.ds(r * slab, slab)],
                                  data_sem.at[slot]).start(priority=r % 2)

    nxt = 1 - slot
    last = n_steps - 1

    if slot == 0:
        @pl.when(i == 0)
        def _():
            idx_copy(0, 0).start()
            idx_copy(0, 0).wait()
            issue_rows(0)
            idx_copy(jnp.minimum(1, last), 1).start()

    rows_wait(slot)
    idx_copy(jnp.minimum(i + 1, last), nxt).wait()

    def prefetch():
        issue_rows(nxt)
        idx_copy(jnp.minimum(i + 2, last), slot).start()

    def drain():
        @pl.when(i == last)
        def _():
            rows_wait(nxt)
            idx_copy(last, slot).wait()

    return prefetch, drain


def _for_each_parity(i, step):
    for slot in range(2):
        @pl.when(i % 2 == slot)
        def _():
            step(slot)


def _expert_kernel(te_ref, src_hbm, h_hbm, wg_ref, wu_ref, wd_ref, y_ref,
                   idx_smem, idx_sem, xbuf, x_sem, wg_s, wu_s, wd_s, xs_s, *, n_steps, tm, slab):
    i = pl.program_id(0)
    new_expert = jnp.logical_or(i == 0, te_ref[i] != te_ref[jnp.maximum(i - 1, 0)])

    @pl.when(new_expert)
    def _():
        wg_s[...] = wg_ref[0, 0].astype(BF16)
        wu_s[...] = wu_ref[0, 0].astype(BF16)
        wd_s[...] = wd_ref[0, 0].astype(BF16)

    def step(slot):
        prefetch, drain = _gather_step(i, slot, n_steps, src_hbm, idx_smem, idx_sem, h_hbm, xbuf, x_sem,
                                       tm, slab)
        xcur = xbuf.at[slot]
        for s in range(slab):
            xs_s[:, s * LANES:(s + 1) * LANES] = xcur[pl.ds(s, tm, stride=slab), :].astype(BF16)
        prefetch()
        x = xs_s[...]
        a = jnp.dot(x, wg_s[...], preferred_element_type=F32)
        u = jnp.dot(x, wu_s[...], preferred_element_type=F32)
        hid = (a * jax.nn.sigmoid(a)) * u
        y = jnp.dot(hid.astype(BF16), wd_s[...], preferred_element_type=F32)
        for s in range(slab):
            y_ref[pl.ds(s, tm, stride=slab), :] = y[:, s * LANES:(s + 1) * LANES]
        drain()

    _for_each_parity(i, step)


def _expert_ffn(layer, tile_expert, src, h_slab, w_eg, w_eu, w_ed):
    n_steps, _, tm = src.shape
    d = w_eg.shape[2]
    ff = w_eg.shape[3]
    slab = d // LANES
    grid_spec = pltpu.PrefetchScalarGridSpec(
        num_scalar_prefetch=1,
        grid=(n_steps,),
        in_specs=[pl.BlockSpec(memory_space=pl.ANY),
                  pl.BlockSpec(memory_space=pl.ANY),
                  pl.BlockSpec((1, 1, d, ff), lambda i, te: (layer, te[i], 0, 0)),
                  pl.BlockSpec((1, 1, d, ff), lambda i, te: (layer, te[i], 0, 0)),
                  pl.BlockSpec((1, 1, ff, d), lambda i, te: (layer, te[i], 0, 0))],
        out_specs=pl.BlockSpec((tm * slab, LANES), lambda i, te: (i, 0)),
        scratch_shapes=[pltpu.SMEM((2, tm), I32),
                        pltpu.SemaphoreType.DMA((2,)),
                        pltpu.VMEM((2, tm * slab, LANES), F32),
                        pltpu.SemaphoreType.DMA((2,)),
                        pltpu.VMEM((d, ff), BF16),
                        pltpu.VMEM((d, ff), BF16),
                        pltpu.VMEM((ff, d), BF16),
                        pltpu.VMEM((tm, d), BF16)],
    )
    return pl.pallas_call(
        functools.partial(_expert_kernel, n_steps=n_steps, tm=tm, slab=slab),
        grid_spec=grid_spec,
        out_shape=jax.ShapeDtypeStruct((n_steps * tm * slab, LANES), F32),
        compiler_params=_cparams("arbitrary"),
        name="moe_experts",
    )(tile_expert, src, h_slab, w_eg, w_eu, w_ed)


def _shared_kernel(h_ref, wg_ref, wu_ref, wd_ref, o_ref):
    h = h_ref[...].astype(BF16)
    a = jnp.dot(h, wg_ref[...], preferred_element_type=F32)
    u = jnp.dot(h, wu_ref[...], preferred_element_type=F32)
    hid = ((a * jax.nn.sigmoid(a)) * u).astype(BF16)
    o_ref[...] = jnp.dot(hid, wd_ref[...], preferred_element_type=F32)


def _shared_ffn(h, wg, wu, wd):
    n, d = h.shape
    ff = wg.shape[1]
    tm = min(TM_FFN, n)
    return pl.pallas_call(
        _shared_kernel,
        grid=(n // tm,),
        in_specs=[pl.BlockSpec((tm, d), lambda i: (i, 0)),
                  pl.BlockSpec((d, ff), lambda i: (0, 0)),
                  pl.BlockSpec((d, ff), lambda i: (0, 0)),
                  pl.BlockSpec((ff, d), lambda i: (0, 0))],
        out_specs=pl.BlockSpec((tm, d), lambda i: (i, 0)),
        out_shape=jax.ShapeDtypeStruct((n, d), F32),
        compiler_params=_cparams("parallel"),
        name="moe_shared",
    )(h, wg, wu, wd)


def _combine_kernel(dest_hbm, y_hbm, wts_ref, sh_ref, x_ref, gf_ref, fn_ref, o_ref,
                    idx_smem, idx_sem, ybuf, y_sem, *, n_steps, tc, slab, final_norm):
    i = pl.program_id(0)
    rows = 8
    d = slab * LANES

    def step(slot):
        prefetch, drain = _gather_step(i, slot, n_steps, dest_hbm, idx_smem, idx_sem, y_hbm, ybuf, y_sem,
                                       TOP_K * tc, slab)
        prefetch()
        ycur = ybuf.at[slot]
        for c in range(tc // rows):
            r0 = c * rows
            wts = wts_ref[r0:r0 + rows, :]
            wb = [jnp.broadcast_to(wts[:, kk:kk + 1], (rows, LANES)) for kk in range(TOP_K)]
            outs = []
            for s in range(slab):
                sl = slice(s * LANES, (s + 1) * LANES)
                acc = sh_ref[r0:r0 + rows, sl]
                for kk in range(TOP_K):
                    acc = acc + wb[kk] * ycur[pl.ds((kk * tc + r0) * slab + s, rows, stride=slab), :]
                outs.append(x_ref[r0:r0 + rows, sl] + gf_ref[0, :, sl] * acc)
            if final_norm:
                ssq = outs[0] * outs[0]
                for o in outs[1:]:
                    ssq = ssq + o * o
                inv = lax.rsqrt(jnp.sum(ssq, axis=1, keepdims=True) * (1.0 / d) + RMS_EPS)
                outs = [o * inv * fn_ref[:, s * LANES:(s + 1) * LANES] for s, o in enumerate(outs)]
            for s, o in enumerate(outs):
                o_ref[r0:r0 + rows, s * LANES:(s + 1) * LANES] = o
        drain()

    _for_each_parity(i, step)


def _combine(dest, y, wts, shared, x, g_f, final_gain, seq, *, final_norm):
    n, d = x.shape
    n_steps, _, rows = dest.shape
    tc = rows // TOP_K
    per_b = seq // tc
    slab = d // LANES
    return pl.pallas_call(
        functools.partial(_combine_kernel, n_steps=n_steps, tc=tc, slab=slab, final_norm=final_norm),
        grid=(n_steps,),
        in_specs=[pl.BlockSpec(memory_space=pl.ANY),
                  pl.BlockSpec(memory_space=pl.ANY),
                  pl.BlockSpec((tc, TOP_K), lambda i: (i, 0)),
                  pl.BlockSpec((tc, d), lambda i: (i, 0)),
                  pl.BlockSpec((tc, d), lambda i: (i, 0)),
                  pl.BlockSpec((1, 1, d), lambda i: (i // per_b, 0, 0)),
                  pl.BlockSpec((1, d), lambda i: (0, 0))],
        out_specs=pl.BlockSpec((tc, d), lambda i: (i, 0)),
        out_shape=jax.ShapeDtypeStruct((n, d), F32),
        scratch_shapes=[pltpu.SMEM((2, rows), I32),
                        pltpu.SemaphoreType.DMA((2,)),
                        pltpu.VMEM((2, rows * slab, LANES), F32),
                        pltpu.SemaphoreType.DMA((2,))],
        compiler_params=_cparams("arbitrary"),
        name="moe_combine",
    )(dest, y, wts, shared, x, g_f, final_gain)


def _rope_tables(positions):
    pos = positions.reshape(-1).astype(F32)
    n = pos.shape[0]

    def cos_sin(half):
        inv_freq = ROPE_THETA ** (-jnp.arange(half, dtype=F32) / half)
        ang = pos[:, None] * inv_freq
        return jnp.cos(ang), jnp.sin(ang)

    cm, sm = cos_sin(MLA_ROPE // 2)
    tail = LANES - MLA_NOPE - MLA_ROPE
    mla_cos = jnp.concatenate([jnp.ones((n, MLA_NOPE), F32), cm, cm, jnp.ones((n, tail), F32)], axis=1)
    mla_sin = jnp.concatenate([jnp.zeros((n, MLA_NOPE), F32), sm, sm, jnp.zeros((n, tail), F32)], axis=1)
    cd, sd = cos_sin(DIFF_ROT // 2)
    rest = DIFF_HD - DIFF_ROT
    dc = jnp.concatenate([cd, cd, jnp.ones((n, rest), F32)], axis=1)
    ds = jnp.concatenate([sd, sd, jnp.zeros((n, rest), F32)], axis=1)
    return mla_cos, mla_sin, jnp.concatenate([dc, dc], axis=1), jnp.concatenate([ds, ds], axis=1)


def _in_proj_weight(w_in):
    d = w_in.shape[0]
    sizes = (MLA_Q_RANK, MLA_KV_RANK, MLA_ROPE, 512, 512, 512, FOX_HEADS, 512, 512, 512, 512, 512, 512)
    parts, off = [], 0
    for s in sizes:
        parts.append(w_in[:, off:off + s])
        off += s
    a_cq, a_ckv, a_kr, b_q, b_k, b_v, b_zf, c_b, c_c, c_u, d_q, d_k, d_v = parts
    zeros = lambda w: jnp.zeros((d, w), w_in.dtype)
    cols = [b_q, b_k, b_v, c_b, c_c, c_u, d_q, d_k, d_v, a_cq,
            zeros(MLA_NOPE), a_kr, zeros(LANES - MLA_NOPE - MLA_ROPE),
            a_ckv, b_zf, zeros(LANES - FOX_HEADS), zeros(P_WIDTH - COL_ZF - LANES)]
    return jnp.concatenate(cols, axis=1).astype(BF16)


def _in_proj_scale():
    s = np.ones((1, P_WIDTH), np.float32)
    qs = FOX_HD ** -0.5 * LOG2E
    s[0, COL_FOXQ:COL_FOXQ + FOX_HEADS * FOX_HD] = qs
    s[0, COL_DQ:COL_DQ + DIFF_HEADS * 2 * DIFF_HD] = qs
    return jnp.asarray(s)


def _dispatch_plan(idx, rank, counts, tm):
    n = idx.shape[0]
    counts = counts.reshape(-1)
    padded = (counts + tm - 1) // tm * tm
    ends = jnp.cumsum(padded)
    starts = ends - padded
    dest = (starts[idx] + rank).astype(I32)
    n_steps = (n * TOP_K) // tm + N_EXPERTS
    tok = jnp.repeat(jnp.arange(n, dtype=I32), TOP_K)
    src = jnp.zeros((n_steps * tm,), I32).at[dest.reshape(-1)].set(
        tok, unique_indices=True, indices_are_sorted=False, mode="promise_in_bounds")
    tile_start = jnp.arange(n_steps, dtype=I32) * tm
    tile_expert = jnp.sum((ends[None, :] <= tile_start[:, None]).astype(I32), axis=1)
    tile_expert = jnp.minimum(tile_expert, N_EXPERTS - 1)
    return dest, src.reshape(n_steps, 1, tm), tile_expert


def kernel(x, c, positions, ada_w, ada_b, norm_mix, norm_ffn, w_in, mla_q_norm, mla_w_uq, mla_kv_norm,
           mla_w_ukv, fox_f_bias, conv_w, diff_lam_q1, diff_lam_k1, diff_lam_q2, diff_lam_k2, diff_subln,
           w_branch, w_gate, b_gate, w_out, w_router, router_bias, w_exp_gate, w_exp_up, w_exp_down,
           w_sh_gate, w_sh_up, w_sh_down, final_norm):
    batch, seq, d = x.shape
    n = batch * seq
    depth = ada_w.shape[0]
    xf = x.reshape(n, d)

    mod = _modulation(c, ada_w, ada_b)
    mla_cos, mla_sin, diff_cos, diff_sin = _rope_tables(positions)
    p_scale = _in_proj_scale()

    for l in range(depth):
        lam_init = 0.8 - 0.6 * math.exp(-0.3 * l)
        mods = mod[l, :batch].reshape(batch, 6, 1, d)
        sh_m, sc_m, g_m, sh_f, sc_f, g_f = (mods[:, t] for t in range(6))

        p = _norm_proj(xf, norm_mix[l][None], sc_m, sh_m, _in_proj_weight(w_in[l]), p_scale, seq,
                       sigmoid=False, name="in_proj")
        w_g = w_gate[l].transpose(1, 0, 2).reshape(d, 4 * d).astype(BF16)
        gates = _norm_proj(xf, norm_mix[l][None], sc_m, sh_m, w_g, b_gate[l].reshape(1, 4 * d), seq,
                           sigmoid=True, name="branch_gates")

        wq = jnp.pad(mla_w_uq[l].reshape(MLA_Q_RANK, MLA_HEADS, MLA_NOPE + MLA_ROPE),
                     ((0, 0), (0, 0), (0, LANES - MLA_NOPE - MLA_ROPE)))
        wq = wq.reshape(MLA_Q_RANK, MLA_HEADS * LANES).astype(BF16)
        wkv = mla_w_ukv[l].reshape(MLA_KV_RANK, MLA_HEADS, MLA_NOPE + MLA_V)
        wk = jnp.pad(wkv[:, :, :MLA_NOPE], ((0, 0), (0, 0), (0, LANES - MLA_NOPE)))
        wk = wk.reshape(MLA_KV_RANK, MLA_HEADS * LANES).astype(BF16)
        wv = wkv[:, :, MLA_NOPE:].reshape(MLA_KV_RANK, MLA_HEADS * MLA_V).astype(BF16)
        q_a, k_a, vt_a = _mla_prep(p, mla_cos, mla_sin, mla_q_norm[l][None], mla_kv_norm[l][None],
                                   wq, wk, wv, batch, seq)
        o_mla = _attention("mla", batch, seq, q=q_a, k=k_a, vt=vt_a)

        k_f, vt_f = _fox_prep(p, fox_f_bias[l], batch, seq)
        o_fox = _attention("fox", batch, seq, q=p, k=k_f, vt=vt_f, q_col=COL_FOXQ)

        o_conv = _short_conv(p, conv_w[l], seq)

        q_d, k_d, vt_d = _diff_prep(p, diff_cos, diff_sin, batch, seq)
        lam_vecs = jnp.stack([diff_lam_q1[l], diff_lam_k1[l], diff_lam_q2[l], diff_lam_k2[l]])
        o_diff = _attention("diff", batch, seq, q=q_d, k=k_d, vt=vt_d,
                            extra=(lam_vecs, diff_subln[l][None]), lam_init=lam_init)

        xf = _merge((o_mla, o_fox, o_conv, o_diff), gates, w_branch[l].astype(BF16),
                    w_out[l].astype(BF16), xf, g_m, seq)

        h, h_slab, idx, wts, rank, counts = _router(xf, norm_ffn[l][None], sc_f, sh_f, w_router[l],
                                                    router_bias[l][None], seq)
        dest, src, tile_expert = _dispatch_plan(idx, rank, counts, min(TM_EXP, n))
        slab = d // LANES
        y = _expert_ffn(l, tile_expert, src * slab, h_slab, w_exp_gate, w_exp_up, w_exp_down)
        shared = _shared_ffn(h, w_sh_gate[l].astype(BF16), w_sh_up[l].astype(BF16),
                             w_sh_down[l].astype(BF16))
        tc = min(TC_COMB, seq)
        dest_t = dest.reshape(n // tc, tc, TOP_K).transpose(0, 2, 1).reshape(n // tc, 1, TOP_K * tc)
        xf = _combine(dest_t * slab, y, wts, shared, xf, g_f, final_norm[None], seq,
                      final_norm=(l == depth - 1))

    return xf.reshape(batch, seq, d)
```

```python
import functools
import math

import numpy as np
import jax
import jax.numpy as jnp
from jax import lax
from jax.experimental import pallas as pl
from jax.experimental.pallas import tpu as pltpu

F32 = jnp.float32
BF16 = jnp.bfloat16
I32 = jnp.int32

CHUNK = 64
CHUNK_SHIFT = 6
RMS_EPS = 1e-6
ROPE_THETA = 500000.0
MLA_HEADS, MLA_Q_RANK, MLA_KV_RANK, MLA_NOPE, MLA_ROPE, MLA_V = 8, 384, 256, 64, 32, 64
FOX_HEADS, FOX_HD = 8, 64
CONV_WIDTH, CONV_TAPS = 512, 3
DIFF_HEADS, DIFF_HD, DIFF_ROT = 4, 64, 16
N_EXPERTS, TOP_K, ROUTED_SCALE = 64, 8, 2.5
BRANCH_WIDTH = 512
LOG2E = 1.4426950408889634

LANES = 128
BF16_SUBLANES = 16
V7X_VMEM_BYTES = 64 * 1024 * 1024
VMEM_LIMIT = V7X_VMEM_BYTES - 8 * 1024 * 1024
NEG_BIG = -1e30

TM_PROJ = 1024
TM_PROJ_CHUNK = 256
TN_PROJ = 512
TM_PREP = 512
TS_CUM = 256
TQ_ATT = 512
TK_ATT = 512
ATT_UNROLL = 4
TM_MERGE = 256
TM_ROUTE = 256
TM_EXP = 256
TM_FFN = 512
TC_COMB = 64

PAIRS = 4
VT_ROWS = LANES + BF16_SUBLANES

COL_FOXQ, COL_FOXK, COL_FOXV = 0, 512, 1024
COL_CB, COL_CC, COL_CU = 1536, 2048, 2560
COL_DQ, COL_DK, COL_DV = 3072, 3584, 4096
COL_CQ = 4608
COL_KR = 4992
COL_CKV = 5120
COL_ZF = 5376
P_WIDTH = 5632


def _cparams(*sem):
    return pltpu.CompilerParams(dimension_semantics=sem, vmem_limit_bytes=VMEM_LIMIT)


def _normmod(x, g, sc, sh):
    ms = jnp.mean(x * x, axis=-1, keepdims=True)
    return (x * lax.rsqrt(ms + RMS_EPS)) * g * (1.0 + sc) + sh


def _rope_lanes(x, cos_t, sin_t, lo, half, period):
    width = x.shape[1]
    lane = lax.broadcasted_iota(I32, x.shape, 1) & (period - 1)
    x_up = pltpu.roll(x, width - half, 1)
    x_dn = pltpu.roll(x, half, 1)
    rot = jnp.where(lane < lo, 0.0,
                    jnp.where(lane < lo + half, -x_up,
                              jnp.where(lane < lo + 2 * half, x_dn, 0.0)))
    return x * cos_t + rot * sin_t


def _store_vt(vt_ref, v):
    tm = v.shape[0]
    ones = jnp.ones((VT_ROWS - LANES, tm), BF16)
    for p in range(PAIRS):
        vt_ref[0, p, 0:LANES, :] = v[:, p * LANES:(p + 1) * LANES].T.astype(BF16)
        vt_ref[0, p, LANES:VT_ROWS, :] = ones


def _vt_out(batch, seq, tm):
    per_b = seq // tm
    spec = pl.BlockSpec((1, PAIRS, VT_ROWS, tm), lambda i: (i // per_b, 0, 0, i % per_b))
    return spec, jax.ShapeDtypeStruct((batch, PAIRS, VT_ROWS, seq), BF16)


def _mod_kernel(c_ref, w_ref, b_ref, o_ref):
    c = c_ref[...]
    ca = (c * jax.nn.sigmoid(c)).astype(BF16)
    o_ref[0] = jnp.dot(ca, w_ref[0].astype(BF16), preferred_element_type=F32) + b_ref[0]


def _modulation(c, ada_w, ada_b):
    n_layers, d, d6 = ada_w.shape
    b = c.shape[0]
    rows = 8
    cp = jnp.zeros((rows, d), F32).at[:b].set(c)
    tn = 1024
    return pl.pallas_call(
        _mod_kernel,
        grid=(n_layers, d6 // tn),
        in_specs=[pl.BlockSpec((rows, d), lambda l, j: (0, 0)),
                  pl.BlockSpec((1, d, tn), lambda l, j: (l, 0, j)),
                  pl.BlockSpec((1, 1, tn), lambda l, j: (l, 0, j))],
        out_specs=pl.BlockSpec((1, rows, tn), lambda l, j: (l, 0, j)),
        out_shape=jax.ShapeDtypeStruct((n_layers, rows, d6), F32),
        compiler_params=_cparams("parallel", "parallel"),
        name="adaln_modulation",
    )(cp, ada_w, ada_b.reshape(n_layers, 1, d6))


def _proj_kernel(x_ref, g_ref, sc_ref, sh_ref, w_ref, e_ref, o_ref, h_scr, *, sigmoid):
    @pl.when(pl.program_id(1) == 0)
    def _():
        h_scr[...] = _normmod(x_ref[...], g_ref[...], sc_ref[0], sh_ref[0]).astype(BF16)

    mc = min(TM_PROJ_CHUNK, h_scr.shape[0])
    for cc in range(h_scr.shape[0] // mc):
        rows = slice(cc * mc, (cc + 1) * mc)
        acc = jnp.dot(h_scr[rows, :], w_ref[0], preferred_element_type=F32)
        if sigmoid:
            acc = jax.nn.sigmoid(acc + e_ref[...])
        else:
            acc = acc * e_ref[...]
        o_ref[rows, :] = acc.astype(o_ref.dtype)


def _norm_proj(x, g, sc, sh, w, epi, seq, *, sigmoid, name):
    n, d = x.shape
    cols = w.shape[1]
    tm, tn = min(TM_PROJ, seq), TN_PROJ
    per_b = seq // tm
    w = w.reshape(d, cols // tn, tn).transpose(1, 0, 2)
    return pl.pallas_call(
        functools.partial(_proj_kernel, sigmoid=sigmoid),
        grid=(n // tm, cols // tn),
        in_specs=[pl.BlockSpec((tm, d), lambda i, j: (i, 0)),
                  pl.BlockSpec((1, d), lambda i, j: (0, 0)),
                  pl.BlockSpec((1, 1, d), lambda i, j: (i // per_b, 0, 0)),
                  pl.BlockSpec((1, 1, d), lambda i, j: (i // per_b, 0, 0)),
                  pl.BlockSpec((1, d, tn), lambda i, j: (j, 0, 0)),
                  pl.BlockSpec((1, tn), lambda i, j: (0, j))],
        out_specs=pl.BlockSpec((tm, tn), lambda i, j: (i, j)),
        out_shape=jax.ShapeDtypeStruct((n, cols), BF16),
        scratch_shapes=[pltpu.VMEM((tm, d), BF16)],
        compiler_params=_cparams("parallel", "arbitrary"),
        name=name,
    )(x, g, sc, sh, w, epi)


def _mla_prep_kernel(cq_ref, ckv_ref, kr_ref, cos_ref, sin_ref, qn_ref, kvn_ref,
                     wq_ref, wk_ref, wv_ref, q_ref, k_ref, vt_ref, *, scale):
    cos_t, sin_t = cos_ref[...], sin_ref[...]

    def rms(t, gain):
        t = t.astype(F32)
        return (t * lax.rsqrt(jnp.mean(t * t, axis=-1, keepdims=True) + RMS_EPS) * gain).astype(BF16)

    def rope(t):
        return _rope_lanes(t, cos_t, sin_t, MLA_NOPE, MLA_ROPE // 2, LANES)

    qn = rms(cq_ref[...], qn_ref[...])
    kvn = rms(ckv_ref[...], kvn_ref[...])
    q = jnp.dot(qn, wq_ref[...], preferred_element_type=F32)
    k = jnp.dot(kvn, wk_ref[...], preferred_element_type=F32)
    _store_vt(vt_ref, jnp.dot(kvn, wv_ref[...], preferred_element_type=F32))
    k_pe = rope(kr_ref[...].astype(F32))
    for h in range(MLA_HEADS):
        sl = slice(h * LANES, (h + 1) * LANES)
        q_ref[:, sl] = (rope(q[:, sl]) * scale).astype(BF16)
        k_ref[:, sl] = (k[:, sl] + k_pe).astype(BF16)


def _mla_prep(p, cos_t, sin_t, q_norm, kv_norm, wq, wk, wv, batch, seq):
    n = p.shape[0]
    tm = min(TM_PREP, seq)
    hq = MLA_HEADS * LANES
    hv = MLA_HEADS * MLA_V
    scale = float(MLA_NOPE + MLA_ROPE) ** -0.5 * LOG2E
    const = lambda i: (0, 0)
    vt_spec, vt_shape = _vt_out(batch, seq, tm)
    return pl.pallas_call(
        functools.partial(_mla_prep_kernel, scale=scale),
        grid=(n // tm,),
        in_specs=[pl.BlockSpec((tm, MLA_Q_RANK), lambda i: (i, COL_CQ // MLA_Q_RANK)),
                  pl.BlockSpec((tm, MLA_KV_RANK), lambda i: (i, COL_CKV // MLA_KV_RANK)),
                  pl.BlockSpec((tm, LANES), lambda i: (i, COL_KR // LANES)),
                  pl.BlockSpec((tm, LANES), lambda i: (i, 0)),
                  pl.BlockSpec((tm, LANES), lambda i: (i, 0)),
                  pl.BlockSpec((1, MLA_Q_RANK), const),
                  pl.BlockSpec((1, MLA_KV_RANK), const),
                  pl.BlockSpec((MLA_Q_RANK, hq), const),
                  pl.BlockSpec((MLA_KV_RANK, hq), const),
                  pl.BlockSpec((MLA_KV_RANK, hv), const)],
        out_specs=[pl.BlockSpec((tm, hq), lambda i: (i, 0)),
                   pl.BlockSpec((tm, hq), lambda i: (i, 0)),
                   vt_spec],
        out_shape=[jax.ShapeDtypeStruct((n, hq), BF16),
                   jax.ShapeDtypeStruct((n, hq), BF16),
                   vt_shape],
        compiler_params=_cparams("parallel"),
        name="mla_prep",
    )(p, p, p, cos_t, sin_t, q_norm, kv_norm, wq, wk, wv)


def _diff_prep_kernel(q_ref, k_ref, v_ref, cos_ref, sin_ref, qo_ref, ko_ref, vt_ref):
    cos_t, sin_t = cos_ref[...], sin_ref[...]
    for src, dst in ((q_ref, qo_ref), (k_ref, ko_ref)):
        for h in range(DIFF_HEADS):
            sl = slice(h * LANES, (h + 1) * LANES)
            t = src[:, sl].astype(F32)
            dst[:, sl] = _rope_lanes(t, cos_t, sin_t, 0, DIFF_ROT // 2, DIFF_HD).astype(BF16)
    _store_vt(vt_ref, v_ref[...].astype(F32))


def _diff_prep(p, cos_t, sin_t, batch, seq):
    n = p.shape[0]
    tm = min(TM_PREP, seq)
    w = DIFF_HEADS * 2 * DIFF_HD
    vt_spec, vt_shape = _vt_out(batch, seq, tm)
    return pl.pallas_call(
        _diff_prep_kernel,
        grid=(n // tm,),
        in_specs=[pl.BlockSpec((tm, w), lambda i: (i, COL_DQ // w)),
                  pl.BlockSpec((tm, w), lambda i: (i, COL_DK // w)),
                  pl.BlockSpec((tm, w), lambda i: (i, COL_DV // w)),
                  pl.BlockSpec((tm, LANES), lambda i: (i, 0)),
                  pl.BlockSpec((tm, LANES), lambda i: (i, 0))],
        out_specs=[pl.BlockSpec((tm, w), lambda i: (i, 0)),
                   pl.BlockSpec((tm, w), lambda i: (i, 0)),
                   vt_spec],
        out_shape=[jax.ShapeDtypeStruct((n, w), BF16)] * 2 + [vt_shape],
        compiler_params=_cparams("parallel"),
        name="diff_prep",
    )(p, p, p, cos_t, sin_t)


def _fox_prep_kernel(z_ref, k_ref, v_ref, b_ref, tri_ref, sel_ref, ko_ref, vt_ref, carry):
    @pl.when(pl.program_id(1) == 0)
    def _():
        carry[...] = jnp.zeros_like(carry)

    z = z_ref[...].astype(F32) + b_ref[...]
    log_f = jnp.minimum(z, 0.0) - jnp.log(1.0 + jnp.exp(-jnp.abs(z)))
    cum = jnp.dot(tri_ref[...], log_f, precision=lax.Precision.HIGHEST,
                  preferred_element_type=F32) + carry[...]
    ts = cum.shape[0]
    carry[...] = cum[ts - 1:ts, :]
    neg = cum * (-LOG2E)
    hi = neg.astype(BF16)
    r1 = neg - hi.astype(F32)
    mid = r1.astype(BF16)
    lo = (r1 - mid.astype(F32)).astype(BF16)
    parts = jnp.concatenate([hi, mid, lo], axis=1)
    for p in range(PAIRS):
        ko_ref[:, 2 * p * LANES:(2 * p + 1) * LANES] = k_ref[:, p * LANES:(p + 1) * LANES]
        ext = jnp.dot(parts, sel_ref[p], preferred_element_type=F32)
        ko_ref[:, (2 * p + 1) * LANES:(2 * p + 2) * LANES] = ext.astype(BF16)
    _store_vt(vt_ref, v_ref[...].astype(F32))


def _fox_decay_select():
    sel = np.zeros((PAIRS, 3 * LANES, LANES), np.float32)
    for p in range(PAIRS):
        for t in range(2):
            for j in range(3):
                sel[p, j * LANES + 2 * p + t, 3 * t + j] = 1.0
    return jnp.asarray(sel, BF16)


def _fox_prep(p, f_bias, batch, seq):
    n = p.shape[0]
    ts = min(TS_CUM, seq)
    per_b = seq // ts
    w = FOX_HEADS * FOX_HD
    bias = jnp.zeros((1, LANES), F32).at[0, :FOX_HEADS].set(f_bias)
    tri = jnp.tril(jnp.ones((ts, ts), F32))
    row = lambda b, j: b * per_b + j
    return pl.pallas_call(
        _fox_prep_kernel,
        grid=(batch, per_b),
        in_specs=[pl.BlockSpec((ts, LANES), lambda b, j: (row(b, j), COL_ZF // LANES)),
                  pl.BlockSpec((ts, w), lambda b, j: (row(b, j), COL_FOXK // w)),
                  pl.BlockSpec((ts, w), lambda b, j: (row(b, j), COL_FOXV // w)),
                  pl.BlockSpec((1, LANES), lambda b, j: (0, 0)),
                  pl.BlockSpec((ts, ts), lambda b, j: (0, 0)),
                  pl.BlockSpec((PAIRS, 3 * LANES, LANES), lambda b, j: (0, 0, 0))],
        out_specs=[pl.BlockSpec((ts, 2 * w), lambda b, j: (row(b, j), 0)),
                   pl.BlockSpec((1, PAIRS, VT_ROWS, ts), lambda b, j: (b, 0, 0, j))],
        out_shape=[jax.ShapeDtypeStruct((n, 2 * w), BF16),
                   jax.ShapeDtypeStruct((batch, PAIRS, VT_ROWS, seq), BF16)],
        scratch_shapes=[pltpu.VMEM((1, LANES), F32)],
        compiler_params=_cparams("parallel", "arbitrary"),
        name="fox_prep",
    )(p, p, p, bias, tri, _fox_decay_select())


def _attn_kernel(*refs, mode, tq, tk, per_frame, lam_init):
    if mode == "diff":
        q_ref, k_ref, vt_ref, lam_ref, subln_ref, o_ref, q_scr, s_scr, m_scr, acc_scr = refs
    else:
        q_ref, k_ref, vt_ref, o_ref, q_scr, s_scr, m_scr, acc_scr = refs
    w = 2 * tq
    i = pl.program_id(2)
    q = q_ref[...]
    zero = jnp.zeros((tq, LANES), BF16)
    lane = lax.broadcasted_iota(I32, (tq, LANES), 1)
    if mode == "mla":
        q_scr[0:tq, :] = jnp.concatenate([q[:, :LANES], zero], axis=1)
        q_scr[tq:w, :] = jnp.concatenate([zero, q[:, LANES:]], axis=1)
    elif mode == "fox":
        ind_a = jnp.where(lane < 3, 1.0, 0.0).astype(BF16)
        ind_b = jnp.where(lane < 3, 0.0, jnp.where(lane < 6, 1.0, 0.0)).astype(BF16)
        q_scr[0:tq, :] = jnp.concatenate([jnp.where(lane < FOX_HD, q, zero), ind_a], axis=1)
        q_scr[tq:w, :] = jnp.concatenate([jnp.where(lane < FOX_HD, zero, q), ind_b], axis=1)
    else:
        q_scr[0:tq, :] = jnp.where(lane < DIFF_HD, q, zero)
        q_scr[tq:w, :] = jnp.where(lane < DIFF_HD, zero, q)
    m_scr[...] = jnp.full(m_scr.shape, NEG_BIG, F32)
    acc_scr[...] = jnp.zeros(acc_scr.shape, F32)

    def logits(j, slot):
        ks = pl.multiple_of(j * tk, tk)
        s_scr[slot] = lax.dot_general(k_ref[pl.ds(ks, tk), :], q_scr[...], (((1,), (1,)), ((), ())),
                                      preferred_element_type=F32)

    def consume(j, slot, masked):
        ks = pl.multiple_of(j * tk, tk)
        s = s_scr[slot]
        if masked:
            k_pos = ks + lax.broadcasted_iota(I32, (tk, w), 0)
            q_pos = i * tq + (lax.broadcasted_iota(I32, (tk, w), 1) & (tq - 1))
            if per_frame:
                visible = k_pos <= q_pos
            else:
                visible = (k_pos >> CHUNK_SHIFT) <= (q_pos >> CHUNK_SHIFT)
            s = jnp.where(visible, s, NEG_BIG)
        m_prev = m_scr[...]
        m_new = jnp.maximum(m_prev, jnp.max(s, axis=0, keepdims=True))
        alpha = jnp.exp2(m_prev - m_new)
        p = jnp.exp2(s - m_new).astype(BF16)
        acc_scr[...] = alpha * acc_scr[...] + jnp.dot(vt_ref[:, pl.ds(ks, tk)], p,
                                                      preferred_element_type=F32)
        m_scr[...] = m_new

    n_un = (i * tq) // tk
    logits(0, 0)

    def run_tiles(j0, count):
        for t in range(count):
            logits(j0 + t + 1, (t + 1) % 2)
            consume(j0 + t, t % 2, False)

    def quad_body(jq, carry):
        run_tiles(ATT_UNROLL * jq, ATT_UNROLL)
        return carry

    n_quads = n_un // ATT_UNROLL
    lax.fori_loop(0, n_quads, quad_body, 0)
    base = ATT_UNROLL * n_quads
    if ATT_UNROLL == 4:
        @pl.when(n_un - base >= 2)
        def _():
            run_tiles(base, 2)

    @pl.when(n_un % 2 == 1)
    def _():
        logits(n_un, 1)
        consume(n_un - 1, 0, False)
        consume(n_un, 1, True)

    @pl.when(n_un % 2 == 0)
    def _():
        consume(n_un, 0, True)

    oa = acc_scr[0:LANES, 0:tq] / acc_scr[LANES:LANES + 1, 0:tq]
    ob = acc_scr[0:LANES, tq:w] / acc_scr[LANES:LANES + 1, tq:w]
    if mode == "diff":
        lv = lam_ref[...]
        lam = (jnp.exp(jnp.sum(lv[0:1] * lv[1:2], axis=1, keepdims=True))
               - jnp.exp(jnp.sum(lv[2:3] * lv[3:4], axis=1, keepdims=True)) + lam_init)
        o = oa - lam * ob
        o = o * lax.rsqrt(jnp.mean(o * o, axis=0, keepdims=True) + RMS_EPS)
        o = o.T * subln_ref[...] * (1.0 - lam_init)
    else:
        row = lax.broadcasted_iota(I32, (LANES, tq), 0)
        o = jnp.where(row < FOX_HD, oa, ob).T
    o_ref[...] = o.astype(BF16)


def _attention(mode, batch, seq, *, q, k, vt, q_col=0, extra=(), lam_init=0.0):
    n = batch * seq
    tq = min(TQ_ATT, seq // 2)
    tk = min(TK_ATT, seq)
    nq = seq // tq
    kw = LANES if mode == "diff" else 2 * LANES
    qw = 2 * LANES if mode == "mla" else LANES
    qb = q_col // qw
    in_specs = [pl.BlockSpec((tq, qw), lambda b, p, i: (b * nq + i, qb + p)),
                pl.BlockSpec((seq, kw), lambda b, p, i: (b, p)),
                pl.BlockSpec((1, 1, VT_ROWS, seq), lambda b, p, i: (b, p, 0, 0))]
    args = (q, k, vt)
    if mode == "diff":
        in_specs += [pl.BlockSpec((4, DIFF_HD), lambda b, p, i: (0, 0)),
                     pl.BlockSpec((1, LANES), lambda b, p, i: (0, 0))]
        args += tuple(extra)

    def body(*refs):
        refs = list(refs)
        refs[2] = refs[2].at[0, 0]
        _attn_kernel(*refs, mode=mode, tq=tq, tk=tk, per_frame=(mode == "fox"), lam_init=lam_init)

    return pl.pallas_call(
        body,
        grid=(batch, PAIRS, nq),
        in_specs=in_specs,
        out_specs=pl.BlockSpec((tq, LANES), lambda b, p, i: (b * nq + i, p)),
        out_shape=jax.ShapeDtypeStruct((n, PAIRS * LANES), BF16),
        scratch_shapes=[pltpu.VMEM((2 * tq, kw), BF16),
                        pltpu.VMEM((2, tk, 2 * tq), F32),
                        pltpu.VMEM((1, 2 * tq), F32),
                        pltpu.VMEM((VT_ROWS, 2 * tq), F32)],
        compiler_params=_cparams("parallel", "parallel", "arbitrary"),
        name="attn_" + mode,
    )(*args)


def _conv_kernel(b_ref, c_ref, u_ref, ch_ref, uh_ref, w_ref, o_ref, *, per_b):
    u = c_ref[...].astype(F32) * u_ref[...].astype(F32)
    halo = ch_ref[...].astype(F32) * uh_ref[...].astype(F32)
    halo = jnp.where(pl.program_id(0) % per_b == 0, 0.0, halo)
    hr = halo.shape[0]
    row = lax.broadcasted_iota(I32, u.shape, 0)
    u1 = jnp.where(row == 0, halo[hr - 1:hr, :], pltpu.roll(u, 1, 0))
    u2 = jnp.where(row == 0, halo[hr - 2:hr - 1, :],
                   jnp.where(row == 1, halo[hr - 1:hr, :], pltpu.roll(u, 2, 0)))
    w = w_ref[...]
    y = u2 * w[0:1, :] + u1 * w[1:2, :] + u * w[2:3, :]
    o_ref[...] = (b_ref[...].astype(F32) * y).astype(BF16)


def _short_conv(p, conv_w, seq):
    n = p.shape[0]
    ts = min(TM_PREP, seq)
    per_b = seq // ts
    w = CONV_WIDTH
    hr = BF16_SUBLANES
    halo_map = lambda col: (lambda i: (jnp.maximum(i * (ts // hr) - 1, 0), col // w))
    return pl.pallas_call(
        functools.partial(_conv_kernel, per_b=per_b),
        grid=(n // ts,),
        in_specs=[pl.BlockSpec((ts, w), lambda i: (i, COL_CB // w)),
                  pl.BlockSpec((ts, w), lambda i: (i, COL_CC // w)),
                  pl.BlockSpec((ts, w), lambda i: (i, COL_CU // w)),
                  pl.BlockSpec((hr, w), halo_map(COL_CC)),
                  pl.BlockSpec((hr, w), halo_map(COL_CU)),
                  pl.BlockSpec((CONV_TAPS, w), lambda i: (0, 0))],
        out_specs=pl.BlockSpec((ts, w), lambda i: (i, 0)),
        out_shape=jax.ShapeDtypeStruct((n, w), BF16),
        compiler_params=_cparams("parallel"),
        name="short_conv",
    )(p, p, p, p, p, conv_w)


def _merge_kernel(o0, o1, o2, o3, g0, g1, g2, g3, wb_ref, wo_ref, x_ref, gm_ref, out_ref):
    merged = None
    for br, (o_ref, g_ref) in enumerate(((o0, g0), (o1, g1), (o2, g2), (o3, g3))):
        t = jnp.dot(o_ref[...], wb_ref[br], preferred_element_type=F32) * g_ref[...].astype(F32)
        merged = t if merged is None else merged + t
    y = jnp.dot(merged.astype(BF16), wo_ref[...], preferred_element_type=F32)
    out_ref[...] = x_ref[...] + gm_ref[0] * y


def _merge(branches, gates, w_branch, w_out, x, g_m, seq):
    n, d = x.shape
    tm = min(TM_MERGE, seq)
    per_b = seq // tm
    bw = BRANCH_WIDTH
    o_spec = pl.BlockSpec((tm, bw), lambda i: (i, 0))
    g_specs = [pl.BlockSpec((tm, d), lambda i, br=br: (i, br)) for br in range(4)]
    single = pl.Buffered(1)
    return pl.pallas_call(
        _merge_kernel,
        grid=(n // tm,),
        in_specs=[o_spec] * 4 + g_specs + [
            pl.BlockSpec((4, bw, d), lambda i: (0, 0, 0), pipeline_mode=single),
            pl.BlockSpec((d, d), lambda i: (0, 0), pipeline_mode=single),
            pl.BlockSpec((tm, d), lambda i: (i, 0)),
            pl.BlockSpec((1, 1, d), lambda i: (i // per_b, 0, 0))],
        out_specs=pl.BlockSpec((tm, d), lambda i: (i, 0)),
        out_shape=jax.ShapeDtypeStruct((n, d), F32),
        compiler_params=_cparams("parallel"),
        name="branch_merge",
    )(*branches, gates, gates, gates, gates, w_branch, w_out, x, g_m)


def _router_kernel(x_ref, g_ref, sc_ref, sh_ref, wr_ref, rb_ref, tri_ref,
                   hb_ref, hs_ref, idx_ref, wts_ref, rank_ref, cnt_ref, carry):
    @pl.when(pl.program_id(0) == 0)
    def _():
        carry[...] = jnp.zeros_like(carry)

    h = _normmod(x_ref[...], g_ref[...], sc_ref[0], sh_ref[0])
    hb_ref[...] = h.astype(BF16)
    slab = h.shape[1] // LANES
    for s in range(slab):
        hs_ref[pl.ds(s, h.shape[0], stride=slab), :] = h[:, s * LANES:(s + 1) * LANES]
    logits = jnp.dot(h, wr_ref[...], precision=lax.Precision.HIGHEST, preferred_element_type=F32)
    scores = jax.nn.sigmoid(logits)
    work = scores + rb_ref[...]
    tm, ne = scores.shape
    lane = lax.broadcasted_iota(I32, (tm, ne), 1).astype(F32)
    slot = lax.broadcasted_iota(I32, (tm, TOP_K), 1)
    idx = jnp.zeros((tm, TOP_K), F32)
    sel = jnp.zeros((tm, TOP_K), F32)
    hits = []
    chosen = jnp.zeros((tm, ne), F32)
    for kk in range(TOP_K):
        mx = jnp.max(work, axis=1, keepdims=True)
        first = jnp.min(jnp.where(work == mx, lane, float(ne)), axis=1, keepdims=True)
        hit = lane == first
        hits.append(hit)
        chosen = jnp.where(hit, 1.0, chosen)
        sc_k = jnp.sum(jnp.where(hit, scores, 0.0), axis=1, keepdims=True)
        work = jnp.where(hit, -jnp.inf, work)
        idx = jnp.where(slot == kk, first, idx)
        sel = jnp.where(slot == kk, sc_k, sel)
    idx_ref[...] = idx.astype(I32)
    wts_ref[...] = sel / jnp.sum(sel, axis=1, keepdims=True) * ROUTED_SCALE
    before = jnp.dot(tri_ref[...], chosen.astype(BF16), preferred_element_type=F32) + carry[...]
    rank = jnp.zeros((tm, TOP_K), F32)
    for kk in range(TOP_K):
        r_k = jnp.sum(jnp.where(hits[kk], before, 0.0), axis=1, keepdims=True)
        rank = jnp.where(slot == kk, r_k, rank)
    rank_ref[...] = rank.astype(I32)
    total = carry[...] + jnp.sum(chosen, axis=0, keepdims=True)
    carry[...] = total
    cnt_ref[...] = total.astype(I32)


def _router(x, g, sc, sh, w_router, router_bias, seq):
    n, d = x.shape
    tm = min(TM_ROUTE, seq)
    per_b = seq // tm
    ne = w_router.shape[1]
    tri = jnp.tril(jnp.ones((tm, tm), F32), k=-1).astype(BF16)
    return pl.pallas_call(
        _router_kernel,
        grid=(n // tm,),
        in_specs=[pl.BlockSpec((tm, d), lambda i: (i, 0)),
                  pl.BlockSpec((1, d), lambda i: (0, 0)),
                  pl.BlockSpec((1, 1, d), lambda i: (i // per_b, 0, 0)),
                  pl.BlockSpec((1, 1, d), lambda i: (i // per_b, 0, 0)),
                  pl.BlockSpec((d, ne), lambda i: (0, 0)),
                  pl.BlockSpec((1, ne), lambda i: (0, 0)),
                  pl.BlockSpec((tm, tm), lambda i: (0, 0))],
        out_specs=[pl.BlockSpec((tm, d), lambda i: (i, 0)),
                   pl.BlockSpec((tm * (d // LANES), LANES), lambda i: (i, 0)),
                   pl.BlockSpec((tm, TOP_K), lambda i: (i, 0)),
                   pl.BlockSpec((tm, TOP_K), lambda i: (i, 0)),
                   pl.BlockSpec((tm, TOP_K), lambda i: (i, 0)),
                   pl.BlockSpec((1, ne), lambda i: (0, 0))],
        out_shape=[jax.ShapeDtypeStruct((n, d), BF16),
                   jax.ShapeDtypeStruct((n * (d // LANES), LANES), F32),
                   jax.ShapeDtypeStruct((n, TOP_K), I32),
                   jax.ShapeDtypeStruct((n, TOP_K), F32),
                   jax.ShapeDtypeStruct((n, TOP_K), I32),
                   jax.ShapeDtypeStruct((1, ne), I32)],
        scratch_shapes=[pltpu.VMEM((1, ne), F32)],
        compiler_params=_cparams("arbitrary"),
        name="moe_router",
    )(x, g, sc, sh, w_router, router_bias, tri)


GATHER_BUFS = 3


def _gather_step(i, slot, n_steps, idx_hbm, idx_smem, idx_sem, data_hbm, buf, data_sem, n_rows, slab):
    assert n_steps >= GATHER_BUFS

    def idx_copy(step, slot):
        return pltpu.make_async_copy(idx_hbm.at[step], idx_smem.at[pl.ds(slot, 1)], idx_sem.at[slot])

    def rows_wait(slot):
        pltpu.make_async_copy(data_hbm.at[pl.ds(0, n_rows * slab)], buf.at[slot], data_sem.at[slot]).wait()

    def issue_rows(slot):
        for r in range(n_rows):
            row = pl.multiple_of(idx_smem[slot, r], slab)
            pltpu.make_async_copy(data_hbm.at[pl.ds(row, slab)], buf.at[slot, pl.ds(r * slab, slab)],
                                  data_sem.at[slot]).start(priority=r % 2)

    ahead = (slot + 2) % GATHER_BUFS
    last = n_steps - 1

    if slot == 0:
        @pl.when(i == 0)
        def _():
            for t in range(2):
                idx_copy(t, t).start()
                idx_copy(t, t).wait()
                issue_rows(t)
            idx_copy(2, 2).start()

    rows_wait(slot)
    idx_copy(jnp.minimum(i + 2, last), ahead).wait()

    def prefetch():
        issue_rows(ahead)
        idx_copy(jnp.minimum(i + 3, last), slot).start()

    def drain():
        @pl.when(i == last)
        def _():
            rows_wait((slot + 1) % GATHER_BUFS)
            rows_wait(ahead)
            idx_copy(last, slot).wait()

    return prefetch, drain


def _for_each_parity(i, step):
    for slot in range(GATHER_BUFS):
        @pl.when(i % GATHER_BUFS == slot)
        def _():
            step(slot)


def _expert_kernel(te_ref, src_hbm, h_hbm, wg_ref, wu_ref, wd_ref, y_ref,
                   idx_smem, idx_sem, xbuf, x_sem, wg_s, wu_s, wd_s, xs_s, *, n_steps, tm, slab):
    i = pl.program_id(0)
    new_expert = jnp.logical_or(i == 0, te_ref[i] != te_ref[jnp.maximum(i - 1, 0)])

    @pl.when(new_expert)
    def _():
        wg_s[...] = wg_ref[0, 0].astype(BF16)
        wu_s[...] = wu_ref[0, 0].astype(BF16)
        wd_s[...] = wd_ref[0, 0].astype(BF16)

    def step(slot):
        prefetch, drain = _gather_step(i, slot, n_steps, src_hbm, idx_smem, idx_sem, h_hbm, xbuf, x_sem,
                                       tm, slab)
        xcur = xbuf.at[slot]
        for s in range(slab):
            xs_s[:, s * LANES:(s + 1) * LANES] = xcur[pl.ds(s, tm, stride=slab), :].astype(BF16)
        prefetch()
        x = xs_s[...]
        a = jnp.dot(x, wg_s[...], preferred_element_type=F32)
        u = jnp.dot(x, wu_s[...], preferred_element_type=F32)
        hid = (a * jax.nn.sigmoid(a)) * u
        y = jnp.dot(hid.astype(BF16), wd_s[...], preferred_element_type=F32)
        for s in range(slab):
            y_ref[pl.ds(s, tm, stride=slab), :] = y[:, s * LANES:(s + 1) * LANES]
        drain()

    _for_each_parity(i, step)


def _expert_ffn(layer, tile_expert, src, h_slab, w_eg, w_eu, w_ed):
    n_steps, _, tm = src.shape
    d = w_eg.shape[2]
    ff = w_eg.shape[3]
    slab = d // LANES
    grid_spec = pltpu.PrefetchScalarGridSpec(
        num_scalar_prefetch=1,
        grid=(n_steps,),
        in_specs=[pl.BlockSpec(memory_space=pl.ANY),
                  pl.BlockSpec(memory_space=pl.ANY),
                  pl.BlockSpec((1, 1, d, ff), lambda i, te: (layer, te[i], 0, 0)),
                  pl.BlockSpec((1, 1, d, ff), lambda i, te: (layer, te[i], 0, 0)),
                  pl.BlockSpec((1, 1, ff, d), lambda i, te: (layer, te[i], 0, 0))],
        out_specs=pl.BlockSpec((tm * slab, LANES), lambda i, te: (i, 0)),
        scratch_shapes=[pltpu.SMEM((GATHER_BUFS, tm), I32),
                        pltpu.SemaphoreType.DMA((GATHER_BUFS,)),
                        pltpu.VMEM((GATHER_BUFS, tm * slab, LANES), F32),
                        pltpu.SemaphoreType.DMA((GATHER_BUFS,)),
                        pltpu.VMEM((d, ff), BF16),
                        pltpu.VMEM((d, ff), BF16),
                        pltpu.VMEM((ff, d), BF16),
                        pltpu.VMEM((tm, d), BF16)],
    )
    return pl.pallas_call(
        functools.partial(_expert_kernel, n_steps=n_steps, tm=tm, slab=slab),
        grid_spec=grid_spec,
        out_shape=jax.ShapeDtypeStruct((n_steps * tm * slab, LANES), F32),
        compiler_params=_cparams("arbitrary"),
        name="moe_experts",
    )(tile_expert, src, h_slab, w_eg, w_eu, w_ed)


def _shared_kernel(h_ref, wg_ref, wu_ref, wd_ref, o_ref):
    h = h_ref[...].astype(BF16)
    a = jnp.dot(h, wg_ref[...], preferred_element_type=F32)
    u = jnp.dot(h, wu_ref[...], preferred_element_type=F32)
    hid = ((a * jax.nn.sigmoid(a)) * u).astype(BF16)
    o_ref[...] = jnp.dot(hid, wd_ref[...], preferred_element_type=F32)


def _shared_ffn(h, wg, wu, wd):
    n, d = h.shape
    ff = wg.shape[1]
    tm = min(TM_FFN, n)
    return pl.pallas_call(
        _shared_kernel,
        grid=(n // tm,),
        in_specs=[pl.BlockSpec((tm, d), lambda i: (i, 0)),
                  pl.BlockSpec((d, ff), lambda i: (0, 0)),
                  pl.BlockSpec((d, ff), lambda i: (0, 0)),
                  pl.BlockSpec((ff, d), lambda i: (0, 0))],
        out_specs=pl.BlockSpec((tm, d), lambda i: (i, 0)),
        out_shape=jax.ShapeDtypeStruct((n, d), F32),
        compiler_params=_cparams("parallel"),
        name="moe_shared",
    )(h, wg, wu, wd)


def _combine_kernel(dest_hbm, y_hbm, wts_ref, sh_ref, x_ref, gf_ref, fn_ref, o_ref,
                    idx_smem, idx_sem, ybuf, y_sem, *, n_steps, tc, slab, final_norm):
    i = pl.program_id(0)
    rows = 8
    d = slab * LANES

    def step(slot):
        prefetch, drain = _gather_step(i, slot, n_steps, dest_hbm, idx_smem, idx_sem, y_hbm, ybuf, y_sem,
                                       TOP_K * tc, slab)
        prefetch()
        ycur = ybuf.at[slot]
        for c in range(tc // rows):
            r0 = c * rows
            wts = wts_ref[r0:r0 + rows, :]
            wb = [jnp.broadcast_to(wts[:, kk:kk + 1], (rows, LANES)) for kk in range(TOP_K)]
            outs = []
            for s in range(slab):
                sl = slice(s * LANES, (s + 1) * LANES)
                acc = sh_ref[r0:r0 + rows, sl]
                for kk in range(TOP_K):
                    acc = acc + wb[kk] * ycur[pl.ds((kk * tc + r0) * slab + s, rows, stride=slab), :]
                outs.append(x_ref[r0:r0 + rows, sl] + gf_ref[0, :, sl] * acc)
            if final_norm:
                ssq = outs[0] * outs[0]
                for o in outs[1:]:
                    ssq = ssq + o * o
                inv = lax.rsqrt(jnp.sum(ssq, axis=1, keepdims=True) * (1.0 / d) + RMS_EPS)
                outs = [o * inv * fn_ref[:, s * LANES:(s + 1) * LANES] for s, o in enumerate(outs)]
            for s, o in enumerate(outs):
                o_ref[r0:r0 + rows, s * LANES:(s + 1) * LANES] = o
        drain()

    _for_each_parity(i, step)


def _combine(dest, y, wts, shared, x, g_f, final_gain, seq, *, final_norm):
    n, d = x.shape
    n_steps, _, rows = dest.shape
    tc = rows // TOP_K
    per_b = seq // tc
    slab = d // LANES
    return pl.pallas_call(
        functools.partial(_combine_kernel, n_steps=n_steps, tc=tc, slab=slab, final_norm=final_norm),
        grid=(n_steps,),
        in_specs=[pl.BlockSpec(memory_space=pl.ANY),
                  pl.BlockSpec(memory_space=pl.ANY),
                  pl.BlockSpec((tc, TOP_K), lambda i: (i, 0)),
                  pl.BlockSpec((tc, d), lambda i: (i, 0)),
                  pl.BlockSpec((tc, d), lambda i: (i, 0)),
                  pl.BlockSpec((1, 1, d), lambda i: (i // per_b, 0, 0)),
                  pl.BlockSpec((1, d), lambda i: (0, 0))],
        out_specs=pl.BlockSpec((tc, d), lambda i: (i, 0)),
        out_shape=jax.ShapeDtypeStruct((n, d), F32),
        scratch_shapes=[pltpu.SMEM((GATHER_BUFS, rows), I32),
                        pltpu.SemaphoreType.DMA((GATHER_BUFS,)),
                        pltpu.VMEM((GATHER_BUFS, rows * slab, LANES), F32),
                        pltpu.SemaphoreType.DMA((GATHER_BUFS,))],
        compiler_params=_cparams("arbitrary"),
        name="moe_combine",
    )(dest, y, wts, shared, x, g_f, final_gain)


def _rope_tables(positions):
    pos = positions.reshape(-1).astype(F32)
    n = pos.shape[0]

    def cos_sin(half):
        inv_freq = ROPE_THETA ** (-jnp.arange(half, dtype=F32) / half)
        ang = pos[:, None] * inv_freq
        return jnp.cos(ang), jnp.sin(ang)

    cm, sm = cos_sin(MLA_ROPE // 2)
    tail = LANES - MLA_NOPE - MLA_ROPE
    mla_cos = jnp.concatenate([jnp.ones((n, MLA_NOPE), F32), cm, cm, jnp.ones((n, tail), F32)], axis=1)
    mla_sin = jnp.concatenate([jnp.zeros((n, MLA_NOPE), F32), sm, sm, jnp.zeros((n, tail), F32)], axis=1)
    cd, sd = cos_sin(DIFF_ROT // 2)
    rest = DIFF_HD - DIFF_ROT
    dc = jnp.concatenate([cd, cd, jnp.ones((n, rest), F32)], axis=1)
    ds = jnp.concatenate([sd, sd, jnp.zeros((n, rest), F32)], axis=1)
    return mla_cos, mla_sin, jnp.concatenate([dc, dc], axis=1), jnp.concatenate([ds, ds], axis=1)


def _in_proj_weight(w_in):
    d = w_in.shape[0]
    sizes = (MLA_Q_RANK, MLA_KV_RANK, MLA_ROPE, 512, 512, 512, FOX_HEADS, 512, 512, 512, 512, 512, 512)
    parts, off = [], 0
    for s in sizes:
        parts.append(w_in[:, off:off + s])
        off += s
    a_cq, a_ckv, a_kr, b_q, b_k, b_v, b_zf, c_b, c_c, c_u, d_q, d_k, d_v = parts
    zeros = lambda w: jnp.zeros((d, w), w_in.dtype)
    cols = [b_q, b_k, b_v, c_b, c_c, c_u, d_q, d_k, d_v, a_cq,
            zeros(MLA_NOPE), a_kr, zeros(LANES - MLA_NOPE - MLA_ROPE),
            a_ckv, b_zf, zeros(LANES - FOX_HEADS), zeros(P_WIDTH - COL_ZF - LANES)]
    return jnp.concatenate(cols, axis=1).astype(BF16)


def _in_proj_scale():
    s = np.ones((1, P_WIDTH), np.float32)
    qs = FOX_HD ** -0.5 * LOG2E
    s[0, COL_FOXQ:COL_FOXQ + FOX_HEADS * FOX_HD] = qs
    s[0, COL_DQ:COL_DQ + DIFF_HEADS * 2 * DIFF_HD] = qs
    return jnp.asarray(s)


def _dispatch_plan(idx, rank, counts, tm):
    n = idx.shape[0]
    counts = counts.reshape(-1)
    padded = (counts + tm - 1) // tm * tm
    ends = jnp.cumsum(padded)
    starts = ends - padded
    dest = (starts[idx] + rank).astype(I32)
    n_steps = (n * TOP_K) // tm + N_EXPERTS
    tok = jnp.repeat(jnp.arange(n, dtype=I32), TOP_K)
    src = jnp.zeros((n_steps * tm,), I32).at[dest.reshape(-1)].set(
        tok, unique_indices=True, indices_are_sorted=False, mode="promise_in_bounds")
    tile_start = jnp.arange(n_steps, dtype=I32) * tm
    tile_expert = jnp.sum((ends[None, :] <= tile_start[:, None]).astype(I32), axis=1)
    tile_expert = jnp.minimum(tile_expert, N_EXPERTS - 1)
    return dest, src.reshape(n_steps, 1, tm), tile_expert


def kernel(x, c, positions, ada_w, ada_b, norm_mix, norm_ffn, w_in, mla_q_norm, mla_w_uq, mla_kv_norm,
           mla_w_ukv, fox_f_bias, conv_w, diff_lam_q1, diff_lam_k1, diff_lam_q2, diff_lam_k2, diff_subln,
           w_branch, w_gate, b_gate, w_out, w_router, router_bias, w_exp_gate, w_exp_up, w_exp_down,
           w_sh_gate, w_sh_up, w_sh_down, final_norm):
    batch, seq, d = x.shape
    n = batch * seq
    depth = ada_w.shape[0]
    xf = x.reshape(n, d)

    mod = _modulation(c, ada_w, ada_b)
    mla_cos, mla_sin, diff_cos, diff_sin = _rope_tables(positions)
    p_scale = _in_proj_scale()

    for l in range(depth):
        lam_init = 0.8 - 0.6 * math.exp(-0.3 * l)
        mods = mod[l, :batch].reshape(batch, 6, 1, d)
        sh_m, sc_m, g_m, sh_f, sc_f, g_f = (mods[:, t] for t in range(6))

        p = _norm_proj(xf, norm_mix[l][None], sc_m, sh_m, _in_proj_weight(w_in[l]), p_scale, seq,
                       sigmoid=False, name="in_proj")
        w_g = w_gate[l].transpose(1, 0, 2).reshape(d, 4 * d).astype(BF16)
        gates = _norm_proj(xf, norm_mix[l][None], sc_m, sh_m, w_g, b_gate[l].reshape(1, 4 * d), seq,
                           sigmoid=True, name="branch_gates")

        wq = jnp.pad(mla_w_uq[l].reshape(MLA_Q_RANK, MLA_HEADS, MLA_NOPE + MLA_ROPE),
                     ((0, 0), (0, 0), (0, LANES - MLA_NOPE - MLA_ROPE)))
        wq = wq.reshape(MLA_Q_RANK, MLA_HEADS * LANES).astype(BF16)
        wkv = mla_w_ukv[l].reshape(MLA_KV_RANK, MLA_HEADS, MLA_NOPE + MLA_V)
        wk = jnp.pad(wkv[:, :, :MLA_NOPE], ((0, 0), (0, 0), (0, LANES - MLA_NOPE)))
        wk = wk.reshape(MLA_KV_RANK, MLA_HEADS * LANES).astype(BF16)
        wv = wkv[:, :, MLA_NOPE:].reshape(MLA_KV_RANK, MLA_HEADS * MLA_V).astype(BF16)
        q_a, k_a, vt_a = _mla_prep(p, mla_cos, mla_sin, mla_q_norm[l][None], mla_kv_norm[l][None],
                                   wq, wk, wv, batch, seq)
        o_mla = _attention("mla", batch, seq, q=q_a, k=k_a, vt=vt_a)

        k_f, vt_f = _fox_prep(p, fox_f_bias[l], batch, seq)
        o_fox = _attention("fox", batch, seq, q=p, k=k_f, vt=vt_f, q_col=COL_FOXQ)

        o_conv = _short_conv(p, conv_w[l], seq)

        q_d, k_d, vt_d = _diff_prep(p, diff_cos, diff_sin, batch, seq)
        lam_vecs = jnp.stack([diff_lam_q1[l], diff_lam_k1[l], diff_lam_q2[l], diff_lam_k2[l]])
        o_diff = _attention("diff", batch, seq, q=q_d, k=k_d, vt=vt_d,
                            extra=(lam_vecs, diff_subln[l][None]), lam_init=lam_init)

        xf = _merge((o_mla, o_fox, o_conv, o_diff), gates, w_branch[l].astype(BF16),
                    w_out[l].astype(BF16), xf, g_m, seq)

        h, h_slab, idx, wts, rank, counts = _router(xf, norm_ffn[l][None], sc_f, sh_f, w_router[l],
                                                    router_bias[l][None], seq)
        dest, src, tile_expert = _dispatch_plan(idx, rank, counts, min(TM_EXP, n))
        slab = d // LANES
        y = _expert_ffn(l, tile_expert, src * slab, h_slab, w_exp_gate, w_exp_up, w_exp_down)
        shared = _shared_ffn(h, w_sh_gate[l].astype(BF16), w_sh_up[l].astype(BF16),
                             w_sh_down[l].astype(BF16))
        tc = min(TC_COMB, seq)
        dest_t = dest.reshape(n // tc, tc, TOP_K).transpose(0, 2, 1).reshape(n // tc, 1, TOP_K * tc)
        xf = _combine(dest_t * slab, y, wts, shared, xf, g_f, final_norm[None], seq,
                      final_norm=(l == depth - 1))

    return xf.reshape(batch, seq, d)
```

```python
import functools
import math

import numpy as np
import jax
import jax.numpy as jnp
from jax import lax
from jax.experimental import pallas as pl
from jax.experimental.pallas import tpu as pltpu

F32 = jnp.float32
BF16 = jnp.bfloat16
I32 = jnp.int32

CHUNK = 64
CHUNK_SHIFT = 6
RMS_EPS = 1e-6
ROPE_THETA = 500000.0
MLA_HEADS, MLA_Q_RANK, MLA_KV_RANK, MLA_NOPE, MLA_ROPE, MLA_V = 8, 384, 256, 64, 32, 64
FOX_HEADS, FOX_HD = 8, 64
CONV_WIDTH, CONV_TAPS = 512, 3
DIFF_HEADS, DIFF_HD, DIFF_ROT = 4, 64, 16
N_EXPERTS, TOP_K, ROUTED_SCALE = 64, 8, 2.5
BRANCH_WIDTH = 512
LOG2E = 1.4426950408889634

LANES = 128
BF16_SUBLANES = 16
V7X_VMEM_BYTES = 64 * 1024 * 1024
VMEM_LIMIT = V7X_VMEM_BYTES - 8 * 1024 * 1024
NEG_BIG = -1e30

TM_PROJ = 1024
TM_PROJ_CHUNK = 256
TN_PROJ = 512
TM_PREP = 512
TS_CUM = 256
TQ_ATT = 512
TK_ATT = 512
ATT_UNROLL = 4
TM_MERGE = 256
TM_ROUTE = 256
TM_EXP = 256
TM_FFN = 512
TC_COMB = 64

PAIRS = 4
VT_ROWS = LANES + BF16_SUBLANES

COL_FOXQ, COL_FOXK, COL_FOXV = 0, 512, 1024
COL_CB, COL_CC, COL_CU = 1536, 2048, 2560
COL_DQ, COL_DK, COL_DV = 3072, 3584, 4096
COL_CQ = 4608
COL_KR = 4992
COL_CKV = 5120
COL_ZF = 5376
P_WIDTH = 5632


def _cparams(*sem):
    return pltpu.CompilerParams(dimension_semantics=sem, vmem_limit_bytes=VMEM_LIMIT)


def _normmod(x, g, sc, sh):
    ms = jnp.mean(x * x, axis=-1, keepdims=True)
    return (x * lax.rsqrt(ms + RMS_EPS)) * g * (1.0 + sc) + sh


def _rope_lanes(x, cos_t, sin_t, lo, half, period):
    width = x.shape[1]
    lane = lax.broadcasted_iota(I32, x.shape, 1) & (period - 1)
    x_up = pltpu.roll(x, width - half, 1)
    x_dn = pltpu.roll(x, half, 1)
    rot = jnp.where(lane < lo, 0.0,
                    jnp.where(lane < lo + half, -x_up,
                              jnp.where(lane < lo + 2 * half, x_dn, 0.0)))
    return x * cos_t + rot * sin_t


def _store_vt(vt_ref, v):
    tm = v.shape[0]
    ones = jnp.ones((VT_ROWS - LANES, tm), BF16)
    for p in range(PAIRS):
        vt_ref[0, p, 0:LANES, :] = v[:, p * LANES:(p + 1) * LANES].T.astype(BF16)
        vt_ref[0, p, LANES:VT_ROWS, :] = ones


def _vt_out(batch, seq, tm):
    per_b = seq // tm
    spec = pl.BlockSpec((1, PAIRS, VT_ROWS, tm), lambda i: (i // per_b, 0, 0, i % per_b))
    return spec, jax.ShapeDtypeStruct((batch, PAIRS, VT_ROWS, seq), BF16)


def _mod_kernel(c_ref, w_ref, b_ref, o_ref):
    c = c_ref[...]
    ca = (c * jax.nn.sigmoid(c)).astype(BF16)
    o_ref[0] = jnp.dot(ca, w_ref[0].astype(BF16), preferred_element_type=F32) + b_ref[0]


def _modulation(c, ada_w, ada_b):
    n_layers, d, d6 = ada_w.shape
    b = c.shape[0]
    rows = 8
    cp = jnp.zeros((rows, d), F32).at[:b].set(c)
    tn = 1024
    return pl.pallas_call(
        _mod_kernel,
        grid=(n_layers, d6 // tn),
        in_specs=[pl.BlockSpec((rows, d), lambda l, j: (0, 0)),
                  pl.BlockSpec((1, d, tn), lambda l, j: (l, 0, j)),
                  pl.BlockSpec((1, 1, tn), lambda l, j: (l, 0, j))],
        out_specs=pl.BlockSpec((1, rows, tn), lambda l, j: (l, 0, j)),
        out_shape=jax.ShapeDtypeStruct((n_layers, rows, d6), F32),
        compiler_params=_cparams("parallel", "parallel"),
        name="adaln_modulation",
    )(cp, ada_w, ada_b.reshape(n_layers, 1, d6))


def _proj_kernel(x_ref, g_ref, sc_ref, sh_ref, w_ref, e_ref, o_ref, h_scr, *, sigmoid):
    @pl.when(pl.program_id(1) == 0)
    def _():
        h_scr[...] = _normmod(x_ref[...], g_ref[...], sc_ref[0], sh_ref[0]).astype(BF16)

    mc = min(TM_PROJ_CHUNK, h_scr.shape[0])
    for cc in range(h_scr.shape[0] // mc):
        rows = slice(cc * mc, (cc + 1) * mc)
        acc = jnp.dot(h_scr[rows, :], w_ref[0], preferred_element_type=F32)
        if sigmoid:
            acc = jax.nn.sigmoid(acc + e_ref[...])
        else:
            acc = acc * e_ref[...]
        o_ref[rows, :] = acc.astype(o_ref.dtype)


def _norm_proj(x, g, sc, sh, w, epi, seq, *, sigmoid, name):
    n, d = x.shape
    cols = w.shape[1]
    tm, tn = min(TM_PROJ, seq), TN_PROJ
    per_b = seq // tm
    w = w.reshape(d, cols // tn, tn).transpose(1, 0, 2)
    return pl.pallas_call(
        functools.partial(_proj_kernel, sigmoid=sigmoid),
        grid=(n // tm, cols // tn),
        in_specs=[pl.BlockSpec((tm, d), lambda i, j: (i, 0)),
                  pl.BlockSpec((1, d), lambda i, j: (0, 0)),
                  pl.BlockSpec((1, 1, d), lambda i, j: (i // per_b, 0, 0)),
                  pl.BlockSpec((1, 1, d), lambda i, j: (i // per_b, 0, 0)),
                  pl.BlockSpec((1, d, tn), lambda i, j: (j, 0, 0)),
                  pl.BlockSpec((1, tn), lambda i, j: (0, j))],
        out_specs=pl.BlockSpec((tm, tn), lambda i, j: (i, j)),
        out_shape=jax.ShapeDtypeStruct((n, cols), BF16),
        scratch_shapes=[pltpu.VMEM((tm, d), BF16)],
        compiler_params=_cparams("parallel", "arbitrary"),
        name=name,
    )(x, g, sc, sh, w, epi)


def _mla_prep_kernel(cq_ref, ckv_ref, kr_ref, cos_ref, sin_ref, qn_ref, kvn_ref,
                     wq_ref, wk_ref, wv_ref, q_ref, k_ref, vt_ref, *, scale):
    cos_t, sin_t = cos_ref[...], sin_ref[...]

    def rms(t, gain):
        t = t.astype(F32)
        return (t * lax.rsqrt(jnp.mean(t * t, axis=-1, keepdims=True) + RMS_EPS) * gain).astype(BF16)

    def rope(t):
        return _rope_lanes(t, cos_t, sin_t, MLA_NOPE, MLA_ROPE // 2, LANES)

    qn = rms(cq_ref[...], qn_ref[...])
    kvn = rms(ckv_ref[...], kvn_ref[...])
    q = jnp.dot(qn, wq_ref[...], preferred_element_type=F32)
    k = jnp.dot(kvn, wk_ref[...], preferred_element_type=F32)
    _store_vt(vt_ref, jnp.dot(kvn, wv_ref[...], preferred_element_type=F32))
    k_pe = rope(kr_ref[...].astype(F32))
    for h in range(MLA_HEADS):
        sl = slice(h * LANES, (h + 1) * LANES)
        q_ref[:, sl] = (rope(q[:, sl]) * scale).astype(BF16)
        k_ref[:, sl] = (k[:, sl] + k_pe).astype(BF16)


def _mla_prep(p, cos_t, sin_t, q_norm, kv_norm, wq, wk, wv, batch, seq):
    n = p.shape[0]
    tm = min(TM_PREP, seq)
    hq = MLA_HEADS * LANES
    hv = MLA_HEADS * MLA_V
    scale = float(MLA_NOPE + MLA_ROPE) ** -0.5 * LOG2E
    const = lambda i: (0, 0)
    vt_spec, vt_shape = _vt_out(batch, seq, tm)
    return pl.pallas_call(
        functools.partial(_mla_prep_kernel, scale=scale),
        grid=(n // tm,),
        in_specs=[pl.BlockSpec((tm, MLA_Q_RANK), lambda i: (i, COL_CQ // MLA_Q_RANK)),
                  pl.BlockSpec((tm, MLA_KV_RANK), lambda i: (i, COL_CKV // MLA_KV_RANK)),
                  pl.BlockSpec((tm, LANES), lambda i: (i, COL_KR // LANES)),
                  pl.BlockSpec((tm, LANES), lambda i: (i, 0)),
                  pl.BlockSpec((tm, LANES), lambda i: (i, 0)),
                  pl.BlockSpec((1, MLA_Q_RANK), const),
                  pl.BlockSpec((1, MLA_KV_RANK), const),
                  pl.BlockSpec((MLA_Q_RANK, hq), const),
                  pl.BlockSpec((MLA_KV_RANK, hq), const),
                  pl.BlockSpec((MLA_KV_RANK, hv), const)],
        out_specs=[pl.BlockSpec((tm, hq), lambda i: (i, 0)),
                   pl.BlockSpec((tm, hq), lambda i: (i, 0)),
                   vt_spec],
        out_shape=[jax.ShapeDtypeStruct((n, hq), BF16),
                   jax.ShapeDtypeStruct((n, hq), BF16),
                   vt_shape],
        compiler_params=_cparams("parallel"),
        name="mla_prep",
    )(p, p, p, cos_t, sin_t, q_norm, kv_norm, wq, wk, wv)


def _diff_prep_kernel(q_ref, k_ref, v_ref, cos_ref, sin_ref, qo_ref, ko_ref, vt_ref):
    cos_t, sin_t = cos_ref[...], sin_ref[...]
    for src, dst in ((q_ref, qo_ref), (k_ref, ko_ref)):
        for h in range(DIFF_HEADS):
            sl = slice(h * LANES, (h + 1) * LANES)
            t = src[:, sl].astype(F32)
            dst[:, sl] = _rope_lanes(t, cos_t, sin_t, 0, DIFF_ROT // 2, DIFF_HD).astype(BF16)
    _store_vt(vt_ref, v_ref[...].astype(F32))


def _diff_prep(p, cos_t, sin_t, batch, seq):
    n = p.shape[0]
    tm = min(TM_PREP, seq)
    w = DIFF_HEADS * 2 * DIFF_HD
    vt_spec, vt_shape = _vt_out(batch, seq, tm)
    return pl.pallas_call(
        _diff_prep_kernel,
        grid=(n // tm,),
        in_specs=[pl.BlockSpec((tm, w), lambda i: (i, COL_DQ // w)),
                  pl.BlockSpec((tm, w), lambda i: (i, COL_DK // w)),
                  pl.BlockSpec((tm, w), lambda i: (i, COL_DV // w)),
                  pl.BlockSpec((tm, LANES), lambda i: (i, 0)),
                  pl.BlockSpec((tm, LANES), lambda i: (i, 0))],
        out_specs=[pl.BlockSpec((tm, w), lambda i: (i, 0)),
                   pl.BlockSpec((tm, w), lambda i: (i, 0)),
                   vt_spec],
        out_shape=[jax.ShapeDtypeStruct((n, w), BF16)] * 2 + [vt_shape],
        compiler_params=_cparams("parallel"),
        name="diff_prep",
    )(p, p, p, cos_t, sin_t)


def _fox_prep_kernel(z_ref, k_ref, v_ref, b_ref, tri_ref, sel_ref, ko_ref, vt_ref, carry):
    @pl.when(pl.program_id(1) == 0)
    def _():
        carry[...] = jnp.zeros_like(carry)

    z = z_ref[...].astype(F32) + b_ref[...]
    log_f = jnp.minimum(z, 0.0) - jnp.log(1.0 + jnp.exp(-jnp.abs(z)))
    cum = jnp.dot(tri_ref[...], log_f, precision=lax.Precision.HIGHEST,
                  preferred_element_type=F32) + carry[...]
    ts = cum.shape[0]
    carry[...] = cum[ts - 1:ts, :]
    neg = cum * (-LOG2E)
    hi = neg.astype(BF16)
    r1 = neg - hi.astype(F32)
    mid = r1.astype(BF16)
    lo = (r1 - mid.astype(F32)).astype(BF16)
    parts = jnp.concatenate([hi, mid, lo], axis=1)
    for p in range(PAIRS):
        ko_ref[:, 2 * p * LANES:(2 * p + 1) * LANES] = k_ref[:, p * LANES:(p + 1) * LANES]
        ext = jnp.dot(parts, sel_ref[p], preferred_element_type=F32)
        ko_ref[:, (2 * p + 1) * LANES:(2 * p + 2) * LANES] = ext.astype(BF16)
    _store_vt(vt_ref, v_ref[...].astype(F32))


def _fox_decay_select():
    sel = np.zeros((PAIRS, 3 * LANES, LANES), np.float32)
    for p in range(PAIRS):
        for t in range(2):
            for j in range(3):
                sel[p, j * LANES + 2 * p + t, 3 * t + j] = 1.0
    return jnp.asarray(sel, BF16)


def _fox_prep(p, f_bias, batch, seq):
    n = p.shape[0]
    ts = min(TS_CUM, seq)
    per_b = seq // ts
    w = FOX_HEADS * FOX_HD
    bias = jnp.zeros((1, LANES), F32).at[0, :FOX_HEADS].set(f_bias)
    tri = jnp.tril(jnp.ones((ts, ts), F32))
    row = lambda b, j: b * per_b + j
    return pl.pallas_call(
        _fox_prep_kernel,
        grid=(batch, per_b),
        in_specs=[pl.BlockSpec((ts, LANES), lambda b, j: (row(b, j), COL_ZF // LANES)),
                  pl.BlockSpec((ts, w), lambda b, j: (row(b, j), COL_FOXK // w)),
                  pl.BlockSpec((ts, w), lambda b, j: (row(b, j), COL_FOXV // w)),
                  pl.BlockSpec((1, LANES), lambda b, j: (0, 0)),
                  pl.BlockSpec((ts, ts), lambda b, j: (0, 0)),
                  pl.BlockSpec((PAIRS, 3 * LANES, LANES), lambda b, j: (0, 0, 0))],
        out_specs=[pl.BlockSpec((ts, 2 * w), lambda b, j: (row(b, j), 0)),
                   pl.BlockSpec((1, PAIRS, VT_ROWS, ts), lambda b, j: (b, 0, 0, j))],
        out_shape=[jax.ShapeDtypeStruct((n, 2 * w), BF16),
                   jax.ShapeDtypeStruct((batch, PAIRS, VT_ROWS, seq), BF16)],
        scratch_shapes=[pltpu.VMEM((1, LANES), F32)],
        compiler_params=_cparams("parallel", "arbitrary"),
        name="fox_prep",
    )(p, p, p, bias, tri, _fox_decay_select())


def _attn_kernel(*refs, mode, tq, tk, per_frame, lam_init):
    if mode == "diff":
        q_ref, k_ref, vt_ref, lam_ref, subln_ref, o_ref, q_scr, s_scr, m_scr, acc_scr = refs
    else:
        q_ref, k_ref, vt_ref, o_ref, q_scr, s_scr, m_scr, acc_scr = refs
    w = 2 * tq
    i = pl.program_id(2)
    q = q_ref[...]
    zero = jnp.zeros((tq, LANES), BF16)
    lane = lax.broadcasted_iota(I32, (tq, LANES), 1)
    if mode == "mla":
        q_scr[0:tq, :] = jnp.concatenate([q[:, :LANES], zero], axis=1)
        q_scr[tq:w, :] = jnp.concatenate([zero, q[:, LANES:]], axis=1)
    elif mode == "fox":
        ind_a = jnp.where(lane < 3, 1.0, 0.0).astype(BF16)
        ind_b = jnp.where(lane < 3, 0.0, jnp.where(lane < 6, 1.0, 0.0)).astype(BF16)
        q_scr[0:tq, :] = jnp.concatenate([jnp.where(lane < FOX_HD, q, zero), ind_a], axis=1)
        q_scr[tq:w, :] = jnp.concatenate([jnp.where(lane < FOX_HD, zero, q), ind_b], axis=1)
    else:
        q_scr[0:tq, :] = jnp.where(lane < DIFF_HD, q, zero)
        q_scr[tq:w, :] = jnp.where(lane < DIFF_HD, zero, q)
    m_scr[...] = jnp.full(m_scr.shape, NEG_BIG, F32)
    acc_scr[...] = jnp.zeros(acc_scr.shape, F32)

    def logits(j, slot):
        ks = pl.multiple_of(j * tk, tk)
        s_scr[slot] = lax.dot_general(k_ref[pl.ds(ks, tk), :], q_scr[...], (((1,), (1,)), ((), ())),
                                      preferred_element_type=F32)

    def consume(j, slot, masked):
        ks = pl.multiple_of(j * tk, tk)
        s = s_scr[slot]
        if masked:
            k_pos = ks + lax.broadcasted_iota(I32, (tk, w), 0)
            q_pos = i * tq + (lax.broadcasted_iota(I32, (tk, w), 1) & (tq - 1))
            if per_frame:
                visible = k_pos <= q_pos
            else:
                visible = (k_pos >> CHUNK_SHIFT) <= (q_pos >> CHUNK_SHIFT)
            s = jnp.where(visible, s, NEG_BIG)
        m_prev = m_scr[...]
        m_new = jnp.maximum(m_prev, jnp.max(s, axis=0, keepdims=True))
        alpha = jnp.exp2(m_prev - m_new)
        p = jnp.exp2(s - m_new).astype(BF16)
        acc_scr[...] = alpha * acc_scr[...] + jnp.dot(vt_ref[:, pl.ds(ks, tk)], p,
                                                      preferred_element_type=F32)
        m_scr[...] = m_new

    n_un = (i * tq) // tk
    logits(0, 0)

    def run_tiles(j0, count):
        for t in range(count):
            logits(j0 + t + 1, (t + 1) % 2)
            consume(j0 + t, t % 2, False)

    def quad_body(jq, carry):
        run_tiles(ATT_UNROLL * jq, ATT_UNROLL)
        return carry

    n_quads = n_un // ATT_UNROLL
    lax.fori_loop(0, n_quads, quad_body, 0)
    base = ATT_UNROLL * n_quads
    if ATT_UNROLL == 4:
        @pl.when(n_un - base >= 2)
        def _():
            run_tiles(base, 2)

    @pl.when(n_un % 2 == 1)
    def _():
        logits(n_un, 1)
        consume(n_un - 1, 0, False)
        consume(n_un, 1, True)

    @pl.when(n_un % 2 == 0)
    def _():
        consume(n_un, 0, True)

    oa = acc_scr[0:LANES, 0:tq] / acc_scr[LANES:LANES + 1, 0:tq]
    ob = acc_scr[0:LANES, tq:w] / acc_scr[LANES:LANES + 1, tq:w]
    if mode == "diff":
        lv = lam_ref[...]
        lam = (jnp.exp(jnp.sum(lv[0:1] * lv[1:2], axis=1, keepdims=True))
               - jnp.exp(jnp.sum(lv[2:3] * lv[3:4], axis=1, keepdims=True)) + lam_init)
        o = oa - lam * ob
        o = o * lax.rsqrt(jnp.mean(o * o, axis=0, keepdims=True) + RMS_EPS)
        o = o.T * subln_ref[...] * (1.0 - lam_init)
    else:
        row = lax.broadcasted_iota(I32, (LANES, tq), 0)
        o = jnp.where(row < FOX_HD, oa, ob).T
    o_ref[...] = o.astype(BF16)


def _attention(mode, batch, seq, *, q, k, vt, q_col=0, extra=(), lam_init=0.0):
    n = batch * seq
    tq = min(TQ_ATT, seq // 2)
    tk = min(TK_ATT, seq)
    nq = seq // tq
    kw = LANES if mode == "diff" else 2 * LANES
    qw = 2 * LANES if mode == "mla" else LANES
    qb = q_col // qw
    in_specs = [pl.BlockSpec((tq, qw), lambda b, p, i: (b * nq + i, qb + p)),
                pl.BlockSpec((seq, kw), lambda b, p, i: (b, p)),
                pl.BlockSpec((1, 1, VT_ROWS, seq), lambda b, p, i: (b, p, 0, 0))]
    args = (q, k, vt)
    if mode == "diff":
        in_specs += [pl.BlockSpec((4, DIFF_HD), lambda b, p, i: (0, 0)),
                     pl.BlockSpec((1, LANES), lambda b, p, i: (0, 0))]
        args += tuple(extra)

    def body(*refs):
        refs = list(refs)
        refs[2] = refs[2].at[0, 0]
        _attn_kernel(*refs, mode=mode, tq=tq, tk=tk, per_frame=(mode == "fox"), lam_init=lam_init)

    return pl.pallas_call(
        body,
        grid=(batch, PAIRS, nq),
        in_specs=in_specs,
        out_specs=pl.BlockSpec((tq, LANES), lambda b, p, i: (b * nq + i, p)),
        out_shape=jax.ShapeDtypeStruct((n, PAIRS * LANES), BF16),
        scratch_shapes=[pltpu.VMEM((2 * tq, kw), BF16),
                        pltpu.VMEM((2, tk, 2 * tq), F32),
                        pltpu.VMEM((1, 2 * tq), F32),
                        pltpu.VMEM((VT_ROWS, 2 * tq), F32)],
        compiler_params=_cparams("parallel", "parallel", "arbitrary"),
        name="attn_" + mode,
    )(*args)


def _conv_kernel(b_ref, c_ref, u_ref, ch_ref, uh_ref, w_ref, o_ref, *, per_b):
    u = c_ref[...].astype(F32) * u_ref[...].astype(F32)
    halo = ch_ref[...].astype(F32) * uh_ref[...].astype(F32)
    halo = jnp.where(pl.program_id(0) % per_b == 0, 0.0, halo)
    hr = halo.shape[0]
    row = lax.broadcasted_iota(I32, u.shape, 0)
    u1 = jnp.where(row == 0, halo[hr - 1:hr, :], pltpu.roll(u, 1, 0))
    u2 = jnp.where(row == 0, halo[hr - 2:hr - 1, :],
                   jnp.where(row == 1, halo[hr - 1:hr, :], pltpu.roll(u, 2, 0)))
    w = w_ref[...]
    y = u2 * w[0:1, :] + u1 * w[1:2, :] + u * w[2:3, :]
    o_ref[...] = (b_ref[...].astype(F32) * y).astype(BF16)


def _short_conv(p, conv_w, seq):
    n = p.shape[0]
    ts = min(TM_PREP, seq)
    per_b = seq // ts
    w = CONV_WIDTH
    hr = BF16_SUBLANES
    halo_map = lambda col: (lambda i: (jnp.maximum(i * (ts // hr) - 1, 0), col // w))
    return pl.pallas_call(
        functools.partial(_conv_kernel, per_b=per_b),
        grid=(n // ts,),
        in_specs=[pl.BlockSpec((ts, w), lambda i: (i, COL_CB // w)),
                  pl.BlockSpec((ts, w), lambda i: (i, COL_CC // w)),
                  pl.BlockSpec((ts, w), lambda i: (i, COL_CU // w)),
                  pl.BlockSpec((hr, w), halo_map(COL_CC)),
                  pl.BlockSpec((hr, w), halo_map(COL_CU)),
                  pl.BlockSpec((CONV_TAPS, w), lambda i: (0, 0))],
        out_specs=pl.BlockSpec((ts, w), lambda i: (i, 0)),
        out_shape=jax.ShapeDtypeStruct((n, w), BF16),
        compiler_params=_cparams("parallel"),
        name="short_conv",
    )(p, p, p, p, p, conv_w)


def _merge_kernel(o0, o1, o2, o3, g0, g1, g2, g3, wb_ref, wo_ref, x_ref, gm_ref, out_ref):
    merged = None
    for br, (o_ref, g_ref) in enumerate(((o0, g0), (o1, g1), (o2, g2), (o3, g3))):
        t = jnp.dot(o_ref[...], wb_ref[br], preferred_element_type=F32) * g_ref[...].astype(F32)
        merged = t if merged is None else merged + t
    y = jnp.dot(merged.astype(BF16), wo_ref[...], preferred_element_type=F32)
    out_ref[...] = x_ref[...] + gm_ref[0] * y


def _merge(branches, gates, w_branch, w_out, x, g_m, seq):
    n, d = x.shape
    tm = min(TM_MERGE, seq)
    per_b = seq // tm
    bw = BRANCH_WIDTH
    o_spec = pl.BlockSpec((tm, bw), lambda i: (i, 0))
    g_specs = [pl.BlockSpec((tm, d), lambda i, br=br: (i, br)) for br in range(4)]
    single = pl.Buffered(1)
    return pl.pallas_call(
        _merge_kernel,
        grid=(n // tm,),
        in_specs=[o_spec] * 4 + g_specs + [
            pl.BlockSpec((4, bw, d), lambda i: (0, 0, 0), pipeline_mode=single),
            pl.BlockSpec((d, d), lambda i: (0, 0), pipeline_mode=single),
            pl.BlockSpec((tm, d), lambda i: (i, 0)),
            pl.BlockSpec((1, 1, d), lambda i: (i // per_b, 0, 0))],
        out_specs=pl.BlockSpec((tm, d), lambda i: (i, 0)),
        out_shape=jax.ShapeDtypeStruct((n, d), F32),
        compiler_params=_cparams("parallel"),
        name="branch_merge",
    )(*branches, gates, gates, gates, gates, w_branch, w_out, x, g_m)


def _router_kernel(x_ref, g_ref, sc_ref, sh_ref, wr_ref, rb_ref, tri_ref,
                   h_ref, idx_ref, wts_ref, rank_ref, cnt_ref, carry):
    @pl.when(pl.program_id(0) == 0)
    def _():
        carry[...] = jnp.zeros_like(carry)

    h = _normmod(x_ref[...], g_ref[...], sc_ref[0], sh_ref[0])
    h_ref[...] = h
    logits = jnp.dot(h, wr_ref[...], precision=lax.Precision.HIGHEST, preferred_element_type=F32)
    scores = jax.nn.sigmoid(logits)
    work = scores + rb_ref[...]
    tm, ne = scores.shape
    lane = lax.broadcasted_iota(I32, (tm, ne), 1).astype(F32)
    slot = lax.broadcasted_iota(I32, (tm, TOP_K), 1)
    idx = jnp.zeros((tm, TOP_K), F32)
    sel = jnp.zeros((tm, TOP_K), F32)
    hits = []
    chosen = jnp.zeros((tm, ne), F32)
    for kk in range(TOP_K):
        mx = jnp.max(work, axis=1, keepdims=True)
        first = jnp.min(jnp.where(work == mx, lane, float(ne)), axis=1, keepdims=True)
        hit = lane == first
        hits.append(hit)
        chosen = jnp.where(hit, 1.0, chosen)
        sc_k = jnp.sum(jnp.where(hit, scores, 0.0), axis=1, keepdims=True)
        work = jnp.where(hit, -jnp.inf, work)
        idx = jnp.where(slot == kk, first, idx)
        sel = jnp.where(slot == kk, sc_k, sel)
    idx_ref[...] = idx.astype(I32)
    wts_ref[...] = sel / jnp.sum(sel, axis=1, keepdims=True) * ROUTED_SCALE
    before = jnp.dot(tri_ref[...], chosen.astype(BF16), preferred_element_type=F32) + carry[...]
    rank = jnp.zeros((tm, TOP_K), F32)
    for kk in range(TOP_K):
        r_k = jnp.sum(jnp.where(hits[kk], before, 0.0), axis=1, keepdims=True)
        rank = jnp.where(slot == kk, r_k, rank)
    rank_ref[...] = rank.astype(I32)
    total = carry[...] + jnp.sum(chosen, axis=0, keepdims=True)
    carry[...] = total
    cnt_ref[...] = total.astype(I32)


def _router(x, g, sc, sh, w_router, router_bias, seq):
    n, d = x.shape
    tm = min(TM_ROUTE, seq)
    per_b = seq // tm
    ne = w_router.shape[1]
    tri = jnp.tril(jnp.ones((tm, tm), F32), k=-1).astype(BF16)
    return pl.pallas_call(
        _router_kernel,
        grid=(n // tm,),
        in_specs=[pl.BlockSpec((tm, d), lambda i: (i, 0)),
                  pl.BlockSpec((1, d), lambda i: (0, 0)),
                  pl.BlockSpec((1, 1, d), lambda i: (i // per_b, 0, 0)),
                  pl.BlockSpec((1, 1, d), lambda i: (i // per_b, 0, 0)),
                  pl.BlockSpec((d, ne), lambda i: (0, 0)),
                  pl.BlockSpec((1, ne), lambda i: (0, 0)),
                  pl.BlockSpec((tm, tm), lambda i: (0, 0))],
        out_specs=[pl.BlockSpec((tm, d), lambda i: (i, 0)),
                   pl.BlockSpec((tm, TOP_K), lambda i: (i, 0)),
                   pl.BlockSpec((tm, TOP_K), lambda i: (i, 0)),
                   pl.BlockSpec((tm, TOP_K), lambda i: (i, 0)),
                   pl.BlockSpec((1, ne), lambda i: (0, 0))],
        out_shape=[jax.ShapeDtypeStruct((n, d), F32),
                   jax.ShapeDtypeStruct((n, TOP_K), I32),
                   jax.ShapeDtypeStruct((n, TOP_K), F32),
                   jax.ShapeDtypeStruct((n, TOP_K), I32),
                   jax.ShapeDtypeStruct((1, ne), I32)],
        scratch_shapes=[pltpu.VMEM((1, ne), F32)],
        compiler_params=_cparams("arbitrary"),
        name="moe_router",
    )(x, g, sc, sh, w_router, router_bias, tri)


GATHER_BUFS = 3


def _gather_step(i, slot, n_steps, idx_hbm, idx_smem, idx_sem, data_hbm, buf, data_sem, n_rows):
    assert n_steps >= GATHER_BUFS

    def idx_copy(step, slot):
        return pltpu.make_async_copy(idx_hbm.at[step], idx_smem.at[pl.ds(slot, 1)], idx_sem.at[slot])

    def rows_wait(slot):
        pltpu.make_async_copy(data_hbm.at[pl.ds(0, n_rows)], buf.at[slot], data_sem.at[slot]).wait()

    def issue_rows(slot):
        for r in range(n_rows):
            row = idx_smem[slot, r]
            pltpu.make_async_copy(data_hbm.at[pl.ds(row, 1)], buf.at[slot, pl.ds(r, 1)],
                                  data_sem.at[slot]).start(priority=r % 2)

    ahead = (slot + 2) % GATHER_BUFS
    last = n_steps - 1

    if slot == 0:
        @pl.when(i == 0)
        def _():
            for t in range(2):
                idx_copy(t, t).start()
                idx_copy(t, t).wait()
                issue_rows(t)
            idx_copy(2, 2).start()

    rows_wait(slot)
    idx_copy(jnp.minimum(i + 2, last), ahead).wait()

    def prefetch():
        issue_rows(ahead)
        idx_copy(jnp.minimum(i + 3, last), slot).start()

    def drain():
        @pl.when(i == last)
        def _():
            rows_wait((slot + 1) % GATHER_BUFS)
            rows_wait(ahead)
            idx_copy(last, slot).wait()

    return prefetch, drain


def _for_each_parity(i, step):
    for slot in range(GATHER_BUFS):
        @pl.when(i % GATHER_BUFS == slot)
        def _():
            step(slot)


def _expert_kernel(te_ref, src_hbm, h_hbm, wg_ref, wu_ref, wd_ref, y_ref,
                   idx_smem, idx_sem, xbuf, x_sem, wg_s, wu_s, wd_s, xs_s, *, n_steps, tm):
    i = pl.program_id(0)
    new_expert = jnp.logical_or(i == 0, te_ref[i] != te_ref[jnp.maximum(i - 1, 0)])

    @pl.when(new_expert)
    def _():
        wg_s[...] = wg_ref[0, 0].astype(BF16)
        wu_s[...] = wu_ref[0, 0].astype(BF16)
        wd_s[...] = wd_ref[0, 0].astype(BF16)

    def step(slot):
        prefetch, drain = _gather_step(i, slot, n_steps, src_hbm, idx_smem, idx_sem, h_hbm, xbuf, x_sem, tm)
        xs_s[...] = xbuf[slot].astype(BF16)
        prefetch()
        x = xs_s[...]
        a = jnp.dot(x, wg_s[...], preferred_element_type=F32)
        u = jnp.dot(x, wu_s[...], preferred_element_type=F32)
        hid = (a * jax.nn.sigmoid(a)) * u
        y_ref[...] = jnp.dot(hid.astype(BF16), wd_s[...], preferred_element_type=F32)
        drain()

    _for_each_parity(i, step)


def _expert_ffn(layer, tile_expert, src, h, w_eg, w_eu, w_ed):
    n_steps, _, tm = src.shape
    d = w_eg.shape[2]
    ff = w_eg.shape[3]
    grid_spec = pltpu.PrefetchScalarGridSpec(
        num_scalar_prefetch=1,
        grid=(n_steps,),
        in_specs=[pl.BlockSpec(memory_space=pl.ANY),
                  pl.BlockSpec(memory_space=pl.ANY),
                  pl.BlockSpec((1, 1, d, ff), lambda i, te: (layer, te[i], 0, 0)),
                  pl.BlockSpec((1, 1, d, ff), lambda i, te: (layer, te[i], 0, 0)),
                  pl.BlockSpec((1, 1, ff, d), lambda i, te: (layer, te[i], 0, 0))],
        out_specs=pl.BlockSpec((tm, d), lambda i, te: (i, 0)),
        scratch_shapes=[pltpu.SMEM((GATHER_BUFS, tm), I32),
                        pltpu.SemaphoreType.DMA((GATHER_BUFS,)),
                        pltpu.VMEM((GATHER_BUFS, tm, d), F32),
                        pltpu.SemaphoreType.DMA((GATHER_BUFS,)),
                        pltpu.VMEM((d, ff), BF16),
                        pltpu.VMEM((d, ff), BF16),
                        pltpu.VMEM((ff, d), BF16),
                        pltpu.VMEM((tm, d), BF16)],
    )
    return pl.pallas_call(
        functools.partial(_expert_kernel, n_steps=n_steps, tm=tm),
        grid_spec=grid_spec,
        out_shape=jax.ShapeDtypeStruct((n_steps * tm, d), F32),
        compiler_params=_cparams("arbitrary"),
        name="moe_experts",
    )(tile_expert, src, h, w_eg, w_eu, w_ed)


def _shared_kernel(h_ref, wg_ref, wu_ref, wd_ref, o_ref):
    h = h_ref[...].astype(BF16)
    a = jnp.dot(h, wg_ref[...], preferred_element_type=F32)
    u = jnp.dot(h, wu_ref[...], preferred_element_type=F32)
    hid = ((a * jax.nn.sigmoid(a)) * u).astype(BF16)
    o_ref[...] = jnp.dot(hid, wd_ref[...], preferred_element_type=F32)


def _shared_ffn(h, wg, wu, wd):
    n, d = h.shape
    ff = wg.shape[1]
    tm = min(TM_FFN, n)
    return pl.pallas_call(
        _shared_kernel,
        grid=(n // tm,),
        in_specs=[pl.BlockSpec((tm, d), lambda i: (i, 0)),
                  pl.BlockSpec((d, ff), lambda i: (0, 0)),
                  pl.BlockSpec((d, ff), lambda i: (0, 0)),
                  pl.BlockSpec((ff, d), lambda i: (0, 0))],
        out_specs=pl.BlockSpec((tm, d), lambda i: (i, 0)),
        out_shape=jax.ShapeDtypeStruct((n, d), F32),
        compiler_params=_cparams("parallel"),
        name="moe_shared",
    )(h, wg, wu, wd)


def _combine_kernel(dest_hbm, y_hbm, wts_ref, sh_ref, x_ref, gf_ref, fn_ref, o_ref,
                    idx_smem, idx_sem, ybuf, y_sem, *, n_steps, tc, final_norm):
    i = pl.program_id(0)
    rows = 8
    d = x_ref.shape[1]

    def step(slot):
        prefetch, drain = _gather_step(i, slot, n_steps, dest_hbm, idx_smem, idx_sem, y_hbm, ybuf, y_sem,
                                       TOP_K * tc)
        prefetch()
        for c in range(tc // rows):
            r0 = c * rows
            wts = wts_ref[r0:r0 + rows, :]
            wb = [jnp.broadcast_to(wts[:, kk:kk + 1], (rows, LANES)) for kk in range(TOP_K)]
            outs = []
            for s in range(d // LANES):
                sl = slice(s * LANES, (s + 1) * LANES)
                acc = sh_ref[r0:r0 + rows, sl]
                for kk in range(TOP_K):
                    acc = acc + wb[kk] * ybuf[slot, kk * tc + r0:kk * tc + r0 + rows, sl]
                outs.append(x_ref[r0:r0 + rows, sl] + gf_ref[0, :, sl] * acc)
            if final_norm:
                ssq = outs[0] * outs[0]
                for o in outs[1:]:
                    ssq = ssq + o * o
                inv = lax.rsqrt(jnp.sum(ssq, axis=1, keepdims=True) * (1.0 / d) + RMS_EPS)
                outs = [o * inv * fn_ref[:, s * LANES:(s + 1) * LANES] for s, o in enumerate(outs)]
            for s, o in enumerate(outs):
                o_ref[r0:r0 + rows, s * LANES:(s + 1) * LANES] = o
        drain()

    _for_each_parity(i, step)


def _combine(dest, y, wts, shared, x, g_f, final_gain, seq, *, final_norm):
    n, d = x.shape
    n_steps, _, rows = dest.shape
    tc = rows // TOP_K
    per_b = seq // tc
    return pl.pallas_call(
        functools.partial(_combine_kernel, n_steps=n_steps, tc=tc, final_norm=final_norm),
        grid=(n_steps,),
        in_specs=[pl.BlockSpec(memory_space=pl.ANY),
                  pl.BlockSpec(memory_space=pl.ANY),
                  pl.BlockSpec((tc, TOP_K), lambda i: (i, 0)),
                  pl.BlockSpec((tc, d), lambda i: (i, 0)),
                  pl.BlockSpec((tc, d), lambda i: (i, 0)),
                  pl.BlockSpec((1, 1, d), lambda i: (i // per_b, 0, 0)),
                  pl.BlockSpec((1, d), lambda i: (0, 0))],
        out_specs=pl.BlockSpec((tc, d), lambda i: (i, 0)),
        out_shape=jax.ShapeDtypeStruct((n, d), F32),
        scratch_shapes=[pltpu.SMEM((GATHER_BUFS, rows), I32),
                        pltpu.SemaphoreType.DMA((GATHER_BUFS,)),
                        pltpu.VMEM((GATHER_BUFS, rows, d), F32),
                        pltpu.SemaphoreType.DMA((GATHER_BUFS,))],
        compiler_params=_cparams("arbitrary"),
        name="moe_combine",
    )(dest, y, wts, shared, x, g_f, final_gain)


def _rope_tables(positions):
    pos = positions.reshape(-1).astype(F32)
    n = pos.shape[0]

    def cos_sin(half):
        inv_freq = ROPE_THETA ** (-jnp.arange(half, dtype=F32) / half)
        ang = pos[:, None] * inv_freq
        return jnp.cos(ang), jnp.sin(ang)

    cm, sm = cos_sin(MLA_ROPE // 2)
    tail = LANES - MLA_NOPE - MLA_ROPE
    mla_cos = jnp.concatenate([jnp.ones((n, MLA_NOPE), F32), cm, cm, jnp.ones((n, tail), F32)], axis=1)
    mla_sin = jnp.concatenate([jnp.zeros((n, MLA_NOPE), F32), sm, sm, jnp.zeros((n, tail), F32)], axis=1)
    cd, sd = cos_sin(DIFF_ROT // 2)
    rest = DIFF_HD - DIFF_ROT
    dc = jnp.concatenate([cd, cd, jnp.ones((n, rest), F32)], axis=1)
    ds = jnp.concatenate([sd, sd, jnp.zeros((n, rest), F32)], axis=1)
    return mla_cos, mla_sin, jnp.concatenate([dc, dc], axis=1), jnp.concatenate([ds, ds], axis=1)


def _in_proj_weight(w_in):
    d = w_in.shape[0]
    sizes = (MLA_Q_RANK, MLA_KV_RANK, MLA_ROPE, 512, 512, 512, FOX_HEADS, 512, 512, 512, 512, 512, 512)
    parts, off = [], 0
    for s in sizes:
        parts.append(w_in[:, off:off + s])
        off += s
    a_cq, a_ckv, a_kr, b_q, b_k, b_v, b_zf, c_b, c_c, c_u, d_q, d_k, d_v = parts
    zeros = lambda w: jnp.zeros((d, w), w_in.dtype)
    cols = [b_q, b_k, b_v, c_b, c_c, c_u, d_q, d_k, d_v, a_cq,
            zeros(MLA_NOPE), a_kr, zeros(LANES - MLA_NOPE - MLA_ROPE),
            a_ckv, b_zf, zeros(LANES - FOX_HEADS), zeros(P_WIDTH - COL_ZF - LANES)]
    return jnp.concatenate(cols, axis=1).astype(BF16)


def _in_proj_scale():
    s = np.ones((1, P_WIDTH), np.float32)
    qs = FOX_HD ** -0.5 * LOG2E
    s[0, COL_FOXQ:COL_FOXQ + FOX_HEADS * FOX_HD] = qs
    s[0, COL_DQ:COL_DQ + DIFF_HEADS * 2 * DIFF_HD] = qs
    return jnp.asarray(s)


def _dispatch_plan(idx, rank, counts, tm):
    n = idx.shape[0]
    counts = counts.reshape(-1)
    padded = (counts + tm - 1) // tm * tm
    ends = jnp.cumsum(padded)
    starts = ends - padded
    dest = (starts[idx] + rank).astype(I32)
    n_steps = (n * TOP_K) // tm + N_EXPERTS
    tok = jnp.repeat(jnp.arange(n, dtype=I32), TOP_K)
    src = jnp.zeros((n_steps * tm,), I32).at[dest.reshape(-1)].set(
        tok, unique_indices=True, indices_are_sorted=False, mode="promise_in_bounds")
    tile_start = jnp.arange(n_steps, dtype=I32) * tm
    tile_expert = jnp.sum((ends[None, :] <= tile_start[:, None]).astype(I32), axis=1)
    tile_expert = jnp.minimum(tile_expert, N_EXPERTS - 1)
    return dest, src.reshape(n_steps, 1, tm), tile_expert


def kernel(x, c, positions, ada_w, ada_b, norm_mix, norm_ffn, w_in, mla_q_norm, mla_w_uq, mla_kv_norm,
           mla_w_ukv, fox_f_bias, conv_w, diff_lam_q1, diff_lam_k1, diff_lam_q2, diff_lam_k2, diff_subln,
           w_branch, w_gate, b_gate, w_out, w_router, router_bias, w_exp_gate, w_exp_up, w_exp_down,
           w_sh_gate, w_sh_up, w_sh_down, final_norm):
    batch, seq, d = x.shape
    n = batch * seq
    depth = ada_w.shape[0]
    xf = x.reshape(n, d)

    mod = _modulation(c, ada_w, ada_b)
    mla_cos, mla_sin, diff_cos, diff_sin = _rope_tables(positions)
    p_scale = _in_proj_scale()

    for l in range(depth):
        lam_init = 0.8 - 0.6 * math.exp(-0.3 * l)
        mods = mod[l, :batch].reshape(batch, 6, 1, d)
        sh_m, sc_m, g_m, sh_f, sc_f, g_f = (mods[:, t] for t in range(6))

        p = _norm_proj(xf, norm_mix[l][None], sc_m, sh_m, _in_proj_weight(w_in[l]), p_scale, seq,
                       sigmoid=False, name="in_proj")
        w_g = w_gate[l].transpose(1, 0, 2).reshape(d, 4 * d).astype(BF16)
        gates = _norm_proj(xf, norm_mix[l][None], sc_m, sh_m, w_g, b_gate[l].reshape(1, 4 * d), seq,
                           sigmoid=True, name="branch_gates")

        wq = jnp.pad(mla_w_uq[l].reshape(MLA_Q_RANK, MLA_HEADS, MLA_NOPE + MLA_ROPE),
                     ((0, 0), (0, 0), (0, LANES - MLA_NOPE - MLA_ROPE)))
        wq = wq.reshape(MLA_Q_RANK, MLA_HEADS * LANES).astype(BF16)
        wkv = mla_w_ukv[l].reshape(MLA_KV_RANK, MLA_HEADS, MLA_NOPE + MLA_V)
        wk = jnp.pad(wkv[:, :, :MLA_NOPE], ((0, 0), (0, 0), (0, LANES - MLA_NOPE)))
        wk = wk.reshape(MLA_KV_RANK, MLA_HEADS * LANES).astype(BF16)
        wv = wkv[:, :, MLA_NOPE:].reshape(MLA_KV_RANK, MLA_HEADS * MLA_V).astype(BF16)
        q_a, k_a, vt_a = _mla_prep(p, mla_cos, mla_sin, mla_q_norm[l][None], mla_kv_norm[l][None],
                                   wq, wk, wv, batch, seq)
        o_mla = _attention("mla", batch, seq, q=q_a, k=k_a, vt=vt_a)

        k_f, vt_f = _fox_prep(p, fox_f_bias[l], batch, seq)
        o_fox = _attention("fox", batch, seq, q=p, k=k_f, vt=vt_f, q_col=COL_FOXQ)

        o_conv = _short_conv(p, conv_w[l], seq)

        q_d, k_d, vt_d = _diff_prep(p, diff_cos, diff_sin, batch, seq)
        lam_vecs = jnp.stack([diff_lam_q1[l], diff_lam_k1[l], diff_lam_q2[l], diff_lam_k2[l]])
        o_diff = _attention("diff", batch, seq, q=q_d, k=k_d, vt=vt_d,
                            extra=(lam_vecs, diff_subln[l][None]), lam_init=lam_init)

        xf = _merge((o_mla, o_fox, o_conv, o_diff), gates, w_branch[l].astype(BF16),
                    w_out[l].astype(BF16), xf, g_m, seq)

        h, idx, wts, rank, counts = _router(xf, norm_ffn[l][None], sc_f, sh_f, w_router[l],
                                            router_bias[l][None], seq)
        dest, src, tile_expert = _dispatch_plan(idx, rank, counts, min(TM_EXP, n))
        y = _expert_ffn(l, tile_expert, src, h, w_exp_gate, w_exp_up, w_exp_down)
        shared = _shared_ffn(h, w_sh_gate[l].astype(BF16), w_sh_up[l].astype(BF16),
                             w_sh_down[l].astype(BF16))
        tc = min(TC_COMB, seq)
        dest_t = dest.reshape(n // tc, tc, TOP_K).transpose(0, 2, 1).reshape(n // tc, 1, TOP_K * tc)
        xf = _combine(dest_t, y, wts, shared, xf, g_f, final_norm[None], seq, final_norm=(l == depth - 1))

    return xf.reshape(batch, seq, d)
```

```python
import functools
import math

import numpy as np
import jax
import jax.numpy as jnp
from jax import lax
from jax.experimental import pallas as pl
from jax.experimental.pallas import tpu as pltpu

F32 = jnp.float32
BF16 = jnp.bfloat16
I32 = jnp.int32

CHUNK = 64
CHUNK_SHIFT = 6
RMS_EPS = 1e-6
ROPE_THETA = 500000.0
MLA_HEADS, MLA_Q_RANK, MLA_KV_RANK, MLA_NOPE, MLA_ROPE, MLA_V = 8, 384, 256, 64, 32, 64
FOX_HEADS, FOX_HD = 8, 64
CONV_WIDTH, CONV_TAPS = 512, 3
DIFF_HEADS, DIFF_HD, DIFF_ROT = 4, 64, 16
N_EXPERTS, TOP_K, ROUTED_SCALE = 64, 8, 2.5
BRANCH_WIDTH = 512
LOG2E = 1.4426950408889634

LANES = 128
BF16_SUBLANES = 16
V7X_VMEM_BYTES = 64 * 1024 * 1024
VMEM_LIMIT = V7X_VMEM_BYTES - 8 * 1024 * 1024
NEG_BIG = -1e30

TM_PROJ = 1024
TM_PROJ_CHUNK = 256
TN_PROJ = 1024
TM_PREP = 512
TS_CUM = 256
TQ_ATT = 512
TK_ATT = 512
ATT_UNROLL = 4
TM_MERGE = 256
TM_ROUTE = 256
TM_EXP = 256
TM_FFN = 512
TC_COMB = 64

PAIRS = 4
VT_ROWS = LANES + BF16_SUBLANES

COL_FOXQ, COL_FOXK, COL_FOXV = 0, 512, 1024
COL_CB, COL_CC, COL_CU = 1536, 2048, 2560
COL_DQ, COL_DK, COL_DV = 3072, 3584, 4096
COL_CQ = 4608
COL_KR = 4992
COL_CKV = 5120
COL_ZF = 5376
P_WIDTH = 6144


def _cparams(*sem):
    return pltpu.CompilerParams(dimension_semantics=sem, vmem_limit_bytes=VMEM_LIMIT)


def _normmod(x, g, sc, sh):
    ms = jnp.mean(x * x, axis=-1, keepdims=True)
    return (x * lax.rsqrt(ms + RMS_EPS)) * g * (1.0 + sc) + sh


def _rope_lanes(x, cos_t, sin_t, lo, half, period):
    width = x.shape[1]
    lane = lax.broadcasted_iota(I32, x.shape, 1) & (period - 1)
    x_up = pltpu.roll(x, width - half, 1)
    x_dn = pltpu.roll(x, half, 1)
    rot = jnp.where(lane < lo, 0.0,
                    jnp.where(lane < lo + half, -x_up,
                              jnp.where(lane < lo + 2 * half, x_dn, 0.0)))
    return x * cos_t + rot * sin_t


def _store_vt(vt_ref, v):
    tm = v.shape[0]
    ones = jnp.ones((VT_ROWS - LANES, tm), BF16)
    for p in range(PAIRS):
        vt_ref[0, p, 0:LANES, :] = v[:, p * LANES:(p + 1) * LANES].T.astype(BF16)
        vt_ref[0, p, LANES:VT_ROWS, :] = ones


def _vt_out(batch, seq, tm):
    per_b = seq // tm
    spec = pl.BlockSpec((1, PAIRS, VT_ROWS, tm), lambda i: (i // per_b, 0, 0, i % per_b))
    return spec, jax.ShapeDtypeStruct((batch, PAIRS, VT_ROWS, seq), BF16)


def _mod_kernel(c_ref, w_ref, b_ref, o_ref):
    c = c_ref[...]
    ca = (c * jax.nn.sigmoid(c)).astype(BF16)
    o_ref[0] = jnp.dot(ca, w_ref[0].astype(BF16), preferred_element_type=F32) + b_ref[0]


def _modulation(c, ada_w, ada_b):
    n_layers, d, d6 = ada_w.shape
    b = c.shape[0]
    rows = 8
    cp = jnp.zeros((rows, d), F32).at[:b].set(c)
    tn = 1024
    return pl.pallas_call(
        _mod_kernel,
        grid=(n_layers, d6 // tn),
        in_specs=[pl.BlockSpec((rows, d), lambda l, j: (0, 0)),
                  pl.BlockSpec((1, d, tn), lambda l, j: (l, 0, j)),
                  pl.BlockSpec((1, 1, tn), lambda l, j: (l, 0, j))],
        out_specs=pl.BlockSpec((1, rows, tn), lambda l, j: (l, 0, j)),
        out_shape=jax.ShapeDtypeStruct((n_layers, rows, d6), F32),
        compiler_params=_cparams("parallel", "parallel"),
        name="adaln_modulation",
    )(cp, ada_w, ada_b.reshape(n_layers, 1, d6))


def _proj_kernel(x_ref, g_ref, sc_ref, sh_ref, w_ref, e_ref, o_ref, h_scr, *, scale_tiles):
    j = pl.program_id(1)

    @pl.when(j == 0)
    def _():
        h_scr[...] = _normmod(x_ref[...], g_ref[...], sc_ref[0], sh_ref[0]).astype(BF16)

    def column_tile(sigmoid):
        mc = min(TM_PROJ_CHUNK, h_scr.shape[0])
        for cc in range(h_scr.shape[0] // mc):
            rows = slice(cc * mc, (cc + 1) * mc)
            acc = jnp.dot(h_scr[rows, :], w_ref[0], preferred_element_type=F32)
            if sigmoid:
                acc = jax.nn.sigmoid(acc + e_ref[...])
            else:
                acc = acc * e_ref[...]
            o_ref[rows, :] = acc.astype(o_ref.dtype)

    @pl.when(j < scale_tiles)
    def _():
        column_tile(False)

    @pl.when(j >= scale_tiles)
    def _():
        column_tile(True)


def _norm_proj(x, g, sc, sh, w, epi, seq, *, scale_cols, name):
    n, d = x.shape
    cols = w.shape[1]
    tm = min(TM_PROJ, seq)
    tn = TN_PROJ
    assert cols % tn == 0 and scale_cols % tn == 0
    per_b = seq // tm
    w = w.reshape(d, cols // tn, tn).transpose(1, 0, 2)
    return pl.pallas_call(
        functools.partial(_proj_kernel, scale_tiles=scale_cols // tn),
        grid=(n // tm, cols // tn),
        in_specs=[pl.BlockSpec((tm, d), lambda i, j: (i, 0)),
                  pl.BlockSpec((1, d), lambda i, j: (0, 0)),
                  pl.BlockSpec((1, 1, d), lambda i, j: (i // per_b, 0, 0)),
                  pl.BlockSpec((1, 1, d), lambda i, j: (i // per_b, 0, 0)),
                  pl.BlockSpec((1, d, tn), lambda i, j: (j, 0, 0)),
                  pl.BlockSpec((1, tn), lambda i, j: (0, j))],
        out_specs=pl.BlockSpec((tm, tn), lambda i, j: (i, j)),
        out_shape=jax.ShapeDtypeStruct((n, cols), BF16),
        scratch_shapes=[pltpu.VMEM((tm, d), BF16)],
        compiler_params=_cparams("parallel", "arbitrary"),
        name=name,
    )(x, g, sc, sh, w, epi)


def _mla_prep_kernel(cq_ref, ckv_ref, kr_ref, cos_ref, sin_ref, qn_ref, kvn_ref,
                     wq_ref, wk_ref, wv_ref, q_ref, k_ref, vt_ref, *, scale):
    cos_t, sin_t = cos_ref[...], sin_ref[...]

    def rms(t, gain):
        t = t.astype(F32)
        return (t * lax.rsqrt(jnp.mean(t * t, axis=-1, keepdims=True) + RMS_EPS) * gain).astype(BF16)

    def rope(t):
        return _rope_lanes(t, cos_t, sin_t, MLA_NOPE, MLA_ROPE // 2, LANES)

    qn = rms(cq_ref[...], qn_ref[...])
    kvn = rms(ckv_ref[...], kvn_ref[...])
    q = jnp.dot(qn, wq_ref[...], preferred_element_type=F32)
    k = jnp.dot(kvn, wk_ref[...], preferred_element_type=F32)
    _store_vt(vt_ref, jnp.dot(kvn, wv_ref[...], preferred_element_type=F32))
    k_pe = rope(kr_ref[...].astype(F32))
    for h in range(MLA_HEADS):
        sl = slice(h * LANES, (h + 1) * LANES)
        q_ref[:, sl] = (rope(q[:, sl]) * scale).astype(BF16)
        k_ref[:, sl] = (k[:, sl] + k_pe).astype(BF16)


def _mla_prep(p, cos_t, sin_t, q_norm, kv_norm, wq, wk, wv, batch, seq):
    n = p.shape[0]
    tm = min(TM_PREP, seq)
    hq = MLA_HEADS * LANES
    hv = MLA_HEADS * MLA_V
    scale = float(MLA_NOPE + MLA_ROPE) ** -0.5 * LOG2E
    const = lambda i: (0, 0)
    vt_spec, vt_shape = _vt_out(batch, seq, tm)
    return pl.pallas_call(
        functools.partial(_mla_prep_kernel, scale=scale),
        grid=(n // tm,),
        in_specs=[pl.BlockSpec((tm, MLA_Q_RANK), lambda i: (i, COL_CQ // MLA_Q_RANK)),
                  pl.BlockSpec((tm, MLA_KV_RANK), lambda i: (i, COL_CKV // MLA_KV_RANK)),
                  pl.BlockSpec((tm, LANES), lambda i: (i, COL_KR // LANES)),
                  pl.BlockSpec((tm, LANES), lambda i: (i, 0)),
                  pl.BlockSpec((tm, LANES), lambda i: (i, 0)),
                  pl.BlockSpec((1, MLA_Q_RANK), const),
                  pl.BlockSpec((1, MLA_KV_RANK), const),
                  pl.BlockSpec((MLA_Q_RANK, hq), const),
                  pl.BlockSpec((MLA_KV_RANK, hq), const),
                  pl.BlockSpec((MLA_KV_RANK, hv), const)],
        out_specs=[pl.BlockSpec((tm, hq), lambda i: (i, 0)),
                   pl.BlockSpec((tm, hq), lambda i: (i, 0)),
                   vt_spec],
        out_shape=[jax.ShapeDtypeStruct((n, hq), BF16),
                   jax.ShapeDtypeStruct((n, hq), BF16),
                   vt_shape],
        compiler_params=_cparams("parallel"),
        name="mla_prep",
    )(p, p, p, cos_t, sin_t, q_norm, kv_norm, wq, wk, wv)


def _diff_prep_kernel(q_ref, k_ref, v_ref, cos_ref, sin_ref, qo_ref, ko_ref, vt_ref):
    cos_t, sin_t = cos_ref[...], sin_ref[...]
    for src, dst in ((q_ref, qo_ref), (k_ref, ko_ref)):
        for h in range(DIFF_HEADS):
            sl = slice(h * LANES, (h + 1) * LANES)
            t = src[:, sl].astype(F32)
            dst[:, sl] = _rope_lanes(t, cos_t, sin_t, 0, DIFF_ROT // 2, DIFF_HD).astype(BF16)
    _store_vt(vt_ref, v_ref[...].astype(F32))


def _diff_prep(p, cos_t, sin_t, batch, seq):
    n = p.shape[0]
    tm = min(TM_PREP, seq)
    w = DIFF_HEADS * 2 * DIFF_HD
    vt_spec, vt_shape = _vt_out(batch, seq, tm)
    return pl.pallas_call(
        _diff_prep_kernel,
        grid=(n // tm,),
        in_specs=[pl.BlockSpec((tm, w), lambda i: (i, COL_DQ // w)),
                  pl.BlockSpec((tm, w), lambda i: (i, COL_DK // w)),
                  pl.BlockSpec((tm, w), lambda i: (i, COL_DV // w)),
                  pl.BlockSpec((tm, LANES), lambda i: (i, 0)),
                  pl.BlockSpec((tm, LANES), lambda i: (i, 0))],
        out_specs=[pl.BlockSpec((tm, w), lambda i: (i, 0)),
                   pl.BlockSpec((tm, w), lambda i: (i, 0)),
                   vt_spec],
        out_shape=[jax.ShapeDtypeStruct((n, w), BF16)] * 2 + [vt_shape],
        compiler_params=_cparams("parallel"),
        name="diff_prep",
    )(p, p, p, cos_t, sin_t)


def _fox_prep_kernel(z_ref, k_ref, v_ref, b_ref, tri_ref, sel_ref, ko_ref, vt_ref, carry):
    @pl.when(pl.program_id(1) == 0)
    def _():
        carry[...] = jnp.zeros_like(carry)

    z = z_ref[...].astype(F32) + b_ref[...]
    log_f = jnp.minimum(z, 0.0) - jnp.log(1.0 + jnp.exp(-jnp.abs(z)))
    cum = jnp.dot(tri_ref[...], log_f, precision=lax.Precision.HIGHEST,
                  preferred_element_type=F32) + carry[...]
    ts = cum.shape[0]
    carry[...] = cum[ts - 1:ts, :]
    neg = cum * (-LOG2E)
    hi = neg.astype(BF16)
    r1 = neg - hi.astype(F32)
    mid = r1.astype(BF16)
    lo = (r1 - mid.astype(F32)).astype(BF16)
    parts = jnp.concatenate([hi, mid, lo], axis=1)
    for p in range(PAIRS):
        ko_ref[:, 2 * p * LANES:(2 * p + 1) * LANES] = k_ref[:, p * LANES:(p + 1) * LANES]
        ext = jnp.dot(parts, sel_ref[p], preferred_element_type=F32)
        ko_ref[:, (2 * p + 1) * LANES:(2 * p + 2) * LANES] = ext.astype(BF16)
    _store_vt(vt_ref, v_ref[...].astype(F32))


def _fox_decay_select():
    sel = np.zeros((PAIRS, 3 * LANES, LANES), np.float32)
    for p in range(PAIRS):
        for t in range(2):
            for j in range(3):
                sel[p, j * LANES + 2 * p + t, 3 * t + j] = 1.0
    return jnp.asarray(sel, BF16)


def _fox_prep(p, f_bias, batch, seq):
    n = p.shape[0]
    ts = min(TS_CUM, seq)
    per_b = seq // ts
    w = FOX_HEADS * FOX_HD
    bias = jnp.zeros((1, LANES), F32).at[0, :FOX_HEADS].set(f_bias)
    tri = jnp.tril(jnp.ones((ts, ts), F32))
    row = lambda b, j: b * per_b + j
    return pl.pallas_call(
        _fox_prep_kernel,
        grid=(batch, per_b),
        in_specs=[pl.BlockSpec((ts, LANES), lambda b, j: (row(b, j), COL_ZF // LANES)),
                  pl.BlockSpec((ts, w), lambda b, j: (row(b, j), COL_FOXK // w)),
                  pl.BlockSpec((ts, w), lambda b, j: (row(b, j), COL_FOXV // w)),
                  pl.BlockSpec((1, LANES), lambda b, j: (0, 0)),
                  pl.BlockSpec((ts, ts), lambda b, j: (0, 0)),
                  pl.BlockSpec((PAIRS, 3 * LANES, LANES), lambda b, j: (0, 0, 0))],
        out_specs=[pl.BlockSpec((ts, 2 * w), lambda b, j: (row(b, j), 0)),
                   pl.BlockSpec((1, PAIRS, VT_ROWS, ts), lambda b, j: (b, 0, 0, j))],
        out_shape=[jax.ShapeDtypeStruct((n, 2 * w), BF16),
                   jax.ShapeDtypeStruct((batch, PAIRS, VT_ROWS, seq), BF16)],
        scratch_shapes=[pltpu.VMEM((1, LANES), F32)],
        compiler_params=_cparams("parallel", "arbitrary"),
        name="fox_prep",
    )(p, p, p, bias, tri, _fox_decay_select())


def _attn_kernel(*refs, mode, tq, tk, per_frame, lam_init):
    if mode == "diff":
        q_ref, k_ref, vt_ref, lam_ref, subln_ref, o_ref, q_scr, s_scr, m_scr, acc_scr = refs
    else:
        q_ref, k_ref, vt_ref, o_ref, q_scr, s_scr, m_scr, acc_scr = refs
    w = 2 * tq
    i = pl.program_id(2)
    q = q_ref[...]
    zero = jnp.zeros((tq, LANES), BF16)
    lane = lax.broadcasted_iota(I32, (tq, LANES), 1)
    if mode == "mla":
        q_scr[0:tq, :] = jnp.concatenate([q[:, :LANES], zero], axis=1)
        q_scr[tq:w, :] = jnp.concatenate([zero, q[:, LANES:]], axis=1)
    elif mode == "fox":
        ind_a = jnp.where(lane < 3, 1.0, 0.0).astype(BF16)
        ind_b = jnp.where(lane < 3, 0.0, jnp.where(lane < 6, 1.0, 0.0)).astype(BF16)
        q_scr[0:tq, :] = jnp.concatenate([jnp.where(lane < FOX_HD, q, zero), ind_a], axis=1)
        q_scr[tq:w, :] = jnp.concatenate([jnp.where(lane < FOX_HD, zero, q), ind_b], axis=1)
    else:
        q_scr[0:tq, :] = jnp.where(lane < DIFF_HD, q, zero)
        q_scr[tq:w, :] = jnp.where(lane < DIFF_HD, zero, q)
    m_scr[...] = jnp.full(m_scr.shape, NEG_BIG, F32)
    acc_scr[...] = jnp.zeros(acc_scr.shape, F32)

    def logits(j, slot):
        ks = pl.multiple_of(j * tk, tk)
        s_scr[slot] = lax.dot_general(k_ref[pl.ds(ks, tk), :], q_scr[...], (((1,), (1,)), ((), ())),
                                      preferred_element_type=F32)

    def consume(j, slot, masked):
        ks = pl.multiple_of(j * tk, tk)
        s = s_scr[slot]
        if masked:
            k_pos = ks + lax.broadcasted_iota(I32, (tk, w), 0)
            q_pos = i * tq + (lax.broadcasted_iota(I32, (tk, w), 1) & (tq - 1))
            if per_frame:
                visible = k_pos <= q_pos
            else:
                visible = (k_pos >> CHUNK_SHIFT) <= (q_pos >> CHUNK_SHIFT)
            s = jnp.where(visible, s, NEG_BIG)
        m_prev = m_scr[...]
        m_new = jnp.maximum(m_prev, jnp.max(s, axis=0, keepdims=True))
        alpha = jnp.exp2(m_prev - m_new)
        p = jnp.exp2(s - m_new).astype(BF16)
        acc_scr[...] = alpha * acc_scr[...] + jnp.dot(vt_ref[:, pl.ds(ks, tk)], p,
                                                      preferred_element_type=F32)
        m_scr[...] = m_new

    n_un = (i * tq) // tk
    logits(0, 0)

    def run_tiles(j0, count):
        for t in range(count):
            logits(j0 + t + 1, (t + 1) % 2)
            consume(j0 + t, t % 2, False)

    def quad_body(jq, carry):
        run_tiles(ATT_UNROLL * jq, ATT_UNROLL)
        return carry

    n_quads = n_un // ATT_UNROLL
    lax.fori_loop(0, n_quads, quad_body, 0)
    base = ATT_UNROLL * n_quads
    if ATT_UNROLL == 4:
        @pl.when(n_un - base >= 2)
        def _():
            run_tiles(base, 2)

    @pl.when(n_un % 2 == 1)
    def _():
        logits(n_un, 1)
        consume(n_un - 1, 0, False)
        consume(n_un, 1, True)

    @pl.when(n_un % 2 == 0)
    def _():
        consume(n_un, 0, True)

    oa = acc_scr[0:LANES, 0:tq] / acc_scr[LANES:LANES + 1, 0:tq]
    ob = acc_scr[0:LANES, tq:w] / acc_scr[LANES:LANES + 1, tq:w]
    if mode == "diff":
        lv = lam_ref[...]
        lam = (jnp.exp(jnp.sum(lv[0:1] * lv[1:2], axis=1, keepdims=True))
               - jnp.exp(jnp.sum(lv[2:3] * lv[3:4], axis=1, keepdims=True)) + lam_init)
        o = oa - lam * ob
        o = o * lax.rsqrt(jnp.mean(o * o, axis=0, keepdims=True) + RMS_EPS)
        o = o.T * subln_ref[...] * (1.0 - lam_init)
    else:
        row = lax.broadcasted_iota(I32, (LANES, tq), 0)
        o = jnp.where(row < FOX_HD, oa, ob).T
    o_ref[...] = o.astype(BF16)


def _attention(mode, batch, seq, *, q, k, vt, q_col=0, extra=(), lam_init=0.0):
    n = batch * seq
    tq = min(TQ_ATT, seq // 2)
    tk = min(TK_ATT, seq)
    nq = seq // tq
    kw = LANES if mode == "diff" else 2 * LANES
    qw = 2 * LANES if mode == "mla" else LANES
    qb = q_col // qw
    in_specs = [pl.BlockSpec((tq, qw), lambda b, p, i: (b * nq + i, qb + p)),
                pl.BlockSpec((seq, kw), lambda b, p, i: (b, p)),
                pl.BlockSpec((1, 1, VT_ROWS, seq), lambda b, p, i: (b, p, 0, 0))]
    args = (q, k, vt)
    if mode == "diff":
        in_specs += [pl.BlockSpec((4, DIFF_HD), lambda b, p, i: (0, 0)),
                     pl.BlockSpec((1, LANES), lambda b, p, i: (0, 0))]
        args += tuple(extra)

    def body(*refs):
        refs = list(refs)
        refs[2] = refs[2].at[0, 0]
        _attn_kernel(*refs, mode=mode, tq=tq, tk=tk, per_frame=(mode == "fox"), lam_init=lam_init)

    return pl.pallas_call(
        body,
        grid=(batch, PAIRS, nq),
        in_specs=in_specs,
        out_specs=pl.BlockSpec((tq, LANES), lambda b, p, i: (b * nq + i, p)),
        out_shape=jax.ShapeDtypeStruct((n, PAIRS * LANES), BF16),
        scratch_shapes=[pltpu.VMEM((2 * tq, kw), BF16),
                        pltpu.VMEM((2, tk, 2 * tq), F32),
                        pltpu.VMEM((1, 2 * tq), F32),
                        pltpu.VMEM((VT_ROWS, 2 * tq), F32)],
        compiler_params=_cparams("parallel", "parallel", "arbitrary"),
        name="attn_" + mode,
    )(*args)


def _conv_kernel(b_ref, c_ref, u_ref, ch_ref, uh_ref, w_ref, o_ref, *, per_b):
    u = c_ref[...].astype(F32) * u_ref[...].astype(F32)
    halo = ch_ref[...].astype(F32) * uh_ref[...].astype(F32)
    halo = jnp.where(pl.program_id(0) % per_b == 0, 0.0, halo)
    hr = halo.shape[0]
    row = lax.broadcasted_iota(I32, u.shape, 0)
    u1 = jnp.where(row == 0, halo[hr - 1:hr, :], pltpu.roll(u, 1, 0))
    u2 = jnp.where(row == 0, halo[hr - 2:hr - 1, :],
                   jnp.where(row == 1, halo[hr - 1:hr, :], pltpu.roll(u, 2, 0)))
    w = w_ref[...]
    y = u2 * w[0:1, :] + u1 * w[1:2, :] + u * w[2:3, :]
    o_ref[...] = (b_ref[...].astype(F32) * y).astype(BF16)


def _short_conv(p, conv_w, seq):
    n = p.shape[0]
    ts = min(TM_PREP, seq)
    per_b = seq // ts
    w = CONV_WIDTH
    hr = BF16_SUBLANES
    halo_map = lambda col: (lambda i: (jnp.maximum(i * (ts // hr) - 1, 0), col // w))
    return pl.pallas_call(
        functools.partial(_conv_kernel, per_b=per_b),
        grid=(n // ts,),
        in_specs=[pl.BlockSpec((ts, w), lambda i: (i, COL_CB // w)),
                  pl.BlockSpec((ts, w), lambda i: (i, COL_CC // w)),
                  pl.BlockSpec((ts, w), lambda i: (i, COL_CU // w)),
                  pl.BlockSpec((hr, w), halo_map(COL_CC)),
                  pl.BlockSpec((hr, w), halo_map(COL_CU)),
                  pl.BlockSpec((CONV_TAPS, w), lambda i: (0, 0))],
        out_specs=pl.BlockSpec((ts, w), lambda i: (i, 0)),
        out_shape=jax.ShapeDtypeStruct((n, w), BF16),
        compiler_params=_cparams("parallel"),
        name="short_conv",
    )(p, p, p, p, p, conv_w)


def _merge_kernel(o0, o1, o2, o3, g0, g1, g2, g3, wb_ref, wo_ref, x_ref, gm_ref, out_ref):
    merged = None
    for br, (o_ref, g_ref) in enumerate(((o0, g0), (o1, g1), (o2, g2), (o3, g3))):
        t = jnp.dot(o_ref[...], wb_ref[br], preferred_element_type=F32) * g_ref[...].astype(F32)
        merged = t if merged is None else merged + t
    y = jnp.dot(merged.astype(BF16), wo_ref[...], preferred_element_type=F32)
    out_ref[...] = x_ref[...] + gm_ref[0] * y


def _merge(branches, gates, w_branch, w_out, x, g_m, seq):
    n, d = x.shape
    tm = min(TM_MERGE, seq)
    per_b = seq // tm
    bw = BRANCH_WIDTH
    o_spec = pl.BlockSpec((tm, bw), lambda i: (i, 0))
    assert P_WIDTH % d == 0
    g0 = P_WIDTH // d
    g_specs = [pl.BlockSpec((tm, d), lambda i, br=br: (i, g0 + br)) for br in range(4)]
    single = pl.Buffered(1)
    return pl.pallas_call(
        _merge_kernel,
        grid=(n // tm,),
        in_specs=[o_spec] * 4 + g_specs + [
            pl.BlockSpec((4, bw, d), lambda i: (0, 0, 0), pipeline_mode=single),
            pl.BlockSpec((d, d), lambda i: (0, 0), pipeline_mode=single),
            pl.BlockSpec((tm, d), lambda i: (i, 0)),
            pl.BlockSpec((1, 1, d), lambda i: (i // per_b, 0, 0))],
        out_specs=pl.BlockSpec((tm, d), lambda i: (i, 0)),
        out_shape=jax.ShapeDtypeStruct((n, d), F32),
        compiler_params=_cparams("parallel"),
        name="branch_merge",
    )(*branches, gates, gates, gates, gates, w_branch, w_out, x, g_m)


def _router_kernel(x_ref, g_ref, sc_ref, sh_ref, wr_ref, rb_ref, tri_ref,
                   h_ref, idx_ref, wts_ref, rank_ref, cnt_ref, carry):
    @pl.when(pl.program_id(0) == 0)
    def _():
        carry[...] = jnp.zeros_like(carry)

    h = _normmod(x_ref[...], g_ref[...], sc_ref[0], sh_ref[0])
    h_ref[...] = h
    logits = jnp.dot(h, wr_ref[...], precision=lax.Precision.HIGHEST, preferred_element_type=F32)
    scores = jax.nn.sigmoid(logits)
    work = scores + rb_ref[...]
    tm, ne = scores.shape
    lane = lax.broadcasted_iota(I32, (tm, ne), 1).astype(F32)
    slot = lax.broadcasted_iota(I32, (tm, TOP_K), 1)
    idx = jnp.zeros((tm, TOP_K), F32)
    sel = jnp.zeros((tm, TOP_K), F32)
    hits = []
    chosen = jnp.zeros((tm, ne), F32)
    for kk in range(TOP_K):
        mx = jnp.max(work, axis=1, keepdims=True)
        first = jnp.min(jnp.where(work == mx, lane, float(ne)), axis=1, keepdims=True)
        hit = lane == first
        hits.append(hit)
        chosen = jnp.where(hit, 1.0, chosen)
        sc_k = jnp.sum(jnp.where(hit, scores, 0.0), axis=1, keepdims=True)
        work = jnp.where(hit, -jnp.inf, work)
        idx = jnp.where(slot == kk, first, idx)
        sel = jnp.where(slot == kk, sc_k, sel)
    idx_ref[...] = idx.astype(I32)
    wts_ref[...] = sel / jnp.sum(sel, axis=1, keepdims=True) * ROUTED_SCALE
    before = jnp.dot(tri_ref[...], chosen.astype(BF16), preferred_element_type=F32) + carry[...]
    rank = jnp.zeros((tm, TOP_K), F32)
    for kk in range(TOP_K):
        r_k = jnp.sum(jnp.where(hits[kk], before, 0.0), axis=1, keepdims=True)
        rank = jnp.where(slot == kk, r_k, rank)
    rank_ref[...] = rank.astype(I32)
    total = carry[...] + jnp.sum(chosen, axis=0, keepdims=True)
    carry[...] = total
    cnt_ref[...] = total.astype(I32)


def _router(x, g, sc, sh, w_router, router_bias, seq):
    n, d = x.shape
    tm = min(TM_ROUTE, seq)
    per_b = seq // tm
    ne = w_router.shape[1]
    tri = jnp.tril(jnp.ones((tm, tm), F32), k=-1).astype(BF16)
    return pl.pallas_call(
        _router_kernel,
        grid=(n // tm,),
        in_specs=[pl.BlockSpec((tm, d), lambda i: (i, 0)),
                  pl.BlockSpec((1, d), lambda i: (0, 0)),
                  pl.BlockSpec((1, 1, d), lambda i: (i // per_b, 0, 0)),
                  pl.BlockSpec((1, 1, d), lambda i: (i // per_b, 0, 0)),
                  pl.BlockSpec((d, ne), lambda i: (0, 0)),
                  pl.BlockSpec((1, ne), lambda i: (0, 0)),
                  pl.BlockSpec((tm, tm), lambda i: (0, 0))],
        out_specs=[pl.BlockSpec((tm, d), lambda i: (i, 0)),
                   pl.BlockSpec((tm, TOP_K), lambda i: (i, 0)),
                   pl.BlockSpec((tm, TOP_K), lambda i: (i, 0)),
                   pl.BlockSpec((tm, TOP_K), lambda i: (i, 0)),
                   pl.BlockSpec((1, ne), lambda i: (0, 0))],
        out_shape=[jax.ShapeDtypeStruct((n, d), F32),
                   jax.ShapeDtypeStruct((n, TOP_K), I32),
                   jax.ShapeDtypeStruct((n, TOP_K), F32),
                   jax.ShapeDtypeStruct((n, TOP_K), I32),
                   jax.ShapeDtypeStruct((1, ne), I32)],
        scratch_shapes=[pltpu.VMEM((1, ne), F32)],
        compiler_params=_cparams("arbitrary"),
        name="moe_router",
    )(x, g, sc, sh, w_router, router_bias, tri)


GATHER_BUFS = 3


def _gather_step(i, slot, n_steps, idx_hbm, idx_smem, idx_sem, data_hbm, buf, data_sem, n_rows):
    assert n_steps >= GATHER_BUFS

    def idx_copy(step, slot):
        return pltpu.make_async_copy(idx_hbm.at[step], idx_smem.at[pl.ds(slot, 1)], idx_sem.at[slot])

    def rows_wait(slot):
        pltpu.make_async_copy(data_hbm.at[pl.ds(0, n_rows)], buf.at[slot], data_sem.at[slot]).wait()

    def issue_rows(slot):
        for r in range(n_rows):
            row = idx_smem[slot, r]
            pltpu.make_async_copy(data_hbm.at[pl.ds(row, 1)], buf.at[slot, pl.ds(r, 1)],
                                  data_sem.at[slot]).start(priority=r % 2)

    ahead = (slot + 2) % GATHER_BUFS
    last = n_steps - 1

    if slot == 0:
        @pl.when(i == 0)
        def _():
            for t in range(2):
                idx_copy(t, t).start()
                idx_copy(t, t).wait()
                issue_rows(t)
            idx_copy(2, 2).start()

    rows_wait(slot)
    idx_copy(jnp.minimum(i + 2, last), ahead).wait()

    def prefetch():
        issue_rows(ahead)
        idx_copy(jnp.minimum(i + 3, last), slot).start()

    def drain():
        @pl.when(i == last)
        def _():
            rows_wait((slot + 1) % GATHER_BUFS)
            rows_wait(ahead)
            idx_copy(last, slot).wait()

    return prefetch, drain


def _for_each_parity(i, step):
    for slot in range(GATHER_BUFS):
        @pl.when(i % GATHER_BUFS == slot)
        def _():
            step(slot)


def _expert_kernel(te_ref, src_hbm, h_hbm, wg_ref, wu_ref, wd_ref, y_ref,
                   idx_smem, idx_sem, xbuf, x_sem, wg_s, wu_s, wd_s, xs_s, *, n_steps, tm):
    i = pl.program_id(0)
    new_expert = jnp.logical_or(i == 0, te_ref[i] != te_ref[jnp.maximum(i - 1, 0)])

    @pl.when(new_expert)
    def _():
        wg_s[...] = wg_ref[0, 0].astype(BF16)
        wu_s[...] = wu_ref[0, 0].astype(BF16)
        wd_s[...] = wd_ref[0, 0].astype(BF16)

    def step(slot):
        prefetch, drain = _gather_step(i, slot, n_steps, src_hbm, idx_smem, idx_sem, h_hbm, xbuf, x_sem, tm)
        xs_s[...] = xbuf[slot].astype(BF16)
        prefetch()
        x = xs_s[...]
        a = jnp.dot(x, wg_s[...], preferred_element_type=F32)
        u = jnp.dot(x, wu_s[...], preferred_element_type=F32)
        hid = (a * jax.nn.sigmoid(a)) * u
        y_ref[...] = jnp.dot(hid.astype(BF16), wd_s[...], preferred_element_type=F32)
        drain()

    _for_each_parity(i, step)


def _expert_ffn(layer, tile_expert, src, h, w_eg, w_eu, w_ed):
    n_steps, _, tm = src.shape
    d = w_eg.shape[2]
    ff = w_eg.shape[3]
    grid_spec = pltpu.PrefetchScalarGridSpec(
        num_scalar_prefetch=1,
        grid=(n_steps,),
        in_specs=[pl.BlockSpec(memory_space=pl.ANY),
                  pl.BlockSpec(memory_space=pl.ANY),
                  pl.BlockSpec((1, 1, d, ff), lambda i, te: (layer, te[i], 0, 0)),
                  pl.BlockSpec((1, 1, d, ff), lambda i, te: (layer, te[i], 0, 0)),
                  pl.BlockSpec((1, 1, ff, d), lambda i, te: (layer, te[i], 0, 0))],
        out_specs=pl.BlockSpec((tm, d), lambda i, te: (i, 0)),
        scratch_shapes=[pltpu.SMEM((GATHER_BUFS, tm), I32),
                        pltpu.SemaphoreType.DMA((GATHER_BUFS,)),
                        pltpu.VMEM((GATHER_BUFS, tm, d), F32),
                        pltpu.SemaphoreType.DMA((GATHER_BUFS,)),
                        pltpu.VMEM((d, ff), BF16),
                        pltpu.VMEM((d, ff), BF16),
                        pltpu.VMEM((ff, d), BF16),
                        pltpu.VMEM((tm, d), BF16)],
    )
    return pl.pallas_call(
        functools.partial(_expert_kernel, n_steps=n_steps, tm=tm),
        grid_spec=grid_spec,
        out_shape=jax.ShapeDtypeStruct((n_steps * tm, d), F32),
        compiler_params=_cparams("arbitrary"),
        name="moe_experts",
    )(tile_expert, src, h, w_eg, w_eu, w_ed)


def _shared_kernel(h_ref, wg_ref, wu_ref, wd_ref, o_ref):
    h = h_ref[...].astype(BF16)
    a = jnp.dot(h, wg_ref[...], preferred_element_type=F32)
    u = jnp.dot(h, wu_ref[...], preferred_element_type=F32)
    hid = ((a * jax.nn.sigmoid(a)) * u).astype(BF16)
    o_ref[...] = jnp.dot(hid, wd_ref[...], preferred_element_type=F32)


def _shared_ffn(h, wg, wu, wd):
    n, d = h.shape
    ff = wg.shape[1]
    tm = min(TM_FFN, n)
    return pl.pallas_call(
        _shared_kernel,
        grid=(n // tm,),
        in_specs=[pl.BlockSpec((tm, d), lambda i: (i, 0)),
                  pl.BlockSpec((d, ff), lambda i: (0, 0)),
                  pl.BlockSpec((d, ff), lambda i: (0, 0)),
                  pl.BlockSpec((ff, d), lambda i: (0, 0))],
        out_specs=pl.BlockSpec((tm, d), lambda i: (i, 0)),
        out_shape=jax.ShapeDtypeStruct((n, d), F32),
        compiler_params=_cparams("parallel"),
        name="moe_shared",
    )(h, wg, wu, wd)


def _combine_kernel(dest_hbm, y_hbm, wts_ref, sh_ref, x_ref, gf_ref, fn_ref, o_ref,
                    idx_smem, idx_sem, ybuf, y_sem, *, n_steps, tc, final_norm):
    i = pl.program_id(0)
    rows = 8
    d = x_ref.shape[1]

    def step(slot):
        prefetch, drain = _gather_step(i, slot, n_steps, dest_hbm, idx_smem, idx_sem, y_hbm, ybuf, y_sem,
                                       TOP_K * tc)
        prefetch()
        for c in range(tc // rows):
            r0 = c * rows
            wts = wts_ref[r0:r0 + rows, :]
            wb = [jnp.broadcast_to(wts[:, kk:kk + 1], (rows, LANES)) for kk in range(TOP_K)]
            outs = []
            for s in range(d // LANES):
                sl = slice(s * LANES, (s + 1) * LANES)
                acc = sh_ref[r0:r0 + rows, sl]
                for kk in range(TOP_K):
                    acc = acc + wb[kk] * ybuf[slot, kk * tc + r0:kk * tc + r0 + rows, sl]
                outs.append(x_ref[r0:r0 + rows, sl] + gf_ref[0, :, sl] * acc)
            if final_norm:
                ssq = outs[0] * outs[0]
                for o in outs[1:]:
                    ssq = ssq + o * o
                inv = lax.rsqrt(jnp.sum(ssq, axis=1, keepdims=True) * (1.0 / d) + RMS_EPS)
                outs = [o * inv * fn_ref[:, s * LANES:(s + 1) * LANES] for s, o in enumerate(outs)]
            for s, o in enumerate(outs):
                o_ref[r0:r0 + rows, s * LANES:(s + 1) * LANES] = o
        drain()

    _for_each_parity(i, step)


def _combine(dest, y, wts, shared, x, g_f, final_gain, seq, *, final_norm):
    n, d = x.shape
    n_steps, _, rows = dest.shape
    tc = rows // TOP_K
    per_b = seq // tc
    return pl.pallas_call(
        functools.partial(_combine_kernel, n_steps=n_steps, tc=tc, final_norm=final_norm),
        grid=(n_steps,),
        in_specs=[pl.BlockSpec(memory_space=pl.ANY),
                  pl.BlockSpec(memory_space=pl.ANY),
                  pl.BlockSpec((tc, TOP_K), lambda i: (i, 0)),
                  pl.BlockSpec((tc, d), lambda i: (i, 0)),
                  pl.BlockSpec((tc, d), lambda i: (i, 0)),
                  pl.BlockSpec((1, 1, d), lambda i: (i // per_b, 0, 0)),
                  pl.BlockSpec((1, d), lambda i: (0, 0))],
        out_specs=pl.BlockSpec((tc, d), lambda i: (i, 0)),
        out_shape=jax.ShapeDtypeStruct((n, d), F32),
        scratch_shapes=[pltpu.SMEM((GATHER_BUFS, rows), I32),
                        pltpu.SemaphoreType.DMA((GATHER_BUFS,)),
                        pltpu.VMEM((GATHER_BUFS, rows, d), F32),
                        pltpu.SemaphoreType.DMA((GATHER_BUFS,))],
        compiler_params=_cparams("arbitrary"),
        name="moe_combine",
    )(dest, y, wts, shared, x, g_f, final_gain)


def _rope_tables(positions):
    pos = positions.reshape(-1).astype(F32)
    n = pos.shape[0]

    def cos_sin(half):
        inv_freq = ROPE_THETA ** (-jnp.arange(half, dtype=F32) / half)
        ang = pos[:, None] * inv_freq
        return jnp.cos(ang), jnp.sin(ang)

    cm, sm = cos_sin(MLA_ROPE // 2)
    tail = LANES - MLA_NOPE - MLA_ROPE
    mla_cos = jnp.concatenate([jnp.ones((n, MLA_NOPE), F32), cm, cm, jnp.ones((n, tail), F32)], axis=1)
    mla_sin = jnp.concatenate([jnp.zeros((n, MLA_NOPE), F32), sm, sm, jnp.zeros((n, tail), F32)], axis=1)
    cd, sd = cos_sin(DIFF_ROT // 2)
    rest = DIFF_HD - DIFF_ROT
    dc = jnp.concatenate([cd, cd, jnp.ones((n, rest), F32)], axis=1)
    ds = jnp.concatenate([sd, sd, jnp.zeros((n, rest), F32)], axis=1)
    return mla_cos, mla_sin, jnp.concatenate([dc, dc], axis=1), jnp.concatenate([ds, ds], axis=1)


def _in_proj_weight(w_in):
    d = w_in.shape[0]
    sizes = (MLA_Q_RANK, MLA_KV_RANK, MLA_ROPE, 512, 512, 512, FOX_HEADS, 512, 512, 512, 512, 512, 512)
    parts, off = [], 0
    for s in sizes:
        parts.append(w_in[:, off:off + s])
        off += s
    a_cq, a_ckv, a_kr, b_q, b_k, b_v, b_zf, c_b, c_c, c_u, d_q, d_k, d_v = parts
    zeros = lambda w: jnp.zeros((d, w), w_in.dtype)
    cols = [b_q, b_k, b_v, c_b, c_c, c_u, d_q, d_k, d_v, a_cq,
            zeros(MLA_NOPE), a_kr, zeros(LANES - MLA_NOPE - MLA_ROPE),
            a_ckv, b_zf, zeros(LANES - FOX_HEADS), zeros(P_WIDTH - COL_ZF - LANES)]
    return jnp.concatenate(cols, axis=1).astype(BF16)


def _in_proj_scale():
    s = np.ones((1, P_WIDTH), np.float32)
    qs = FOX_HD ** -0.5 * LOG2E
    s[0, COL_FOXQ:COL_FOXQ + FOX_HEADS * FOX_HD] = qs
    s[0, COL_DQ:COL_DQ + DIFF_HEADS * 2 * DIFF_HD] = qs
    return jnp.asarray(s)


def _dispatch_plan(idx, rank, counts, tm):
    n = idx.shape[0]
    counts = counts.reshape(-1)
    padded = (counts + tm - 1) // tm * tm
    ends = jnp.cumsum(padded)
    starts = ends - padded
    dest = (starts[idx] + rank).astype(I32)
    n_steps = (n * TOP_K) // tm + N_EXPERTS
    tok = jnp.repeat(jnp.arange(n, dtype=I32), TOP_K)
    src = jnp.zeros((n_steps * tm,), I32).at[dest.reshape(-1)].set(
        tok, unique_indices=True, indices_are_sorted=False, mode="promise_in_bounds")
    tile_start = jnp.arange(n_steps, dtype=I32) * tm
    tile_expert = jnp.sum((ends[None, :] <= tile_start[:, None]).astype(I32), axis=1)
    tile_expert = jnp.minimum(tile_expert, N_EXPERTS - 1)
    return dest, src.reshape(n_steps, 1, tm), tile_expert


def kernel(x, c, positions, ada_w, ada_b, norm_mix, norm_ffn, w_in, mla_q_norm, mla_w_uq, mla_kv_norm,
           mla_w_ukv, fox_f_bias, conv_w, diff_lam_q1, diff_lam_k1, diff_lam_q2, diff_lam_k2, diff_subln,
           w_branch, w_gate, b_gate, w_out, w_router, router_bias, w_exp_gate, w_exp_up, w_exp_down,
           w_sh_gate, w_sh_up, w_sh_down, final_norm):
    batch, seq, d = x.shape
    n = batch * seq
    depth = ada_w.shape[0]
    xf = x.reshape(n, d)

    mod = _modulation(c, ada_w, ada_b)
    mla_cos, mla_sin, diff_cos, diff_sin = _rope_tables(positions)
    p_scale = _in_proj_scale()

    for l in range(depth):
        lam_init = 0.8 - 0.6 * math.exp(-0.3 * l)
        mods = mod[l, :batch].reshape(batch, 6, 1, d)
        sh_m, sc_m, g_m, sh_f, sc_f, g_f = (mods[:, t] for t in range(6))

        w_g = w_gate[l].transpose(1, 0, 2).reshape(d, 4 * d).astype(BF16)
        w_all = jnp.concatenate([_in_proj_weight(w_in[l]), w_g], axis=1)
        epi = jnp.concatenate([p_scale, b_gate[l].reshape(1, 4 * d)], axis=1)
        p = _norm_proj(xf, norm_mix[l][None], sc_m, sh_m, w_all, epi, seq, scale_cols=P_WIDTH,
                       name="in_proj_gates")
        gates = p

        wq = jnp.pad(mla_w_uq[l].reshape(MLA_Q_RANK, MLA_HEADS, MLA_NOPE + MLA_ROPE),
                     ((0, 0), (0, 0), (0, LANES - MLA_NOPE - MLA_ROPE)))
        wq = wq.reshape(MLA_Q_RANK, MLA_HEADS * LANES).astype(BF16)
        wkv = mla_w_ukv[l].reshape(MLA_KV_RANK, MLA_HEADS, MLA_NOPE + MLA_V)
        wk = jnp.pad(wkv[:, :, :MLA_NOPE], ((0, 0), (0, 0), (0, LANES - MLA_NOPE)))
        wk = wk.reshape(MLA_KV_RANK, MLA_HEADS * LANES).astype(BF16)
        wv = wkv[:, :, MLA_NOPE:].reshape(MLA_KV_RANK, MLA_HEADS * MLA_V).astype(BF16)
        q_a, k_a, vt_a = _mla_prep(p, mla_cos, mla_sin, mla_q_norm[l][None], mla_kv_norm[l][None],
                                   wq, wk, wv, batch, seq)
        o_mla = _attention("mla", batch, seq, q=q_a, k=k_a, vt=vt_a)

        k_f, vt_f = _fox_prep(p, fox_f_bias[l], batch, seq)
        o_fox = _attention("fox", batch, seq, q=p, k=k_f, vt=vt_f, q_col=COL_FOXQ)

        o_conv = _short_conv(p, conv_w[l], seq)

        q_d, k_d, vt_d = _diff_prep(p, diff_cos, diff_sin, batch, seq)
        lam_vecs = jnp.stack([diff_lam_q1[l], diff_lam_k1[l], diff_lam_q2[l], diff_lam_k2[l]])
        o_diff = _attention("diff", batch, seq, q=q_d, k=k_d, vt=vt_d,
                            extra=(lam_vecs, diff_subln[l][None]), lam_init=lam_init)

        xf = _merge((o_mla, o_fox, o_conv, o_diff), gates, w_branch[l].astype(BF16),
                    w_out[l].astype(BF16), xf, g_m, seq)

        h, idx, wts, rank, counts = _router(xf, norm_ffn[l][None], sc_f, sh_f, w_router[l],
                                            router_bias[l][None], seq)
        dest, src, tile_expert = _dispatch_plan(idx, rank, counts, min(TM_EXP, n))
        y = _expert_ffn(l, tile_expert, src, h, w_exp_gate, w_exp_up, w_exp_down)
        shared = _shared_ffn(h, w_sh_gate[l].astype(BF16), w_sh_up[l].astype(BF16),
                             w_sh_down[l].astype(BF16))
        tc = min(TC_COMB, seq)
        dest_t = dest.reshape(n // tc, tc, TOP_K).transpose(0, 2, 1).reshape(n // tc, 1, TOP_K * tc)
        xf = _combine(dest_t, y, wts, shared, xf, g_f, final_norm[None], seq, final_norm=(l == depth - 1))

    return xf.reshape(batch, seq, d)
```

```python
import functools
import math

import numpy as np
import jax
import jax.numpy as jnp
from jax import lax
from jax.experimental import pallas as pl
from jax.experimental.pallas import tpu as pltpu

F32 = jnp.float32
BF16 = jnp.bfloat16
I32 = jnp.int32

CHUNK = 64
CHUNK_SHIFT = 6
RMS_EPS = 1e-6
ROPE_THETA = 500000.0
MLA_HEADS, MLA_Q_RANK, MLA_KV_RANK, MLA_NOPE, MLA_ROPE, MLA_V = 8, 384, 256, 64, 32, 64
FOX_HEADS, FOX_HD = 8, 64
CONV_WIDTH, CONV_TAPS = 512, 3
DIFF_HEADS, DIFF_HD, DIFF_ROT = 4, 64, 16
N_EXPERTS, TOP_K, ROUTED_SCALE = 64, 8, 2.5
BRANCH_WIDTH = 512
LOG2E = 1.4426950408889634

LANES = 128
BF16_SUBLANES = 16
V7X_VMEM_BYTES = 64 * 1024 * 1024
VMEM_LIMIT = V7X_VMEM_BYTES - 8 * 1024 * 1024
NEG_BIG = -1e30

TM_PROJ = 1024
TM_PROJ_CHUNK = 256
TN_PROJ = 1024
TM_PREP = 512
TS_CUM = 256
TQ_ATT = 512
TK_ATT = 512
ATT_UNROLL = 8
TM_MERGE = 256
TM_ROUTE = 256
TM_EXP = 256
TM_FFN = 512
TC_COMB = 64

PAIRS = 4
VT_ROWS = LANES + BF16_SUBLANES

COL_FOXQ, COL_FOXK, COL_FOXV = 0, 512, 1024
COL_CB, COL_CC, COL_CU = 1536, 2048, 2560
COL_DQ, COL_DK, COL_DV = 3072, 3584, 4096
COL_CQ = 4608
COL_KR = 4992
COL_CKV = 5120
COL_ZF = 5376
P_WIDTH = 6144


def _cparams(*sem):
    return pltpu.CompilerParams(dimension_semantics=sem, vmem_limit_bytes=VMEM_LIMIT)


def _normmod(x, g, sc, sh):
    ms = jnp.mean(x * x, axis=-1, keepdims=True)
    return (x * lax.rsqrt(ms + RMS_EPS)) * g * (1.0 + sc) + sh


def _rope_lanes(x, cos_t, sin_t, lo, half, period):
    width = x.shape[1]
    lane = lax.broadcasted_iota(I32, x.shape, 1) & (period - 1)
    x_up = pltpu.roll(x, width - half, 1)
    x_dn = pltpu.roll(x, half, 1)
    rot = jnp.where(lane < lo, 0.0,
                    jnp.where(lane < lo + half, -x_up,
                              jnp.where(lane < lo + 2 * half, x_dn, 0.0)))
    return x * cos_t + rot * sin_t


def _store_vt(vt_ref, v):
    tm = v.shape[0]
    ones = jnp.ones((VT_ROWS - LANES, tm), BF16)
    for p in range(PAIRS):
        vt_ref[0, p, 0:LANES, :] = v[:, p * LANES:(p + 1) * LANES].T.astype(BF16)
        vt_ref[0, p, LANES:VT_ROWS, :] = ones


def _vt_out(batch, seq, tm):
    per_b = seq // tm
    spec = pl.BlockSpec((1, PAIRS, VT_ROWS, tm), lambda i: (i // per_b, 0, 0, i % per_b))
    return spec, jax.ShapeDtypeStruct((batch, PAIRS, VT_ROWS, seq), BF16)


def _mod_kernel(c_ref, w_ref, b_ref, o_ref):
    c = c_ref[...]
    ca = (c * jax.nn.sigmoid(c)).astype(BF16)
    o_ref[0] = jnp.dot(ca, w_ref[0].astype(BF16), preferred_element_type=F32) + b_ref[0]


def _modulation(c, ada_w, ada_b):
    n_layers, d, d6 = ada_w.shape
    b = c.shape[0]
    rows = 8
    cp = jnp.zeros((rows, d), F32).at[:b].set(c)
    tn = 1024
    return pl.pallas_call(
        _mod_kernel,
        grid=(n_layers, d6 // tn),
        in_specs=[pl.BlockSpec((rows, d), lambda l, j: (0, 0)),
                  pl.BlockSpec((1, d, tn), lambda l, j: (l, 0, j)),
                  pl.BlockSpec((1, 1, tn), lambda l, j: (l, 0, j))],
        out_specs=pl.BlockSpec((1, rows, tn), lambda l, j: (l, 0, j)),
        out_shape=jax.ShapeDtypeStruct((n_layers, rows, d6), F32),
        compiler_params=_cparams("parallel", "parallel"),
        name="adaln_modulation",
    )(cp, ada_w, ada_b.reshape(n_layers, 1, d6))


def _proj_kernel(x_ref, g_ref, sc_ref, sh_ref, w_ref, e_ref, o_ref, h_scr, *, scale_tiles):
    j = pl.program_id(1)

    @pl.when(j == 0)
    def _():
        h_scr[...] = _normmod(x_ref[...], g_ref[...], sc_ref[0], sh_ref[0]).astype(BF16)

    def column_tile(sigmoid):
        mc = min(TM_PROJ_CHUNK, h_scr.shape[0])
        for cc in range(h_scr.shape[0] // mc):
            rows = slice(cc * mc, (cc + 1) * mc)
            acc = jnp.dot(h_scr[rows, :], w_ref[0], preferred_element_type=F32)
            if sigmoid:
                acc = jax.nn.sigmoid(acc + e_ref[...])
            else:
                acc = acc * e_ref[...]
            o_ref[rows, :] = acc.astype(o_ref.dtype)

    @pl.when(j < scale_tiles)
    def _():
        column_tile(False)

    @pl.when(j >= scale_tiles)
    def _():
        column_tile(True)


def _norm_proj(x, g, sc, sh, w, epi, seq, *, scale_cols, name):
    n, d = x.shape
    cols = w.shape[1]
    tm = min(TM_PROJ, seq)
    tn = TN_PROJ
    assert cols % tn == 0 and scale_cols % tn == 0
    per_b = seq // tm
    w = w.reshape(d, cols // tn, tn).transpose(1, 0, 2)
    return pl.pallas_call(
        functools.partial(_proj_kernel, scale_tiles=scale_cols // tn),
        grid=(n // tm, cols // tn),
        in_specs=[pl.BlockSpec((tm, d), lambda i, j: (i, 0)),
                  pl.BlockSpec((1, d), lambda i, j: (0, 0)),
                  pl.BlockSpec((1, 1, d), lambda i, j: (i // per_b, 0, 0)),
                  pl.BlockSpec((1, 1, d), lambda i, j: (i // per_b, 0, 0)),
                  pl.BlockSpec((1, d, tn), lambda i, j: (j, 0, 0)),
                  pl.BlockSpec((1, tn), lambda i, j: (0, j))],
        out_specs=pl.BlockSpec((tm, tn), lambda i, j: (i, j)),
        out_shape=jax.ShapeDtypeStruct((n, cols), BF16),
        scratch_shapes=[pltpu.VMEM((tm, d), BF16)],
        compiler_params=_cparams("parallel", "arbitrary"),
        name=name,
    )(x, g, sc, sh, w, epi)


def _mla_prep_kernel(cq_ref, ckv_ref, kr_ref, cos_ref, sin_ref, qn_ref, kvn_ref,
                     wq_ref, wk_ref, wv_ref, q_ref, k_ref, vt_ref, *, scale):
    cos_t, sin_t = cos_ref[...], sin_ref[...]

    def rms(t, gain):
        t = t.astype(F32)
        return (t * lax.rsqrt(jnp.mean(t * t, axis=-1, keepdims=True) + RMS_EPS) * gain).astype(BF16)

    def rope(t):
        return _rope_lanes(t, cos_t, sin_t, MLA_NOPE, MLA_ROPE // 2, LANES)

    qn = rms(cq_ref[...], qn_ref[...])
    kvn = rms(ckv_ref[...], kvn_ref[...])
    q = jnp.dot(qn, wq_ref[...], preferred_element_type=F32)
    k = jnp.dot(kvn, wk_ref[...], preferred_element_type=F32)
    _store_vt(vt_ref, jnp.dot(kvn, wv_ref[...], preferred_element_type=F32))
    k_pe = rope(kr_ref[...].astype(F32))
    for h in range(MLA_HEADS):
        sl = slice(h * LANES, (h + 1) * LANES)
        q_ref[:, sl] = (rope(q[:, sl]) * scale).astype(BF16)
        k_ref[:, sl] = (k[:, sl] + k_pe).astype(BF16)


def _mla_prep(p, cos_t, sin_t, q_norm, kv_norm, wq, wk, wv, batch, seq):
    n = p.shape[0]
    tm = min(TM_PREP, seq)
    hq = MLA_HEADS * LANES
    hv = MLA_HEADS * MLA_V
    scale = float(MLA_NOPE + MLA_ROPE) ** -0.5 * LOG2E
    const = lambda i: (0, 0)
    vt_spec, vt_shape = _vt_out(batch, seq, tm)
    return pl.pallas_call(
        functools.partial(_mla_prep_kernel, scale=scale),
        grid=(n // tm,),
        in_specs=[pl.BlockSpec((tm, MLA_Q_RANK), lambda i: (i, COL_CQ // MLA_Q_RANK)),
                  pl.BlockSpec((tm, MLA_KV_RANK), lambda i: (i, COL_CKV // MLA_KV_RANK)),
                  pl.BlockSpec((tm, LANES), lambda i: (i, COL_KR // LANES)),
                  pl.BlockSpec((tm, LANES), lambda i: (i, 0)),
                  pl.BlockSpec((tm, LANES), lambda i: (i, 0)),
                  pl.BlockSpec((1, MLA_Q_RANK), const),
                  pl.BlockSpec((1, MLA_KV_RANK), const),
                  pl.BlockSpec((MLA_Q_RANK, hq), const),
                  pl.BlockSpec((MLA_KV_RANK, hq), const),
                  pl.BlockSpec((MLA_KV_RANK, hv), const)],
        out_specs=[pl.BlockSpec((tm, hq), lambda i: (i, 0)),
                   pl.BlockSpec((tm, hq), lambda i: (i, 0)),
                   vt_spec],
        out_shape=[jax.ShapeDtypeStruct((n, hq), BF16),
                   jax.ShapeDtypeStruct((n, hq), BF16),
                   vt_shape],
        compiler_params=_cparams("parallel"),
        name="mla_prep",
    )(p, p, p, cos_t, sin_t, q_norm, kv_norm, wq, wk, wv)


def _diff_prep_kernel(q_ref, k_ref, v_ref, cos_ref, sin_ref, qo_ref, ko_ref, vt_ref):
    cos_t, sin_t = cos_ref[...], sin_ref[...]
    for src, dst in ((q_ref, qo_ref), (k_ref, ko_ref)):
        for h in range(DIFF_HEADS):
            sl = slice(h * LANES, (h + 1) * LANES)
            t = src[:, sl].astype(F32)
            dst[:, sl] = _rope_lanes(t, cos_t, sin_t, 0, DIFF_ROT // 2, DIFF_HD).astype(BF16)
    _store_vt(vt_ref, v_ref[...].astype(F32))


def _diff_prep(p, cos_t, sin_t, batch, seq):
    n = p.shape[0]
    tm = min(TM_PREP, seq)
    w = DIFF_HEADS * 2 * DIFF_HD
    vt_spec, vt_shape = _vt_out(batch, seq, tm)
    return pl.pallas_call(
        _diff_prep_kernel,
        grid=(n // tm,),
        in_specs=[pl.BlockSpec((tm, w), lambda i: (i, COL_DQ // w)),
                  pl.BlockSpec((tm, w), lambda i: (i, COL_DK // w)),
                  pl.BlockSpec((tm, w), lambda i: (i, COL_DV // w)),
                  pl.BlockSpec((tm, LANES), lambda i: (i, 0)),
                  pl.BlockSpec((tm, LANES), lambda i: (i, 0))],
        out_specs=[pl.BlockSpec((tm, w), lambda i: (i, 0)),
                   pl.BlockSpec((tm, w), lambda i: (i, 0)),
                   vt_spec],
        out_shape=[jax.ShapeDtypeStruct((n, w), BF16)] * 2 + [vt_shape],
        compiler_params=_cparams("parallel"),
        name="diff_prep",
    )(p, p, p, cos_t, sin_t)


def _fox_prep_kernel(z_ref, k_ref, v_ref, b_ref, tri_ref, sel_ref, ko_ref, vt_ref, carry):
    @pl.when(pl.program_id(1) == 0)
    def _():
        carry[...] = jnp.zeros_like(carry)

    z = z_ref[...].astype(F32) + b_ref[...]
    log_f = jnp.minimum(z, 0.0) - jnp.log(1.0 + jnp.exp(-jnp.abs(z)))
    cum = jnp.dot(tri_ref[...], log_f, precision=lax.Precision.HIGHEST,
                  preferred_element_type=F32) + carry[...]
    ts = cum.shape[0]
    carry[...] = cum[ts - 1:ts, :]
    neg = cum * (-LOG2E)
    hi = neg.astype(BF16)
    r1 = neg - hi.astype(F32)
    mid = r1.astype(BF16)
    lo = (r1 - mid.astype(F32)).astype(BF16)
    parts = jnp.concatenate([hi, mid, lo], axis=1)
    for p in range(PAIRS):
        ko_ref[:, 2 * p * LANES:(2 * p + 1) * LANES] = k_ref[:, p * LANES:(p + 1) * LANES]
        ext = jnp.dot(parts, sel_ref[p], preferred_element_type=F32)
        ko_ref[:, (2 * p + 1) * LANES:(2 * p + 2) * LANES] = ext.astype(BF16)
    _store_vt(vt_ref, v_ref[...].astype(F32))


def _fox_decay_select():
    sel = np.zeros((PAIRS, 3 * LANES, LANES), np.float32)
    for p in range(PAIRS):
        for t in range(2):
            for j in range(3):
                sel[p, j * LANES + 2 * p + t, 3 * t + j] = 1.0
    return jnp.asarray(sel, BF16)


def _fox_prep(p, f_bias, batch, seq):
    n = p.shape[0]
    ts = min(TS_CUM, seq)
    per_b = seq // ts
    w = FOX_HEADS * FOX_HD
    bias = jnp.zeros((1, LANES), F32).at[0, :FOX_HEADS].set(f_bias)
    tri = jnp.tril(jnp.ones((ts, ts), F32))
    row = lambda b, j: b * per_b + j
    return pl.pallas_call(
        _fox_prep_kernel,
        grid=(batch, per_b),
        in_specs=[pl.BlockSpec((ts, LANES), lambda b, j: (row(b, j), COL_ZF // LANES)),
                  pl.BlockSpec((ts, w), lambda b, j: (row(b, j), COL_FOXK // w)),
                  pl.BlockSpec((ts, w), lambda b, j: (row(b, j), COL_FOXV // w)),
                  pl.BlockSpec((1, LANES), lambda b, j: (0, 0)),
                  pl.BlockSpec((ts, ts), lambda b, j: (0, 0)),
                  pl.BlockSpec((PAIRS, 3 * LANES, LANES), lambda b, j: (0, 0, 0))],
        out_specs=[pl.BlockSpec((ts, 2 * w), lambda b, j: (row(b, j), 0)),
                   pl.BlockSpec((1, PAIRS, VT_ROWS, ts), lambda b, j: (b, 0, 0, j))],
        out_shape=[jax.ShapeDtypeStruct((n, 2 * w), BF16),
                   jax.ShapeDtypeStruct((batch, PAIRS, VT_ROWS, seq), BF16)],
        scratch_shapes=[pltpu.VMEM((1, LANES), F32)],
        compiler_params=_cparams("parallel", "arbitrary"),
        name="fox_prep",
    )(p, p, p, bias, tri, _fox_decay_select())


def _attn_kernel(*refs, mode, tq, tk, per_frame, lam_init):
    if mode == "diff":
        q_ref, k_ref, vt_ref, lam_ref, subln_ref, o_ref, q_scr, s_scr, m_scr, acc_scr = refs
    else:
        q_ref, k_ref, vt_ref, o_ref, q_scr, s_scr, m_scr, acc_scr = refs
    w = 2 * tq
    i = pl.program_id(2)
    q = q_ref[...]
    zero = jnp.zeros((tq, LANES), BF16)
    lane = lax.broadcasted_iota(I32, (tq, LANES), 1)
    if mode == "mla":
        q_scr[0:tq, :] = jnp.concatenate([q[:, :LANES], zero], axis=1)
        q_scr[tq:w, :] = jnp.concatenate([zero, q[:, LANES:]], axis=1)
    elif mode == "fox":
        ind_a = jnp.where(lane < 3, 1.0, 0.0).astype(BF16)
        ind_b = jnp.where(lane < 3, 0.0, jnp.where(lane < 6, 1.0, 0.0)).astype(BF16)
        q_scr[0:tq, :] = jnp.concatenate([jnp.where(lane < FOX_HD, q, zero), ind_a], axis=1)
        q_scr[tq:w, :] = jnp.concatenate([jnp.where(lane < FOX_HD, zero, q), ind_b], axis=1)
    else:
        q_scr[0:tq, :] = jnp.where(lane < DIFF_HD, q, zero)
        q_scr[tq:w, :] = jnp.where(lane < DIFF_HD, zero, q)
    m_scr[...] = jnp.full(m_scr.shape, NEG_BIG, F32)
    acc_scr[...] = jnp.zeros(acc_scr.shape, F32)

    def logits(j, slot):
        ks = pl.multiple_of(j * tk, tk)
        s_scr[slot] = lax.dot_general(k_ref[pl.ds(ks, tk), :], q_scr[...], (((1,), (1,)), ((), ())),
                                      preferred_element_type=F32)

    def consume(j, slot, masked):
        ks = pl.multiple_of(j * tk, tk)
        s = s_scr[slot]
        if masked:
            k_pos = ks + lax.broadcasted_iota(I32, (tk, w), 0)
            q_pos = i * tq + (lax.broadcasted_iota(I32, (tk, w), 1) & (tq - 1))
            if per_frame:
                visible = k_pos <= q_pos
            else:
                visible = (k_pos >> CHUNK_SHIFT) <= (q_pos >> CHUNK_SHIFT)
            s = jnp.where(visible, s, NEG_BIG)
        m_prev = m_scr[...]
        m_new = jnp.maximum(m_prev, jnp.max(s, axis=0, keepdims=True))
        alpha = jnp.exp2(m_prev - m_new)
        p = jnp.exp2(s - m_new).astype(BF16)
        acc_scr[...] = alpha * acc_scr[...] + jnp.dot(vt_ref[:, pl.ds(ks, tk)], p,
                                                      preferred_element_type=F32)
        m_scr[...] = m_new

    n_un = (i * tq) // tk
    logits(0, 0)

    def run_tiles(j0, count):
        for t in range(count):
            logits(j0 + t + 1, (t + 1) % 2)
            consume(j0 + t, t % 2, False)

    def main_body(jm, carry):
        run_tiles(ATT_UNROLL * jm, ATT_UNROLL)
        return carry

    n_main = n_un // ATT_UNROLL
    lax.fori_loop(0, n_main, main_body, 0)
    base = ATT_UNROLL * n_main
    rem = n_un - base
    span = ATT_UNROLL // 2
    while span >= 2:
        def tail(span=span):
            run_tiles(base + (rem & -(2 * span)), span)
        pl.when((rem & span) != 0)(tail)
        span //= 2

    @pl.when(n_un % 2 == 1)
    def _():
        logits(n_un, 1)
        consume(n_un - 1, 0, False)
        consume(n_un, 1, True)

    @pl.when(n_un % 2 == 0)
    def _():
        consume(n_un, 0, True)

    oa = acc_scr[0:LANES, 0:tq] / acc_scr[LANES:LANES + 1, 0:tq]
    ob = acc_scr[0:LANES, tq:w] / acc_scr[LANES:LANES + 1, tq:w]
    if mode == "diff":
        lv = lam_ref[...]
        lam = (jnp.exp(jnp.sum(lv[0:1] * lv[1:2], axis=1, keepdims=True))
               - jnp.exp(jnp.sum(lv[2:3] * lv[3:4], axis=1, keepdims=True)) + lam_init)
        o = oa - lam * ob
        o = o * lax.rsqrt(jnp.mean(o * o, axis=0, keepdims=True) + RMS_EPS)
        o = o.T * subln_ref[...] * (1.0 - lam_init)
    else:
        row = lax.broadcasted_iota(I32, (LANES, tq), 0)
        o = jnp.where(row < FOX_HD, oa, ob).T
    o_ref[...] = o.astype(BF16)


def _attention(mode, batch, seq, *, q, k, vt, q_col=0, extra=(), lam_init=0.0):
    n = batch * seq
    tq = min(TQ_ATT, seq // 2)
    tk = min(TK_ATT, seq)
    nq = seq // tq
    kw = LANES if mode == "diff" else 2 * LANES
    qw = 2 * LANES if mode == "mla" else LANES
    qb = q_col // qw
    in_specs = [pl.BlockSpec((tq, qw), lambda b, p, i: (b * nq + i, qb + p)),
                pl.BlockSpec((seq, kw), lambda b, p, i: (b, p)),
                pl.BlockSpec((1, 1, VT_ROWS, seq), lambda b, p, i: (b, p, 0, 0))]
    args = (q, k, vt)
    if mode == "diff":
        in_specs += [pl.BlockSpec((4, DIFF_HD), lambda b, p, i: (0, 0)),
                     pl.BlockSpec((1, LANES), lambda b, p, i: (0, 0))]
        args += tuple(extra)

    def body(*refs):
        refs = list(refs)
        refs[2] = refs[2].at[0, 0]
        _attn_kernel(*refs, mode=mode, tq=tq, tk=tk, per_frame=(mode == "fox"), lam_init=lam_init)

    return pl.pallas_call(
        body,
        grid=(batch, PAIRS, nq),
        in_specs=in_specs,
        out_specs=pl.BlockSpec((tq, LANES), lambda b, p, i: (b * nq + i, p)),
        out_shape=jax.ShapeDtypeStruct((n, PAIRS * LANES), BF16),
        scratch_shapes=[pltpu.VMEM((2 * tq, kw), BF16),
                        pltpu.VMEM((2, tk, 2 * tq), F32),
                        pltpu.VMEM((1, 2 * tq), F32),
                        pltpu.VMEM((VT_ROWS, 2 * tq), F32)],
        compiler_params=_cparams("parallel", "parallel", "arbitrary"),
        name="attn_" + mode,
    )(*args)


def _conv_kernel(b_ref, c_ref, u_ref, ch_ref, uh_ref, w_ref, o_ref, *, per_b):
    u = c_ref[...].astype(F32) * u_ref[...].astype(F32)
    halo = ch_ref[...].astype(F32) * uh_ref[...].astype(F32)
    halo = jnp.where(pl.program_id(0) % per_b == 0, 0.0, halo)
    hr = halo.shape[0]
    row = lax.broadcasted_iota(I32, u.shape, 0)
    u1 = jnp.where(row == 0, halo[hr - 1:hr, :], pltpu.roll(u, 1, 0))
    u2 = jnp.where(row == 0, halo[hr - 2:hr - 1, :],
                   jnp.where(row == 1, halo[hr - 1:hr, :], pltpu.roll(u, 2, 0)))
    w = w_ref[...]
    y = u2 * w[0:1, :] + u1 * w[1:2, :] + u * w[2:3, :]
    o_ref[...] = (b_ref[...].astype(F32) * y).astype(BF16)


def _short_conv(p, conv_w, seq):
    n = p.shape[0]
    ts = min(TM_PREP, seq)
    per_b = seq // ts
    w = CONV_WIDTH
    hr = BF16_SUBLANES
    halo_map = lambda col: (lambda i: (jnp.maximum(i * (ts // hr) - 1, 0), col // w))
    return pl.pallas_call(
        functools.partial(_conv_kernel, per_b=per_b),
        grid=(n // ts,),
        in_specs=[pl.BlockSpec((ts, w), lambda i: (i, COL_CB // w)),
                  pl.BlockSpec((ts, w), lambda i: (i, COL_CC // w)),
                  pl.BlockSpec((ts, w), lambda i: (i, COL_CU // w)),
                  pl.BlockSpec((hr, w), halo_map(COL_CC)),
                  pl.BlockSpec((hr, w), halo_map(COL_CU)),
                  pl.BlockSpec((CONV_TAPS, w), lambda i: (0, 0))],
        out_specs=pl.BlockSpec((ts, w), lambda i: (i, 0)),
        out_shape=jax.ShapeDtypeStruct((n, w), BF16),
        compiler_params=_cparams("parallel"),
        name="short_conv",
    )(p, p, p, p, p, conv_w)


def _merge_kernel(o0, o1, o2, o3, g0, g1, g2, g3, wb_ref, wo_ref, x_ref, gm_ref, out_ref):
    merged = None
    for br, (o_ref, g_ref) in enumerate(((o0, g0), (o1, g1), (o2, g2), (o3, g3))):
        t = jnp.dot(o_ref[...], wb_ref[br], preferred_element_type=F32) * g_ref[...].astype(F32)
        merged = t if merged is None else merged + t
    y = jnp.dot(merged.astype(BF16), wo_ref[...], preferred_element_type=F32)
    out_ref[...] = x_ref[...] + gm_ref[0] * y


def _merge(branches, gates, w_branch, w_out, x, g_m, seq):
    n, d = x.shape
    tm = min(TM_MERGE, seq)
    per_b = seq // tm
    bw = BRANCH_WIDTH
    o_spec = pl.BlockSpec((tm, bw), lambda i: (i, 0))
    assert P_WIDTH % d == 0
    g0 = P_WIDTH // d
    g_specs = [pl.BlockSpec((tm, d), lambda i, br=br: (i, g0 + br)) for br in range(4)]
    single = pl.Buffered(1)
    return pl.pallas_call(
        _merge_kernel,
        grid=(n // tm,),
        in_specs=[o_spec] * 4 + g_specs + [
            pl.BlockSpec((4, bw, d), lambda i: (0, 0, 0), pipeline_mode=single),
            pl.BlockSpec((d, d), lambda i: (0, 0), pipeline_mode=single),
            pl.BlockSpec((tm, d), lambda i: (i, 0)),
            pl.BlockSpec((1, 1, d), lambda i: (i // per_b, 0, 0))],
        out_specs=pl.BlockSpec((tm, d), lambda i: (i, 0)),
        out_shape=jax.ShapeDtypeStruct((n, d), F32),
        compiler_params=_cparams("parallel"),
        name="branch_merge",
    )(*branches, gates, gates, gates, gates, w_branch, w_out, x, g_m)


def _router_kernel(x_ref, g_ref, sc_ref, sh_ref, wr_ref, rb_ref, tri_ref,
                   h_ref, idx_ref, wts_ref, rank_ref, cnt_ref, carry):
    @pl.when(pl.program_id(0) == 0)
    def _():
        carry[...] = jnp.zeros_like(carry)

    h = _normmod(x_ref[...], g_ref[...], sc_ref[0], sh_ref[0])
    h_ref[...] = h
    logits = jnp.dot(h, wr_ref[...], precision=lax.Precision.HIGHEST, preferred_element_type=F32)
    scores = jax.nn.sigmoid(logits)
    work = scores + rb_ref[...]
    tm, ne = scores.shape
    lane = lax.broadcasted_iota(I32, (tm, ne), 1).astype(F32)
    slot = lax.broadcasted_iota(I32, (tm, TOP_K), 1)
    idx = jnp.zeros((tm, TOP_K), F32)
    sel = jnp.zeros((tm, TOP_K), F32)
    hits = []
    chosen = jnp.zeros((tm, ne), F32)
    for kk in range(TOP_K):
        mx = jnp.max(work, axis=1, keepdims=True)
        first = jnp.min(jnp.where(work == mx, lane, float(ne)), axis=1, keepdims=True)
        hit = lane == first
        hits.append(hit)
        chosen = jnp.where(hit, 1.0, chosen)
        sc_k = jnp.sum(jnp.where(hit, scores, 0.0), axis=1, keepdims=True)
        work = jnp.where(hit, -jnp.inf, work)
        idx = jnp.where(slot == kk, first, idx)
        sel = jnp.where(slot == kk, sc_k, sel)
    idx_ref[...] = idx.astype(I32)
    wts_ref[...] = sel / jnp.sum(sel, axis=1, keepdims=True) * ROUTED_SCALE
    before = jnp.dot(tri_ref[...], chosen.astype(BF16), preferred_element_type=F32) + carry[...]
    rank = jnp.zeros((tm, TOP_K), F32)
    for kk in range(TOP_K):
        r_k = jnp.sum(jnp.where(hits[kk], before, 0.0), axis=1, keepdims=True)
        rank = jnp.where(slot == kk, r_k, rank)
    rank_ref[...] = rank.astype(I32)
    total = carry[...] + jnp.sum(chosen, axis=0, keepdims=True)
    carry[...] = total
    cnt_ref[...] = total.astype(I32)


def _router(x, g, sc, sh, w_router, router_bias, seq):
    n, d = x.shape
    tm = min(TM_ROUTE, seq)
    per_b = seq // tm
    ne = w_router.shape[1]
    tri = jnp.tril(jnp.ones((tm, tm), F32), k=-1).astype(BF16)
    return pl.pallas_call(
        _router_kernel,
        grid=(n // tm,),
        in_specs=[pl.BlockSpec((tm, d), lambda i: (i, 0)),
                  pl.BlockSpec((1, d), lambda i: (0, 0)),
                  pl.BlockSpec((1, 1, d), lambda i: (i // per_b, 0, 0)),
                  pl.BlockSpec((1, 1, d), lambda i: (i // per_b, 0, 0)),
                  pl.BlockSpec((d, ne), lambda i: (0, 0)),
                  pl.BlockSpec((1, ne), lambda i: (0, 0)),
                  pl.BlockSpec((tm, tm), lambda i: (0, 0))],
        out_specs=[pl.BlockSpec((tm, d), lambda i: (i, 0)),
                   pl.BlockSpec((tm, TOP_K), lambda i: (i, 0)),
                   pl.BlockSpec((tm, TOP_K), lambda i: (i, 0)),
                   pl.BlockSpec((tm, TOP_K), lambda i: (i, 0)),
                   pl.BlockSpec((1, ne), lambda i: (0, 0))],
        out_shape=[jax.ShapeDtypeStruct((n, d), F32),
                   jax.ShapeDtypeStruct((n, TOP_K), I32),
                   jax.ShapeDtypeStruct((n, TOP_K), F32),
                   jax.ShapeDtypeStruct((n, TOP_K), I32),
                   jax.ShapeDtypeStruct((1, ne), I32)],
        scratch_shapes=[pltpu.VMEM((1, ne), F32)],
        compiler_params=_cparams("arbitrary"),
        name="moe_router",
    )(x, g, sc, sh, w_router, router_bias, tri)


GATHER_BUFS = 3


def _gather_step(i, slot, n_steps, idx_hbm, idx_smem, idx_sem, data_hbm, buf, data_sem, n_rows):
    assert n_steps >= GATHER_BUFS

    def idx_copy(step, slot):
        return pltpu.make_async_copy(idx_hbm.at[step], idx_smem.at[pl.ds(slot, 1)], idx_sem.at[slot])

    def rows_wait(slot):
        pltpu.make_async_copy(data_hbm.at[pl.ds(0, n_rows)], buf.at[slot], data_sem.at[slot]).wait()

    def issue_rows(slot):
        for r in range(n_rows):
            row = idx_smem[slot, r]
            pltpu.make_async_copy(data_hbm.at[pl.ds(row, 1)], buf.at[slot, pl.ds(r, 1)],
                                  data_sem.at[slot]).start(priority=r % 2)

    ahead = (slot + 2) % GATHER_BUFS
    last = n_steps - 1

    if slot == 0:
        @pl.when(i == 0)
        def _():
            for t in range(2):
                idx_copy(t, t).start()
                idx_copy(t, t).wait()
                issue_rows(t)
            idx_copy(2, 2).start()

    rows_wait(slot)
    idx_copy(jnp.minimum(i + 2, last), ahead).wait()

    def prefetch():
        issue_rows(ahead)
        idx_copy(jnp.minimum(i + 3, last), slot).start()

    def drain():
        @pl.when(i == last)
        def _():
            rows_wait((slot + 1) % GATHER_BUFS)
            rows_wait(ahead)
            idx_copy(last, slot).wait()

    return prefetch, drain


def _for_each_parity(i, step):
    for slot in range(GATHER_BUFS):
        @pl.when(i % GATHER_BUFS == slot)
        def _():
            step(slot)


def _expert_kernel(te_ref, src_hbm, h_hbm, wg_ref, wu_ref, wd_ref, y_ref,
                   idx_smem, idx_sem, xbuf, x_sem, wg_s, wu_s, wd_s, xs_s, *, n_steps, tm):
    i = pl.program_id(0)
    new_expert = jnp.logical_or(i == 0, te_ref[i] != te_ref[jnp.maximum(i - 1, 0)])

    @pl.when(new_expert)
    def _():
        wg_s[...] = wg_ref[0, 0].astype(BF16)
        wu_s[...] = wu_ref[0, 0].astype(BF16)
        wd_s[...] = wd_ref[0, 0].astype(BF16)

    def step(slot):
        prefetch, drain = _gather_step(i, slot, n_steps, src_hbm, idx_smem, idx_sem, h_hbm, xbuf, x_sem, tm)
        xs_s[...] = xbuf[slot].astype(BF16)
        prefetch()
        x = xs_s[...]
        a = jnp.dot(x, wg_s[...], preferred_element_type=F32)
        u = jnp.dot(x, wu_s[...], preferred_element_type=F32)
        hid = (a * jax.nn.sigmoid(a)) * u
        y_ref[...] = jnp.dot(hid.astype(BF16), wd_s[...], preferred_element_type=F32)
        drain()

    _for_each_parity(i, step)


def _expert_ffn(layer, tile_expert, src, h, w_eg, w_eu, w_ed):
    n_steps, _, tm = src.shape
    d = w_eg.shape[2]
    ff = w_eg.shape[3]
    grid_spec = pltpu.PrefetchScalarGridSpec(
        num_scalar_prefetch=1,
        grid=(n_steps,),
        in_specs=[pl.BlockSpec(memory_space=pl.ANY),
                  pl.BlockSpec(memory_space=pl.ANY),
                  pl.BlockSpec((1, 1, d, ff), lambda i, te: (layer, te[i], 0, 0)),
                  pl.BlockSpec((1, 1, d, ff), lambda i, te: (layer, te[i], 0, 0)),
                  pl.BlockSpec((1, 1, ff, d), lambda i, te: (layer, te[i], 0, 0))],
        out_specs=pl.BlockSpec((tm, d), lambda i, te: (i, 0)),
        scratch_shapes=[pltpu.SMEM((GATHER_BUFS, tm), I32),
                        pltpu.SemaphoreType.DMA((GATHER_BUFS,)),
                        pltpu.VMEM((GATHER_BUFS, tm, d), F32),
                        pltpu.SemaphoreType.DMA((GATHER_BUFS,)),
                        pltpu.VMEM((d, ff), BF16),
                        pltpu.VMEM((d, ff), BF16),
                        pltpu.VMEM((ff, d), BF16),
                        pltpu.VMEM((tm, d), BF16)],
    )
    return pl.pallas_call(
        functools.partial(_expert_kernel, n_steps=n_steps, tm=tm),
        grid_spec=grid_spec,
        out_shape=jax.ShapeDtypeStruct((n_steps * tm, d), F32),
        compiler_params=_cparams("arbitrary"),
        name="moe_experts",
    )(tile_expert, src, h, w_eg, w_eu, w_ed)


def _shared_kernel(h_ref, wg_ref, wu_ref, wd_ref, o_ref):
    h = h_ref[...].astype(BF16)
    a = jnp.dot(h, wg_ref[...], preferred_element_type=F32)
    u = jnp.dot(h, wu_ref[...], preferred_element_type=F32)
    hid = ((a * jax.nn.sigmoid(a)) * u).astype(BF16)
    o_ref[...] = jnp.dot(hid, wd_ref[...], preferred_element_type=F32)


def _shared_ffn(h, wg, wu, wd):
    n, d = h.shape
    ff = wg.shape[1]
    tm = min(TM_FFN, n)
    return pl.pallas_call(
        _shared_kernel,
        grid=(n // tm,),
        in_specs=[pl.BlockSpec((tm, d), lambda i: (i, 0)),
                  pl.BlockSpec((d, ff), lambda i: (0, 0)),
                  pl.BlockSpec((d, ff), lambda i: (0, 0)),
                  pl.BlockSpec((ff, d), lambda i: (0, 0))],
        out_specs=pl.BlockSpec((tm, d), lambda i: (i, 0)),
        out_shape=jax.ShapeDtypeStruct((n, d), F32),
        compiler_params=_cparams("parallel"),
        name="moe_shared",
    )(h, wg, wu, wd)


def _combine_kernel(dest_hbm, y_hbm, wts_ref, sh_ref, x_ref, gf_ref, fn_ref, o_ref,
                    idx_smem, idx_sem, ybuf, y_sem, *, n_steps, tc, final_norm):
    i = pl.program_id(0)
    rows = 8
    d = x_ref.shape[1]

    def step(slot):
        prefetch, drain = _gather_step(i, slot, n_steps, dest_hbm, idx_smem, idx_sem, y_hbm, ybuf, y_sem,
                                       TOP_K * tc)
        prefetch()
        for c in range(tc // rows):
            r0 = c * rows
            wts = wts_ref[r0:r0 + rows, :]
            wb = [jnp.broadcast_to(wts[:, kk:kk + 1], (rows, LANES)) for kk in range(TOP_K)]
            outs = []
            for s in range(d // LANES):
                sl = slice(s * LANES, (s + 1) * LANES)
                acc = sh_ref[r0:r0 + rows, sl]
                for kk in range(TOP_K):
                    acc = acc + wb[kk] * ybuf[slot, kk * tc + r0:kk * tc + r0 + rows, sl]
                outs.append(x_ref[r0:r0 + rows, sl] + gf_ref[0, :, sl] * acc)
            if final_norm:
                ssq = outs[0] * outs[0]
                for o in outs[1:]:
                    ssq = ssq + o * o
                inv = lax.rsqrt(jnp.sum(ssq, axis=1, keepdims=True) * (1.0 / d) + RMS_EPS)
                outs = [o * inv * fn_ref[:, s * LANES:(s + 1) * LANES] for s, o in enumerate(outs)]
            for s, o in enumerate(outs):
                o_ref[r0:r0 + rows, s * LANES:(s + 1) * LANES] = o
        drain()

    _for_each_parity(i, step)


def _combine(dest, y, wts, shared, x, g_f, final_gain, seq, *, final_norm):
    n, d = x.shape
    n_steps, _, rows = dest.shape
    tc = rows // TOP_K
    per_b = seq // tc
    return pl.pallas_call(
        functools.partial(_combine_kernel, n_steps=n_steps, tc=tc, final_norm=final_norm),
        grid=(n_steps,),
        in_specs=[pl.BlockSpec(memory_space=pl.ANY),
                  pl.BlockSpec(memory_space=pl.ANY),
                  pl.BlockSpec((tc, TOP_K), lambda i: (i, 0)),
                  pl.BlockSpec((tc, d), lambda i: (i, 0)),
                  pl.BlockSpec((tc, d), lambda i: (i, 0)),
                  pl.BlockSpec((1, 1, d), lambda i: (i // per_b, 0, 0)),
                  pl.BlockSpec((1, d), lambda i: (0, 0))],
        out_specs=pl.BlockSpec((tc, d), lambda i: (i, 0)),
        out_shape=jax.ShapeDtypeStruct((n, d), F32),
        scratch_shapes=[pltpu.SMEM((GATHER_BUFS, rows), I32),
                        pltpu.SemaphoreType.DMA((GATHER_BUFS,)),
                        pltpu.VMEM((GATHER_BUFS, rows, d), F32),
                        pltpu.SemaphoreType.DMA((GATHER_BUFS,))],
        compiler_params=_cparams("arbitrary"),
        name="moe_combine",
    )(dest, y, wts, shared, x, g_f, final_gain)


def _rope_tables(positions):
    pos = positions.reshape(-1).astype(F32)
    n = pos.shape[0]

    def cos_sin(half):
        inv_freq = ROPE_THETA ** (-jnp.arange(half, dtype=F32) / half)
        ang = pos[:, None] * inv_freq
        return jnp.cos(ang), jnp.sin(ang)

    cm, sm = cos_sin(MLA_ROPE // 2)
    tail = LANES - MLA_NOPE - MLA_ROPE
    mla_cos = jnp.concatenate([jnp.ones((n, MLA_NOPE), F32), cm, cm, jnp.ones((n, tail), F32)], axis=1)
    mla_sin = jnp.concatenate([jnp.zeros((n, MLA_NOPE), F32), sm, sm, jnp.zeros((n, tail), F32)], axis=1)
    cd, sd = cos_sin(DIFF_ROT // 2)
    rest = DIFF_HD - DIFF_ROT
    dc = jnp.concatenate([cd, cd, jnp.ones((n, rest), F32)], axis=1)
    ds = jnp.concatenate([sd, sd, jnp.zeros((n, rest), F32)], axis=1)
    return mla_cos, mla_sin, jnp.concatenate([dc, dc], axis=1), jnp.concatenate([ds, ds], axis=1)


def _in_proj_weight(w_in):
    d = w_in.shape[0]
    sizes = (MLA_Q_RANK, MLA_KV_RANK, MLA_ROPE, 512, 512, 512, FOX_HEADS, 512, 512, 512, 512, 512, 512)
    parts, off = [], 0
    for s in sizes:
        parts.append(w_in[:, off:off + s])
        off += s
    a_cq, a_ckv, a_kr, b_q, b_k, b_v, b_zf, c_b, c_c, c_u, d_q, d_k, d_v = parts
    zeros = lambda w: jnp.zeros((d, w), w_in.dtype)
    cols = [b_q, b_k, b_v, c_b, c_c, c_u, d_q, d_k, d_v, a_cq,
            zeros(MLA_NOPE), a_kr, zeros(LANES - MLA_NOPE - MLA_ROPE),
            a_ckv, b_zf, zeros(LANES - FOX_HEADS), zeros(P_WIDTH - COL_ZF - LANES)]
    return jnp.concatenate(cols, axis=1).astype(BF16)


def _in_proj_scale():
    s = np.ones((1, P_WIDTH), np.float32)
    qs = FOX_HD ** -0.5 * LOG2E
    s[0, COL_FOXQ:COL_FOXQ + FOX_HEADS * FOX_HD] = qs
    s[0, COL_DQ:COL_DQ + DIFF_HEADS * 2 * DIFF_HD] = qs
    return jnp.asarray(s)


def _dispatch_plan(idx, rank, counts, tm):
    n = idx.shape[0]
    counts = counts.reshape(-1)
    padded = (counts + tm - 1) // tm * tm
    ends = jnp.cumsum(padded)
    starts = ends - padded
    dest = (starts[idx] + rank).astype(I32)
    n_steps = (n * TOP_K) // tm + N_EXPERTS
    tok = jnp.repeat(jnp.arange(n, dtype=I32), TOP_K)
    src = jnp.zeros((n_steps * tm,), I32).at[dest.reshape(-1)].set(
        tok, unique_indices=True, indices_are_sorted=False, mode="promise_in_bounds")
    tile_start = jnp.arange(n_steps, dtype=I32) * tm
    tile_expert = jnp.sum((ends[None, :] <= tile_start[:, None]).astype(I32), axis=1)
    tile_expert = jnp.minimum(tile_expert, N_EXPERTS - 1)
    return dest, src.reshape(n_steps, 1, tm), tile_expert


def kernel(x, c, positions, ada_w, ada_b, norm_mix, norm_ffn, w_in, mla_q_norm, mla_w_uq, mla_kv_norm,
           mla_w_ukv, fox_f_bias, conv_w, diff_lam_q1, diff_lam_k1, diff_lam_q2, diff_lam_k2, diff_subln,
           w_branch, w_gate, b_gate, w_out, w_router, router_bias, w_exp_gate, w_exp_up, w_exp_down,
           w_sh_gate, w_sh_up, w_sh_down, final_norm):
    batch, seq, d = x.shape
    n = batch * seq
    depth = ada_w.shape[0]
    xf = x.reshape(n, d)

    mod = _modulation(c, ada_w, ada_b)
    mla_cos, mla_sin, diff_cos, diff_sin = _rope_tables(positions)
    p_scale = _in_proj_scale()

    for l in range(depth):
        lam_init = 0.8 - 0.6 * math.exp(-0.3 * l)
        mods = mod[l, :batch].reshape(batch, 6, 1, d)
        sh_m, sc_m, g_m, sh_f, sc_f, g_f = (mods[:, t] for t in range(6))

        w_g = w_gate[l].transpose(1, 0, 2).reshape(d, 4 * d).astype(BF16)
        w_all = jnp.concatenate([_in_proj_weight(w_in[l]), w_g], axis=1)
        epi = jnp.concatenate([p_scale, b_gate[l].reshape(1, 4 * d)], axis=1)
        p = _norm_proj(xf, norm_mix[l][None], sc_m, sh_m, w_all, epi, seq, scale_cols=P_WIDTH,
                       name="in_proj_gates")
        gates = p

        wq = jnp.pad(mla_w_uq[l].reshape(MLA_Q_RANK, MLA_HEADS, MLA_NOPE + MLA_ROPE),
                     ((0, 0), (0, 0), (0, LANES - MLA_NOPE - MLA_ROPE)))
        wq = wq.reshape(MLA_Q_RANK, MLA_HEADS * LANES).astype(BF16)
        wkv = mla_w_ukv[l].reshape(MLA_KV_RANK, MLA_HEADS, MLA_NOPE + MLA_V)
        wk = jnp.pad(wkv[:, :, :MLA_NOPE], ((0, 0), (0, 0), (0, LANES - MLA_NOPE)))
        wk = wk.reshape(MLA_KV_RANK, MLA_HEADS * LANES).astype(BF16)
        wv = wkv[:, :, MLA_NOPE:].reshape(MLA_KV_RANK, MLA_HEADS * MLA_V).astype(BF16)
        q_a, k_a, vt_a = _mla_prep(p, mla_cos, mla_sin, mla_q_norm[l][None], mla_kv_norm[l][None],
                                   wq, wk, wv, batch, seq)
        o_mla = _attention("mla", batch, seq, q=q_a, k=k_a, vt=vt_a)

        k_f, vt_f = _fox_prep(p, fox_f_bias[l], batch, seq)
        o_fox = _attention("fox", batch, seq, q=p, k=k_f, vt=vt_f, q_col=COL_FOXQ)

        o_conv = _short_conv(p, conv_w[l], seq)

        q_d, k_d, vt_d = _diff_prep(p, diff_cos, diff_sin, batch, seq)
        lam_vecs = jnp.stack([diff_lam_q1[l], diff_lam_k1[l], diff_lam_q2[l], diff_lam_k2[l]])
        o_diff = _attention("diff", batch, seq, q=q_d, k=k_d, vt=vt_d,
                            extra=(lam_vecs, diff_subln[l][None]), lam_init=lam_init)

        xf = _merge((o_mla, o_fox, o_conv, o_diff), gates, w_branch[l].astype(BF16),
                    w_out[l].astype(BF16), xf, g_m, seq)

        h, idx, wts, rank, counts = _router(xf, norm_ffn[l][None], sc_f, sh_f, w_router[l],
                                            router_bias[l][None], seq)
        dest, src, tile_expert = _dispatch_plan(idx, rank, counts, min(TM_EXP, n))
        y = _expert_ffn(l, tile_expert, src, h, w_exp_gate, w_exp_up, w_exp_down)
        shared = _shared_ffn(h, w_sh_gate[l].astype(BF16), w_sh_up[l].astype(BF16),
                             w_sh_down[l].astype(BF16))
        tc = min(TC_COMB, seq)
        dest_t = dest.reshape(n // tc, tc, TOP_K).transpose(0, 2, 1).reshape(n // tc, 1, TOP_K * tc)
        xf = _combine(dest_t, y, wts, shared, xf, g_f, final_norm[None], seq, final_norm=(l == depth - 1))

    return xf.reshape(batch, seq, d)
```

```python
import functools
import math

import numpy as np
import jax
import jax.numpy as jnp
from jax import lax
from jax.experimental import pallas as pl
from jax.experimental.pallas import tpu as pltpu

F32 = jnp.float32
BF16 = jnp.bfloat16
I32 = jnp.int32

CHUNK = 64
CHUNK_SHIFT = 6
RMS_EPS = 1e-6
ROPE_THETA = 500000.0
MLA_HEADS, MLA_Q_RANK, MLA_KV_RANK, MLA_NOPE, MLA_ROPE, MLA_V = 8, 384, 256, 64, 32, 64
FOX_HEADS, FOX_HD = 8, 64
CONV_WIDTH, CONV_TAPS = 512, 3
DIFF_HEADS, DIFF_HD, DIFF_ROT = 4, 64, 16
N_EXPERTS, TOP_K, ROUTED_SCALE = 64, 8, 2.5
BRANCH_WIDTH = 512
LOG2E = 1.4426950408889634

LANES = 128
BF16_SUBLANES = 16
V7X_VMEM_BYTES = 64 * 1024 * 1024
VMEM_LIMIT = V7X_VMEM_BYTES - 8 * 1024 * 1024
NEG_BIG = -1e30

TM_PROJ = 1024
TM_PROJ_CHUNK = 256
TM_NORM_CHUNK = 16
TN_PROJ = 1024
TM_PREP = 512
TS_CUM = 256
TQ_ATT = 512
TK_ATT = 512
ATT_UNROLL = 8
TM_MERGE = 256
TM_ROUTE = 256
TM_EXP = 256
TM_FFN = 512
TC_COMB = 64

PAIRS = 4
VT_ROWS = LANES + BF16_SUBLANES

COL_FOXQ, COL_FOXK, COL_FOXV = 0, 512, 1024
COL_CB, COL_CC, COL_CU = 1536, 2048, 2560
COL_DQ, COL_DK, COL_DV = 3072, 3584, 4096
COL_CQ = 4608
COL_KR = 4992
COL_CKV = 5120
COL_ZF = 5376
P_WIDTH = 6144


def _cparams(*sem):
    return pltpu.CompilerParams(dimension_semantics=sem, vmem_limit_bytes=VMEM_LIMIT)


def _normmod(x, g, sc, sh):
    ms = jnp.mean(x * x, axis=-1, keepdims=True)
    return (x * lax.rsqrt(ms + RMS_EPS)) * g * (1.0 + sc) + sh


def _rope_lanes(x, cos_t, sin_t, lo, half, period):
    width = x.shape[1]
    lane = lax.broadcasted_iota(I32, x.shape, 1) & (period - 1)
    x_up = pltpu.roll(x, width - half, 1)
    x_dn = pltpu.roll(x, half, 1)
    rot = jnp.where(lane < lo, 0.0,
                    jnp.where(lane < lo + half, -x_up,
                              jnp.where(lane < lo + 2 * half, x_dn, 0.0)))
    return x * cos_t + rot * sin_t


def _store_vt(vt_ref, v):
    tm = v.shape[0]
    ones = jnp.ones((VT_ROWS - LANES, tm), BF16)
    for p in range(PAIRS):
        vt_ref[0, p, 0:LANES, :] = v[:, p * LANES:(p + 1) * LANES].T.astype(BF16)
        vt_ref[0, p, LANES:VT_ROWS, :] = ones


def _vt_out(batch, seq, tm):
    per_b = seq // tm
    spec = pl.BlockSpec((1, PAIRS, VT_ROWS, tm), lambda i: (i // per_b, 0, 0, i % per_b))
    return spec, jax.ShapeDtypeStruct((batch, PAIRS, VT_ROWS, seq), BF16)


def _mod_kernel(c_ref, w_ref, b_ref, o_ref):
    c = c_ref[...]
    ca = (c * jax.nn.sigmoid(c)).astype(BF16)
    o_ref[0] = jnp.dot(ca, w_ref[0].astype(BF16), preferred_element_type=F32) + b_ref[0]


def _modulation(c, ada_w, ada_b):
    n_layers, d, d6 = ada_w.shape
    b = c.shape[0]
    rows = 8
    cp = jnp.zeros((rows, d), F32).at[:b].set(c)
    tn = 1024
    return pl.pallas_call(
        _mod_kernel,
        grid=(n_layers, d6 // tn),
        in_specs=[pl.BlockSpec((rows, d), lambda l, j: (0, 0)),
                  pl.BlockSpec((1, d, tn), lambda l, j: (l, 0, j)),
                  pl.BlockSpec((1, 1, tn), lambda l, j: (l, 0, j))],
        out_specs=pl.BlockSpec((1, rows, tn), lambda l, j: (l, 0, j)),
        out_shape=jax.ShapeDtypeStruct((n_layers, rows, d6), F32),
        compiler_params=_cparams("parallel", "parallel"),
        name="adaln_modulation",
    )(cp, ada_w, ada_b.reshape(n_layers, 1, d6))


def _proj_kernel(x_ref, g_ref, sc_ref, sh_ref, w_ref, e_ref, o_ref, h_scr, *, scale_tiles):
    j = pl.program_id(1)

    @pl.when(j == 0)
    def _():
        nc = min(TM_NORM_CHUNK, h_scr.shape[0])
        for cc in range(h_scr.shape[0] // nc):
            rows = slice(cc * nc, (cc + 1) * nc)
            h_scr[rows, :] = _normmod(x_ref[rows, :], g_ref[...], sc_ref[0], sh_ref[0]).astype(BF16)

    def column_tile(sigmoid):
        mc = min(TM_PROJ_CHUNK, h_scr.shape[0])
        for cc in range(h_scr.shape[0] // mc):
            rows = slice(cc * mc, (cc + 1) * mc)
            acc = jnp.dot(h_scr[rows, :], w_ref[0], preferred_element_type=F32)
            if sigmoid:
                acc = jax.nn.sigmoid(acc + e_ref[...])
            else:
                acc = acc * e_ref[...]
            o_ref[rows, :] = acc.astype(o_ref.dtype)

    @pl.when(j < scale_tiles)
    def _():
        column_tile(False)

    @pl.when(j >= scale_tiles)
    def _():
        column_tile(True)


def _norm_proj(x, g, sc, sh, w, epi, seq, *, scale_cols, name):
    n, d = x.shape
    cols = w.shape[1]
    tm = min(TM_PROJ, seq)
    tn = TN_PROJ
    assert cols % tn == 0 and scale_cols % tn == 0
    per_b = seq // tm
    w = w.reshape(d, cols // tn, tn).transpose(1, 0, 2)
    return pl.pallas_call(
        functools.partial(_proj_kernel, scale_tiles=scale_cols // tn),
        grid=(n // tm, cols // tn),
        in_specs=[pl.BlockSpec((tm, d), lambda i, j: (i, 0)),
                  pl.BlockSpec((1, d), lambda i, j: (0, 0)),
                  pl.BlockSpec((1, 1, d), lambda i, j: (i // per_b, 0, 0)),
                  pl.BlockSpec((1, 1, d), lambda i, j: (i // per_b, 0, 0)),
                  pl.BlockSpec((1, d, tn), lambda i, j: (j, 0, 0)),
                  pl.BlockSpec((1, tn), lambda i, j: (0, j))],
        out_specs=pl.BlockSpec((tm, tn), lambda i, j: (i, j)),
        out_shape=jax.ShapeDtypeStruct((n, cols), BF16),
        scratch_shapes=[pltpu.VMEM((tm, d), BF16)],
        compiler_params=_cparams("parallel", "arbitrary"),
        name=name,
    )(x, g, sc, sh, w, epi)


def _mla_prep_kernel(cq_ref, ckv_ref, kr_ref, cos_ref, sin_ref, qn_ref, kvn_ref,
                     wq_ref, wk_ref, wv_ref, q_ref, k_ref, vt_ref, *, scale):
    cos_t, sin_t = cos_ref[...], sin_ref[...]

    def rms(t, gain):
        t = t.astype(F32)
        return (t * lax.rsqrt(jnp.mean(t * t, axis=-1, keepdims=True) + RMS_EPS) * gain).astype(BF16)

    def rope(t):
        return _rope_lanes(t, cos_t, sin_t, MLA_NOPE, MLA_ROPE // 2, LANES)

    qn = rms(cq_ref[...], qn_ref[...])
    kvn = rms(ckv_ref[...], kvn_ref[...])
    q = jnp.dot(qn, wq_ref[...], preferred_element_type=F32)
    k = jnp.dot(kvn, wk_ref[...], preferred_element_type=F32)
    _store_vt(vt_ref, jnp.dot(kvn, wv_ref[...], preferred_element_type=F32))
    k_pe = rope(kr_ref[...].astype(F32))
    for h in range(MLA_HEADS):
        sl = slice(h * LANES, (h + 1) * LANES)
        q_ref[:, sl] = (rope(q[:, sl]) * scale).astype(BF16)
        k_ref[:, sl] = (k[:, sl] + k_pe).astype(BF16)


def _mla_prep(p, cos_t, sin_t, q_norm, kv_norm, wq, wk, wv, batch, seq):
    n = p.shape[0]
    tm = min(TM_PREP, seq)
    hq = MLA_HEADS * LANES
    hv = MLA_HEADS * MLA_V
    scale = float(MLA_NOPE + MLA_ROPE) ** -0.5 * LOG2E
    const = lambda i: (0, 0)
    vt_spec, vt_shape = _vt_out(batch, seq, tm)
    return pl.pallas_call(
        functools.partial(_mla_prep_kernel, scale=scale),
        grid=(n // tm,),
        in_specs=[pl.BlockSpec((tm, MLA_Q_RANK), lambda i: (i, COL_CQ // MLA_Q_RANK)),
                  pl.BlockSpec((tm, MLA_KV_RANK), lambda i: (i, COL_CKV // MLA_KV_RANK)),
                  pl.BlockSpec((tm, LANES), lambda i: (i, COL_KR // LANES)),
                  pl.BlockSpec((tm, LANES), lambda i: (i, 0)),
                  pl.BlockSpec((tm, LANES), lambda i: (i, 0)),
                  pl.BlockSpec((1, MLA_Q_RANK), const),
                  pl.BlockSpec((1, MLA_KV_RANK), const),
                  pl.BlockSpec((MLA_Q_RANK, hq), const),
                  pl.BlockSpec((MLA_KV_RANK, hq), const),
                  pl.BlockSpec((MLA_KV_RANK, hv), const)],
        out_specs=[pl.BlockSpec((tm, hq), lambda i: (i, 0)),
                   pl.BlockSpec((tm, hq), lambda i: (i, 0)),
                   vt_spec],
        out_shape=[jax.ShapeDtypeStruct((n, hq), BF16),
                   jax.ShapeDtypeStruct((n, hq), BF16),
                   vt_shape],
        compiler_params=_cparams("parallel"),
        name="mla_prep",
    )(p, p, p, cos_t, sin_t, q_norm, kv_norm, wq, wk, wv)


def _diff_prep_kernel(q_ref, k_ref, v_ref, cos_ref, sin_ref, qo_ref, ko_ref, vt_ref):
    cos_t, sin_t = cos_ref[...], sin_ref[...]
    for src, dst in ((q_ref, qo_ref), (k_ref, ko_ref)):
        for h in range(DIFF_HEADS):
            sl = slice(h * LANES, (h + 1) * LANES)
            t = src[:, sl].astype(F32)
            dst[:, sl] = _rope_lanes(t, cos_t, sin_t, 0, DIFF_ROT // 2, DIFF_HD).astype(BF16)
    _store_vt(vt_ref, v_ref[...].astype(F32))


def _diff_prep(p, cos_t, sin_t, batch, seq):
    n = p.shape[0]
    tm = min(TM_PREP, seq)
    w = DIFF_HEADS * 2 * DIFF_HD
    vt_spec, vt_shape = _vt_out(batch, seq, tm)
    return pl.pallas_call(
        _diff_prep_kernel,
        grid=(n // tm,),
        in_specs=[pl.BlockSpec((tm, w), lambda i: (i, COL_DQ // w)),
                  pl.BlockSpec((tm, w), lambda i: (i, COL_DK // w)),
                  pl.BlockSpec((tm, w), lambda i: (i, COL_DV // w)),
                  pl.BlockSpec((tm, LANES), lambda i: (i, 0)),
                  pl.BlockSpec((tm, LANES), lambda i: (i, 0))],
        out_specs=[pl.BlockSpec((tm, w), lambda i: (i, 0)),
                   pl.BlockSpec((tm, w), lambda i: (i, 0)),
                   vt_spec],
        out_shape=[jax.ShapeDtypeStruct((n, w), BF16)] * 2 + [vt_shape],
        compiler_params=_cparams("parallel"),
        name="diff_prep",
    )(p, p, p, cos_t, sin_t)


def _fox_prep_kernel(z_ref, k_ref, v_ref, b_ref, tri_ref, sel_ref, ko_ref, vt_ref, carry):
    @pl.when(pl.program_id(1) == 0)
    def _():
        carry[...] = jnp.zeros_like(carry)

    z = z_ref[...].astype(F32) + b_ref[...]
    log_f = jnp.minimum(z, 0.0) - jnp.log(1.0 + jnp.exp(-jnp.abs(z)))
    cum = jnp.dot(tri_ref[...], log_f, precision=lax.Precision.HIGHEST,
                  preferred_element_type=F32) + carry[...]
    ts = cum.shape[0]
    carry[...] = cum[ts - 1:ts, :]
    neg = cum * (-LOG2E)
    hi = neg.astype(BF16)
    r1 = neg - hi.astype(F32)
    mid = r1.astype(BF16)
    lo = (r1 - mid.astype(F32)).astype(BF16)
    parts = jnp.concatenate([hi, mid, lo], axis=1)
    for p in range(PAIRS):
        ko_ref[:, 2 * p * LANES:(2 * p + 1) * LANES] = k_ref[:, p * LANES:(p + 1) * LANES]
        ext = jnp.dot(parts, sel_ref[p], preferred_element_type=F32)
        ko_ref[:, (2 * p + 1) * LANES:(2 * p + 2) * LANES] = ext.astype(BF16)
    _store_vt(vt_ref, v_ref[...].astype(F32))


def _fox_decay_select():
    sel = np.zeros((PAIRS, 3 * LANES, LANES), np.float32)
    for p in range(PAIRS):
        for t in range(2):
            for j in range(3):
                sel[p, j * LANES + 2 * p + t, 3 * t + j] = 1.0
    return jnp.asarray(sel, BF16)


def _fox_prep(p, f_bias, batch, seq):
    n = p.shape[0]
    ts = min(TS_CUM, seq)
    per_b = seq // ts
    w = FOX_HEADS * FOX_HD
    bias = jnp.zeros((1, LANES), F32).at[0, :FOX_HEADS].set(f_bias)
    tri = jnp.tril(jnp.ones((ts, ts), F32))
    row = lambda b, j: b * per_b + j
    return pl.pallas_call(
        _fox_prep_kernel,
        grid=(batch, per_b),
        in_specs=[pl.BlockSpec((ts, LANES), lambda b, j: (row(b, j), COL_ZF // LANES)),
                  pl.BlockSpec((ts, w), lambda b, j: (row(b, j), COL_FOXK // w)),
                  pl.BlockSpec((ts, w), lambda b, j: (row(b, j), COL_FOXV // w)),
                  pl.BlockSpec((1, LANES), lambda b, j: (0, 0)),
                  pl.BlockSpec((ts, ts), lambda b, j: (0, 0)),
                  pl.BlockSpec((PAIRS, 3 * LANES, LANES), lambda b, j: (0, 0, 0))],
        out_specs=[pl.BlockSpec((ts, 2 * w), lambda b, j: (row(b, j), 0)),
                   pl.BlockSpec((1, PAIRS, VT_ROWS, ts), lambda b, j: (b, 0, 0, j))],
        out_shape=[jax.ShapeDtypeStruct((n, 2 * w), BF16),
                   jax.ShapeDtypeStruct((batch, PAIRS, VT_ROWS, seq), BF16)],
        scratch_shapes=[pltpu.VMEM((1, LANES), F32)],
        compiler_params=_cparams("parallel", "arbitrary"),
        name="fox_prep",
    )(p, p, p, bias, tri, _fox_decay_select())


def _attn_kernel(*refs, mode, tq, tk, per_frame, lam_init):
    if mode == "diff":
        q_ref, k_ref, vt_ref, lam_ref, subln_ref, o_ref, q_scr, s_scr, m_scr, acc_scr = refs
    else:
        q_ref, k_ref, vt_ref, o_ref, q_scr, s_scr, m_scr, acc_scr = refs
    w = 2 * tq
    i = pl.program_id(2)
    q = q_ref[...]
    zero = jnp.zeros((tq, LANES), BF16)
    lane = lax.broadcasted_iota(I32, (tq, LANES), 1)
    if mode == "mla":
        q_scr[0:tq, :] = jnp.concatenate([q[:, :LANES], zero], axis=1)
        q_scr[tq:w, :] = jnp.concatenate([zero, q[:, LANES:]], axis=1)
    elif mode == "fox":
        ind_a = jnp.where(lane < 3, 1.0, 0.0).astype(BF16)
        ind_b = jnp.where(lane < 3, 0.0, jnp.where(lane < 6, 1.0, 0.0)).astype(BF16)
        q_scr[0:tq, :] = jnp.concatenate([jnp.where(lane < FOX_HD, q, zero), ind_a], axis=1)
        q_scr[tq:w, :] = jnp.concatenate([jnp.where(lane < FOX_HD, zero, q), ind_b], axis=1)
    else:
        q_scr[0:tq, :] = jnp.where(lane < DIFF_HD, q, zero)
        q_scr[tq:w, :] = jnp.where(lane < DIFF_HD, zero, q)
    m_scr[...] = jnp.full(m_scr.shape, NEG_BIG, F32)
    acc_scr[...] = jnp.zeros(acc_scr.shape, F32)

    def logits(j, slot):
        ks = pl.multiple_of(j * tk, tk)
        s_scr[slot] = lax.dot_general(k_ref[pl.ds(ks, tk), :], q_scr[...], (((1,), (1,)), ((), ())),
                                      preferred_element_type=F32)

    def consume(j, slot, masked):
        ks = pl.multiple_of(j * tk, tk)
        s = s_scr[slot]
        if masked:
            k_pos = ks + lax.broadcasted_iota(I32, (tk, w), 0)
            q_pos = i * tq + (lax.broadcasted_iota(I32, (tk, w), 1) & (tq - 1))
            if per_frame:
                visible = k_pos <= q_pos
            else:
                visible = (k_pos >> CHUNK_SHIFT) <= (q_pos >> CHUNK_SHIFT)
            s = jnp.where(visible, s, NEG_BIG)
        m_prev = m_scr[...]
        m_new = jnp.maximum(m_prev, jnp.max(s, axis=0, keepdims=True))
        alpha = jnp.exp2(m_prev - m_new)
        p = jnp.exp2(s - m_new).astype(BF16)
        acc_scr[...] = alpha * acc_scr[...] + jnp.dot(vt_ref[:, pl.ds(ks, tk)], p,
                                                      preferred_element_type=F32)
        m_scr[...] = m_new

    n_un = (i * tq) // tk
    logits(0, 0)

    def run_tiles(j0, count):
        for t in range(count):
            logits(j0 + t + 1, (t + 1) % 2)
            consume(j0 + t, t % 2, False)

    def main_body(jm, carry):
        run_tiles(ATT_UNROLL * jm, ATT_UNROLL)
        return carry

    n_main = n_un // ATT_UNROLL
    lax.fori_loop(0, n_main, main_body, 0)
    base = ATT_UNROLL * n_main
    rem = n_un - base
    span = ATT_UNROLL // 2
    while span >= 2:
        def tail(span=span):
            run_tiles(base + (rem & -(2 * span)), span)
        pl.when((rem & span) != 0)(tail)
        span //= 2

    @pl.when(n_un % 2 == 1)
    def _():
        logits(n_un, 1)
        consume(n_un - 1, 0, False)
        consume(n_un, 1, True)

    @pl.when(n_un % 2 == 0)
    def _():
        consume(n_un, 0, True)

    oa = acc_scr[0:LANES, 0:tq] / acc_scr[LANES:LANES + 1, 0:tq]
    ob = acc_scr[0:LANES, tq:w] / acc_scr[LANES:LANES + 1, tq:w]
    if mode == "diff":
        lv = lam_ref[...]
        lam = (jnp.exp(jnp.sum(lv[0:1] * lv[1:2], axis=1, keepdims=True))
               - jnp.exp(jnp.sum(lv[2:3] * lv[3:4], axis=1, keepdims=True)) + lam_init)
        o = oa - lam * ob
        o = o * lax.rsqrt(jnp.mean(o * o, axis=0, keepdims=True) + RMS_EPS)
        o = o.T * subln_ref[...] * (1.0 - lam_init)
    else:
        row = lax.broadcasted_iota(I32, (LANES, tq), 0)
        o = jnp.where(row < FOX_HD, oa, ob).T
    o_ref[...] = o.astype(BF16)


def _attention(mode, batch, seq, *, q, k, vt, q_col=0, extra=(), lam_init=0.0):
    n = batch * seq
    tq = min(TQ_ATT, seq // 2)
    tk = min(TK_ATT, seq)
    nq = seq // tq
    kw = LANES if mode == "diff" else 2 * LANES
    qw = 2 * LANES if mode == "mla" else LANES
    qb = q_col // qw
    in_specs = [pl.BlockSpec((tq, qw), lambda b, p, i: (b * nq + i, qb + p)),
                pl.BlockSpec((seq, kw), lambda b, p, i: (b, p)),
                pl.BlockSpec((1, 1, VT_ROWS, seq), lambda b, p, i: (b, p, 0, 0))]
    args = (q, k, vt)
    if mode == "diff":
        in_specs += [pl.BlockSpec((4, DIFF_HD), lambda b, p, i: (0, 0)),
                     pl.BlockSpec((1, LANES), lambda b, p, i: (0, 0))]
        args += tuple(extra)

    def body(*refs):
        refs = list(refs)
        refs[2] = refs[2].at[0, 0]
        _attn_kernel(*refs, mode=mode, tq=tq, tk=tk, per_frame=(mode == "fox"), lam_init=lam_init)

    return pl.pallas_call(
        body,
        grid=(batch, PAIRS, nq),
        in_specs=in_specs,
        out_specs=pl.BlockSpec((tq, LANES), lambda b, p, i: (b * nq + i, p)),
        out_shape=jax.ShapeDtypeStruct((n, PAIRS * LANES), BF16),
        scratch_shapes=[pltpu.VMEM((2 * tq, kw), BF16),
                        pltpu.VMEM((2, tk, 2 * tq), F32),
                        pltpu.VMEM((1, 2 * tq), F32),
                        pltpu.VMEM((VT_ROWS, 2 * tq), F32)],
        compiler_params=_cparams("parallel", "parallel", "arbitrary"),
        name="attn_" + mode,
    )(*args)


def _conv_kernel(b_ref, c_ref, u_ref, ch_ref, uh_ref, w_ref, o_ref, *, per_b):
    u = c_ref[...].astype(F32) * u_ref[...].astype(F32)
    halo = ch_ref[...].astype(F32) * uh_ref[...].astype(F32)
    halo = jnp.where(pl.program_id(0) % per_b == 0, 0.0, halo)
    hr = halo.shape[0]
    row = lax.broadcasted_iota(I32, u.shape, 0)
    u1 = jnp.where(row == 0, halo[hr - 1:hr, :], pltpu.roll(u, 1, 0))
    u2 = jnp.where(row == 0, halo[hr - 2:hr - 1, :],
                   jnp.where(row == 1, halo[hr - 1:hr, :], pltpu.roll(u, 2, 0)))
    w = w_ref[...]
    y = u2 * w[0:1, :] + u1 * w[1:2, :] + u * w[2:3, :]
    o_ref[...] = (b_ref[...].astype(F32) * y).astype(BF16)


def _short_conv(p, conv_w, seq):
    n = p.shape[0]
    ts = min(TM_PREP, seq)
    per_b = seq // ts
    w = CONV_WIDTH
    hr = BF16_SUBLANES
    halo_map = lambda col: (lambda i: (jnp.maximum(i * (ts // hr) - 1, 0), col // w))
    return pl.pallas_call(
        functools.partial(_conv_kernel, per_b=per_b),
        grid=(n // ts,),
        in_specs=[pl.BlockSpec((ts, w), lambda i: (i, COL_CB // w)),
                  pl.BlockSpec((ts, w), lambda i: (i, COL_CC // w)),
                  pl.BlockSpec((ts, w), lambda i: (i, COL_CU // w)),
                  pl.BlockSpec((hr, w), halo_map(COL_CC)),
                  pl.BlockSpec((hr, w), halo_map(COL_CU)),
                  pl.BlockSpec((CONV_TAPS, w), lambda i: (0, 0))],
        out_specs=pl.BlockSpec((ts, w), lambda i: (i, 0)),
        out_shape=jax.ShapeDtypeStruct((n, w), BF16),
        compiler_params=_cparams("parallel"),
        name="short_conv",
    )(p, p, p, p, p, conv_w)


def _merge_kernel(o0, o1, o2, o3, g0, g1, g2, g3, wb_ref, wo_ref, x_ref, gm_ref, out_ref):
    merged = None
    for br, (o_ref, g_ref) in enumerate(((o0, g0), (o1, g1), (o2, g2), (o3, g3))):
        t = jnp.dot(o_ref[...], wb_ref[br], preferred_element_type=F32) * g_ref[...].astype(F32)
        merged = t if merged is None else merged + t
    y = jnp.dot(merged.astype(BF16), wo_ref[...], preferred_element_type=F32)
    out_ref[...] = x_ref[...] + gm_ref[0] * y


def _merge(branches, gates, w_branch, w_out, x, g_m, seq):
    n, d = x.shape
    tm = min(TM_MERGE, seq)
    per_b = seq // tm
    bw = BRANCH_WIDTH
    o_spec = pl.BlockSpec((tm, bw), lambda i: (i, 0))
    assert P_WIDTH % d == 0
    g0 = P_WIDTH // d
    g_specs = [pl.BlockSpec((tm, d), lambda i, br=br: (i, g0 + br)) for br in range(4)]
    single = pl.Buffered(1)
    return pl.pallas_call(
        _merge_kernel,
        grid=(n // tm,),
        in_specs=[o_spec] * 4 + g_specs + [
            pl.BlockSpec((4, bw, d), lambda i: (0, 0, 0), pipeline_mode=single),
            pl.BlockSpec((d, d), lambda i: (0, 0), pipeline_mode=single),
            pl.BlockSpec((tm, d), lambda i: (i, 0)),
            pl.BlockSpec((1, 1, d), lambda i: (i // per_b, 0, 0))],
        out_specs=pl.BlockSpec((tm, d), lambda i: (i, 0)),
        out_shape=jax.ShapeDtypeStruct((n, d), F32),
        compiler_params=_cparams("parallel"),
        name="branch_merge",
    )(*branches, gates, gates, gates, gates, w_branch, w_out, x, g_m)


def _router_kernel(x_ref, g_ref, sc_ref, sh_ref, wr_ref, rb_ref, tri_ref,
                   h_ref, idx_ref, wts_ref, rank_ref, cnt_ref, carry):
    @pl.when(pl.program_id(0) == 0)
    def _():
        carry[...] = jnp.zeros_like(carry)

    h = _normmod(x_ref[...], g_ref[...], sc_ref[0], sh_ref[0])
    h_ref[...] = h
    logits = jnp.dot(h, wr_ref[...], precision=lax.Precision.HIGHEST, preferred_element_type=F32)
    scores = jax.nn.sigmoid(logits)
    work = scores + rb_ref[...]
    tm, ne = scores.shape
    lane = lax.broadcasted_iota(I32, (tm, ne), 1).astype(F32)
    slot = lax.broadcasted_iota(I32, (tm, TOP_K), 1)
    idx = jnp.zeros((tm, TOP_K), F32)
    sel = jnp.zeros((tm, TOP_K), F32)
    hits = []
    chosen = jnp.zeros((tm, ne), F32)
    for kk in range(TOP_K):
        mx = jnp.max(work, axis=1, keepdims=True)
        first = jnp.min(jnp.where(work == mx, lane, float(ne)), axis=1, keepdims=True)
        hit = lane == first
        hits.append(hit)
        chosen = jnp.where(hit, 1.0, chosen)
        sc_k = jnp.sum(jnp.where(hit, scores, 0.0), axis=1, keepdims=True)
        work = jnp.where(hit, -jnp.inf, work)
        idx = jnp.where(slot == kk, first, idx)
        sel = jnp.where(slot == kk, sc_k, sel)
    idx_ref[...] = idx.astype(I32)
    wts_ref[...] = sel / jnp.sum(sel, axis=1, keepdims=True) * ROUTED_SCALE
    before = jnp.dot(tri_ref[...], chosen.astype(BF16), preferred_element_type=F32) + carry[...]
    rank = jnp.zeros((tm, TOP_K), F32)
    for kk in range(TOP_K):
        r_k = jnp.sum(jnp.where(hits[kk], before, 0.0), axis=1, keepdims=True)
        rank = jnp.where(slot == kk, r_k, rank)
    rank_ref[...] = rank.astype(I32)
    total = carry[...] + jnp.sum(chosen, axis=0, keepdims=True)
    carry[...] = total
    cnt_ref[...] = total.astype(I32)


def _router(x, g, sc, sh, w_router, router_bias, seq):
    n, d = x.shape
    tm = min(TM_ROUTE, seq)
    per_b = seq // tm
    ne = w_router.shape[1]
    tri = jnp.tril(jnp.ones((tm, tm), F32), k=-1).astype(BF16)
    return pl.pallas_call(
        _router_kernel,
        grid=(n // tm,),
        in_specs=[pl.BlockSpec((tm, d), lambda i: (i, 0)),
                  pl.BlockSpec((1, d), lambda i: (0, 0)),
                  pl.BlockSpec((1, 1, d), lambda i: (i // per_b, 0, 0)),
                  pl.BlockSpec((1, 1, d), lambda i: (i // per_b, 0, 0)),
                  pl.BlockSpec((d, ne), lambda i: (0, 0)),
                  pl.BlockSpec((1, ne), lambda i: (0, 0)),
                  pl.BlockSpec((tm, tm), lambda i: (0, 0))],
        out_specs=[pl.BlockSpec((tm, d), lambda i: (i, 0)),
                   pl.BlockSpec((tm, TOP_K), lambda i: (i, 0)),
                   pl.BlockSpec((tm, TOP_K), lambda i: (i, 0)),
                   pl.BlockSpec((tm, TOP_K), lambda i: (i, 0)),
                   pl.BlockSpec((1, ne), lambda i: (0, 0))],
        out_shape=[jax.ShapeDtypeStruct((n, d), F32),
                   jax.ShapeDtypeStruct((n, TOP_K), I32),
                   jax.ShapeDtypeStruct((n, TOP_K), F32),
                   jax.ShapeDtypeStruct((n, TOP_K), I32),
                   jax.ShapeDtypeStruct((1, ne), I32)],
        scratch_shapes=[pltpu.VMEM((1, ne), F32)],
        compiler_params=_cparams("arbitrary"),
        name="moe_router",
    )(x, g, sc, sh, w_router, router_bias, tri)


GATHER_BUFS = 3


def _gather_step(i, slot, n_steps, idx_hbm, idx_smem, idx_sem, data_hbm, buf, data_sem, n_rows):
    assert n_steps >= GATHER_BUFS

    def idx_copy(step, slot):
        return pltpu.make_async_copy(idx_hbm.at[step], idx_smem.at[pl.ds(slot, 1)], idx_sem.at[slot])

    def rows_wait(slot):
        pltpu.make_async_copy(data_hbm.at[pl.ds(0, n_rows)], buf.at[slot], data_sem.at[slot]).wait()

    def issue_rows(slot):
        for r in range(n_rows):
            row = idx_smem[slot, r]
            pltpu.make_async_copy(data_hbm.at[pl.ds(row, 1)], buf.at[slot, pl.ds(r, 1)],
                                  data_sem.at[slot]).start(priority=r % 2)

    ahead = (slot + 2) % GATHER_BUFS
    last = n_steps - 1

    if slot == 0:
        @pl.when(i == 0)
        def _():
            for t in range(2):
                idx_copy(t, t).start()
                idx_copy(t, t).wait()
                issue_rows(t)
            idx_copy(2, 2).start()

    rows_wait(slot)
    idx_copy(jnp.minimum(i + 2, last), ahead).wait()

    def prefetch():
        issue_rows(ahead)
        idx_copy(jnp.minimum(i + 3, last), slot).start()

    def drain():
        @pl.when(i == last)
        def _():
            rows_wait((slot + 1) % GATHER_BUFS)
            rows_wait(ahead)
            idx_copy(last, slot).wait()

    return prefetch, drain


def _for_each_parity(i, step):
    for slot in range(GATHER_BUFS):
        @pl.when(i % GATHER_BUFS == slot)
        def _():
            step(slot)


def _expert_kernel(te_ref, src_hbm, h_hbm, wg_ref, wu_ref, wd_ref, y_ref,
                   idx_smem, idx_sem, xbuf, x_sem, wg_s, wu_s, wd_s, xs_s, *, n_steps, tm):
    i = pl.program_id(0)
    new_expert = jnp.logical_or(i == 0, te_ref[i] != te_ref[jnp.maximum(i - 1, 0)])

    @pl.when(new_expert)
    def _():
        wg_s[...] = wg_ref[0, 0].astype(BF16)
        wu_s[...] = wu_ref[0, 0].astype(BF16)
        wd_s[...] = wd_ref[0, 0].astype(BF16)

    def step(slot):
        prefetch, drain = _gather_step(i, slot, n_steps, src_hbm, idx_smem, idx_sem, h_hbm, xbuf, x_sem, tm)
        xs_s[...] = xbuf[slot].astype(BF16)
        prefetch()
        x = xs_s[...]
        a = jnp.dot(x, wg_s[...], preferred_element_type=F32)
        u = jnp.dot(x, wu_s[...], preferred_element_type=F32)
        hid = (a * jax.nn.sigmoid(a)) * u
        y_ref[...] = jnp.dot(hid.astype(BF16), wd_s[...], preferred_element_type=F32)
        drain()

    _for_each_parity(i, step)


def _expert_ffn(layer, tile_expert, src, h, w_eg, w_eu, w_ed):
    n_steps, _, tm = src.shape
    d = w_eg.shape[2]
    ff = w_eg.shape[3]
    grid_spec = pltpu.PrefetchScalarGridSpec(
        num_scalar_prefetch=1,
        grid=(n_steps,),
        in_specs=[pl.BlockSpec(memory_space=pl.ANY),
                  pl.BlockSpec(memory_space=pl.ANY),
                  pl.BlockSpec((1, 1, d, ff), lambda i, te: (layer, te[i], 0, 0)),
                  pl.BlockSpec((1, 1, d, ff), lambda i, te: (layer, te[i], 0, 0)),
                  pl.BlockSpec((1, 1, ff, d), lambda i, te: (layer, te[i], 0, 0))],
        out_specs=pl.BlockSpec((tm, d), lambda i, te: (i, 0)),
        scratch_shapes=[pltpu.SMEM((GATHER_BUFS, tm), I32),
                        pltpu.SemaphoreType.DMA((GATHER_BUFS,)),
                        pltpu.VMEM((GATHER_BUFS, tm, d), F32),
                        pltpu.SemaphoreType.DMA((GATHER_BUFS,)),
                        pltpu.VMEM((d, ff), BF16),
                        pltpu.VMEM((d, ff), BF16),
                        pltpu.VMEM((ff, d), BF16),
                        pltpu.VMEM((tm, d), BF16)],
    )
    return pl.pallas_call(
        functools.partial(_expert_kernel, n_steps=n_steps, tm=tm),
        grid_spec=grid_spec,
        out_shape=jax.ShapeDtypeStruct((n_steps * tm, d), F32),
        compiler_params=_cparams("arbitrary"),
        name="moe_experts",
    )(tile_expert, src, h, w_eg, w_eu, w_ed)


def _shared_kernel(h_ref, wg_ref, wu_ref, wd_ref, o_ref):
    h = h_ref[...].astype(BF16)
    a = jnp.dot(h, wg_ref[...], preferred_element_type=F32)
    u = jnp.dot(h, wu_ref[...], preferred_element_type=F32)
    hid = ((a * jax.nn.sigmoid(a)) * u).astype(BF16)
    o_ref[...] = jnp.dot(hid, wd_ref[...], preferred_element_type=F32)


def _shared_ffn(h, wg, wu, wd):
    n, d = h.shape
    ff = wg.shape[1]
    tm = min(TM_FFN, n)
    return pl.pallas_call(
        _shared_kernel,
        grid=(n // tm,),
        in_specs=[pl.BlockSpec((tm, d), lambda i: (i, 0)),
                  pl.BlockSpec((d, ff), lambda i: (0, 0)),
                  pl.BlockSpec((d, ff), lambda i: (0, 0)),
                  pl.BlockSpec((ff, d), lambda i: (0, 0))],
        out_specs=pl.BlockSpec((tm, d), lambda i: (i, 0)),
        out_shape=jax.ShapeDtypeStruct((n, d), F32),
        compiler_params=_cparams("parallel"),
        name="moe_shared",
    )(h, wg, wu, wd)


def _combine_kernel(dest_hbm, y_hbm, wts_ref, sh_ref, x_ref, gf_ref, fn_ref, o_ref,
                    idx_smem, idx_sem, ybuf, y_sem, *, n_steps, tc, final_norm):
    i = pl.program_id(0)
    rows = 8
    d = x_ref.shape[1]

    def step(slot):
        prefetch, drain = _gather_step(i, slot, n_steps, dest_hbm, idx_smem, idx_sem, y_hbm, ybuf, y_sem,
                                       TOP_K * tc)
        prefetch()
        for c in range(tc // rows):
            r0 = c * rows
            wts = wts_ref[r0:r0 + rows, :]
            wb = [jnp.broadcast_to(wts[:, kk:kk + 1], (rows, LANES)) for kk in range(TOP_K)]
            outs = []
            for s in range(d // LANES):
                sl = slice(s * LANES, (s + 1) * LANES)
                acc = sh_ref[r0:r0 + rows, sl]
                for kk in range(TOP_K):
                    acc = acc + wb[kk] * ybuf[slot, kk * tc + r0:kk * tc + r0 + rows, sl]
                outs.append(x_ref[r0:r0 + rows, sl] + gf_ref[0, :, sl] * acc)
            if final_norm:
                ssq = outs[0] * outs[0]
                for o in outs[1:]:
                    ssq = ssq + o * o
                inv = lax.rsqrt(jnp.sum(ssq, axis=1, keepdims=True) * (1.0 / d) + RMS_EPS)
                outs = [o * inv * fn_ref[:, s * LANES:(s + 1) * LANES] for s, o in enumerate(outs)]
            for s, o in enumerate(outs):
                o_ref[r0:r0 + rows, s * LANES:(s + 1) * LANES] = o
        drain()

    _for_each_parity(i, step)


def _combine(dest, y, wts, shared, x, g_f, final_gain, seq, *, final_norm):
    n, d = x.shape
    n_steps, _, rows = dest.shape
    tc = rows // TOP_K
    per_b = seq // tc
    return pl.pallas_call(
        functools.partial(_combine_kernel, n_steps=n_steps, tc=tc, final_norm=final_norm),
        grid=(n_steps,),
        in_specs=[pl.BlockSpec(memory_space=pl.ANY),
                  pl.BlockSpec(memory_space=pl.ANY),
                  pl.BlockSpec((tc, TOP_K), lambda i: (i, 0)),
                  pl.BlockSpec((tc, d), lambda i: (i, 0)),
                  pl.BlockSpec((tc, d), lambda i: (i, 0)),
                  pl.BlockSpec((1, 1, d), lambda i: (i // per_b, 0, 0)),
                  pl.BlockSpec((1, d), lambda i: (0, 0))],
        out_specs=pl.BlockSpec((tc, d), lambda i: (i, 0)),
        out_shape=jax.ShapeDtypeStruct((n, d), F32),
        scratch_shapes=[pltpu.SMEM((GATHER_BUFS, rows), I32),
                        pltpu.SemaphoreType.DMA((GATHER_BUFS,)),
                        pltpu.VMEM((GATHER_BUFS, rows, d), F32),
                        pltpu.SemaphoreType.DMA((GATHER_BUFS,))],
        compiler_params=_cparams("arbitrary"),
        name="moe_combine",
    )(dest, y, wts, shared, x, g_f, final_gain)


def _rope_tables(positions):
    pos = positions.reshape(-1).astype(F32)
    n = pos.shape[0]

    def cos_sin(half):
        inv_freq = ROPE_THETA ** (-jnp.arange(half, dtype=F32) / half)
        ang = pos[:, None] * inv_freq
        return jnp.cos(ang), jnp.sin(ang)

    cm, sm = cos_sin(MLA_ROPE // 2)
    tail = LANES - MLA_NOPE - MLA_ROPE
    mla_cos = jnp.concatenate([jnp.ones((n, MLA_NOPE), F32), cm, cm, jnp.ones((n, tail), F32)], axis=1)
    mla_sin = jnp.concatenate([jnp.zeros((n, MLA_NOPE), F32), sm, sm, jnp.zeros((n, tail), F32)], axis=1)
    cd, sd = cos_sin(DIFF_ROT // 2)
    rest = DIFF_HD - DIFF_ROT
    dc = jnp.concatenate([cd, cd, jnp.ones((n, rest), F32)], axis=1)
    ds = jnp.concatenate([sd, sd, jnp.zeros((n, rest), F32)], axis=1)
    return mla_cos, mla_sin, jnp.concatenate([dc, dc], axis=1), jnp.concatenate([ds, ds], axis=1)


def _in_proj_weight(w_in):
    d = w_in.shape[0]
    sizes = (MLA_Q_RANK, MLA_KV_RANK, MLA_ROPE, 512, 512, 512, FOX_HEADS, 512, 512, 512, 512, 512, 512)
    parts, off = [], 0
    for s in sizes:
        parts.append(w_in[:, off:off + s])
        off += s
    a_cq, a_ckv, a_kr, b_q, b_k, b_v, b_zf, c_b, c_c, c_u, d_q, d_k, d_v = parts
    zeros = lambda w: jnp.zeros((d, w), w_in.dtype)
    cols = [b_q, b_k, b_v, c_b, c_c, c_u, d_q, d_k, d_v, a_cq,
            zeros(MLA_NOPE), a_kr, zeros(LANES - MLA_NOPE - MLA_ROPE),
            a_ckv, b_zf, zeros(LANES - FOX_HEADS), zeros(P_WIDTH - COL_ZF - LANES)]
    return jnp.concatenate(cols, axis=1).astype(BF16)


def _in_proj_scale():
    s = np.ones((1, P_WIDTH), np.float32)
    qs = FOX_HD ** -0.5 * LOG2E
    s[0, COL_FOXQ:COL_FOXQ + FOX_HEADS * FOX_HD] = qs
    s[0, COL_DQ:COL_DQ + DIFF_HEADS * 2 * DIFF_HD] = qs
    return jnp.asarray(s)


def _dispatch_plan(idx, rank, counts, tm):
    n = idx.shape[0]
    counts = counts.reshape(-1)
    padded = (counts + tm - 1) // tm * tm
    ends = jnp.cumsum(padded)
    starts = ends - padded
    dest = (starts[idx] + rank).astype(I32)
    n_steps = (n * TOP_K) // tm + N_EXPERTS
    tok = jnp.repeat(jnp.arange(n, dtype=I32), TOP_K)
    src = jnp.zeros((n_steps * tm,), I32).at[dest.reshape(-1)].set(
        tok, unique_indices=True, indices_are_sorted=False, mode="promise_in_bounds")
    tile_start = jnp.arange(n_steps, dtype=I32) * tm
    tile_expert = jnp.sum((ends[None, :] <= tile_start[:, None]).astype(I32), axis=1)
    tile_expert = jnp.minimum(tile_expert, N_EXPERTS - 1)
    return dest, src.reshape(n_steps, 1, tm), tile_expert


def kernel(x, c, positions, ada_w, ada_b, norm_mix, norm_ffn, w_in, mla_q_norm, mla_w_uq, mla_kv_norm,
           mla_w_ukv, fox_f_bias, conv_w, diff_lam_q1, diff_lam_k1, diff_lam_q2, diff_lam_k2, diff_subln,
           w_branch, w_gate, b_gate, w_out, w_router, router_bias, w_exp_gate, w_exp_up, w_exp_down,
           w_sh_gate, w_sh_up, w_sh_down, final_norm):
    batch, seq, d = x.shape
    n = batch * seq
    depth = ada_w.shape[0]
    xf = x.reshape(n, d)

    mod = _modulation(c, ada_w, ada_b)
    mla_cos, mla_sin, diff_cos, diff_sin = _rope_tables(positions)
    p_scale = _in_proj_scale()

    for l in range(depth):
        lam_init = 0.8 - 0.6 * math.exp(-0.3 * l)
        mods = mod[l, :batch].reshape(batch, 6, 1, d)
        sh_m, sc_m, g_m, sh_f, sc_f, g_f = (mods[:, t] for t in range(6))

        w_g = w_gate[l].transpose(1, 0, 2).reshape(d, 4 * d).astype(BF16)
        w_all = jnp.concatenate([_in_proj_weight(w_in[l]), w_g], axis=1)
        epi = jnp.concatenate([p_scale, b_gate[l].reshape(1, 4 * d)], axis=1)
        p = _norm_proj(xf, norm_mix[l][None], sc_m, sh_m, w_all, epi, seq, scale_cols=P_WIDTH,
                       name="in_proj_gates")
        gates = p

        wq = jnp.pad(mla_w_uq[l].reshape(MLA_Q_RANK, MLA_HEADS, MLA_NOPE + MLA_ROPE),
                     ((0, 0), (0, 0), (0, LANES - MLA_NOPE - MLA_ROPE)))
        wq = wq.reshape(MLA_Q_RANK, MLA_HEADS * LANES).astype(BF16)
        wkv = mla_w_ukv[l].reshape(MLA_KV_RANK, MLA_HEADS, MLA_NOPE + MLA_V)
        wk = jnp.pad(wkv[:, :, :MLA_NOPE], ((0, 0), (0, 0), (0, LANES - MLA_NOPE)))
        wk = wk.reshape(MLA_KV_RANK, MLA_HEADS * LANES).astype(BF16)
        wv = wkv[:, :, MLA_NOPE:].reshape(MLA_KV_RANK, MLA_HEADS * MLA_V).astype(BF16)
        q_a, k_a, vt_a = _mla_prep(p, mla_cos, mla_sin, mla_q_norm[l][None], mla_kv_norm[l][None],
                                   wq, wk, wv, batch, seq)
        o_mla = _attention("mla", batch, seq, q=q_a, k=k_a, vt=vt_a)

        k_f, vt_f = _fox_prep(p, fox_f_bias[l], batch, seq)
        o_fox = _attention("fox", batch, seq, q=p, k=k_f, vt=vt_f, q_col=COL_FOXQ)

        o_conv = _short_conv(p, conv_w[l], seq)

        q_d, k_d, vt_d = _diff_prep(p, diff_cos, diff_sin, batch, seq)
        lam_vecs = jnp.stack([diff_lam_q1[l], diff_lam_k1[l], diff_lam_q2[l], diff_lam_k2[l]])
        o_diff = _attention("diff", batch, seq, q=q_d, k=k_d, vt=vt_d,
                            extra=(lam_vecs, diff_subln[l][None]), lam_init=lam_init)

        xf = _merge((o_mla, o_fox, o_conv, o_diff), gates, w_branch[l].astype(BF16),
                    w_out[l].astype(BF16), xf, g_m, seq)

        h, idx, wts, rank, counts = _router(xf, norm_ffn[l][None], sc_f, sh_f, w_router[l],
                                            router_bias[l][None], seq)
        dest, src, tile_expert = _dispatch_plan(idx, rank, counts, min(TM_EXP, n))
        y = _expert_ffn(l, tile_expert, src, h, w_exp_gate, w_exp_up, w_exp_down)
        shared = _shared_ffn(h, w_sh_gate[l].astype(BF16), w_sh_up[l].astype(BF16),
                             w_sh_down[l].astype(BF16))
        tc = min(TC_COMB, seq)
        dest_t = dest.reshape(n // tc, tc, TOP_K).transpose(0, 2, 1).reshape(n // tc, 1, TOP_K * tc)
        xf = _combine(dest_t, y, wts, shared, xf, g_f, final_norm[None], seq, final_norm=(l == depth - 1))

    return xf.reshape(batch, seq, d)
```
